```python
import jax, jax.numpy as jnp
from jax import lax
import numpy as np

D_MODEL = 1024
BATCH = 4
SEQ = 4096
DEPTH = 1
DEC_BATCH = 128
DEC_SEQ = 4
PAST_LEN = 8192
PAGE_SIZE = 128

GLA_HEADS = 4
GLA_DK = D_MODEL // 8
GLA_DV = D_MODEL // 4
GLA_RANK = 16
GLA_TAU = 16.0
GLA_CHUNK = 64
DIL_GROUPS = ((128, 1), (512, 4), (2048, 16))
N_DIL = 3
DIL_HEADS = 4
DIL_HD = D_MODEL // 8
MEM_LEN = 256
MEM_HEADS = 4
MEM_HD = D_MODEL // 8
ROPE_THETA = 10000.0
PEER_HEADS = 8
PEER_NKEYS = 128
PEER_N = PEER_NKEYS * PEER_NKEYS
PEER_DKEY = D_MODEL // 4
PEER_TOPK = 16
PEER_BLOCK = 128
N_BRANCH = 3
EPS = 1e-6
NEG = -1e30
IN_SIZES = (GLA_HEADS * GLA_DK, GLA_HEADS * GLA_DK, GLA_HEADS * GLA_DV, GLA_HEADS * GLA_DV,
            GLA_RANK, N_DIL * 3 * DIL_HEADS * DIL_HD, MEM_HEADS * MEM_HD, N_BRANCH * D_MODEL)
W_IN_COLS = sum(IN_SIZES)

kernel_name = "hybrid_gla_dilated_peer_step"


def rms_norm(x, g):
    xf = x.astype(jnp.float32)
    y = xf * lax.rsqrt(jnp.mean(xf * xf, axis=-1, keepdims=True) + EPS)
    return (y * g.astype(jnp.float32)).astype(x.dtype)


def rope(x, pos):
    hd = x.shape[-1]
    half = hd // 2
    inv = ROPE_THETA ** (-jnp.arange(half, dtype=jnp.float32) / half)
    ang = pos.astype(jnp.float32)[:, None] * inv[None, :]
    shp = (1, pos.shape[0]) + (1,) * (x.ndim - 3) + (half,)
    cos = jnp.cos(ang).reshape(shp)
    sin = jnp.sin(ang).reshape(shp)
    xf = x.astype(jnp.float32)
    x1, x2 = xf[..., :half], xf[..., half:]
    return jnp.concatenate([x1 * cos - x2 * sin, x2 * cos + x1 * sin], axis=-1).astype(x.dtype)


def masked_softmax(s, mask):
    s = jnp.where(mask, s, NEG)
    m = jnp.max(s, axis=-1, keepdims=True)
    e = jnp.exp(s - m)
    den = jnp.sum(e, axis=-1, keepdims=True)
    return e / den, (m + jnp.log(den))[..., 0]


def split_cols(z):
    idx, acc = [], 0
    for sz in IN_SIZES[:-1]:
        acc += sz
        idx.append(acc)
    return jnp.split(z, idx, axis=-1)


def gla_chunked(q, k, v, g, s0, chunk):
    B, S, H, dk = q.shape
    dv = v.shape[-1]
    nc = S // chunk

    def to_chunks(a):
        return a.reshape(B, nc, chunk, H, a.shape[-1]).transpose(1, 0, 3, 2, 4)

    tri = (jnp.arange(chunk)[:, None] >= jnp.arange(chunk)[None, :])[..., None]

    def step(st, inp):
        qc, kc, vc, gc = inp
        b = jnp.cumsum(gc.astype(jnp.float32), axis=2)
        diff = b[:, :, :, None, :] - b[:, :, None, :, :]
        decay = jnp.exp(jnp.where(tri, diff, 0.0)) * tri
        a = jnp.einsum('bhtd,bhsd,bhtsd->bhts', qc, kc, decay)
        o = (jnp.einsum('bhtd,bhde->bhte', qc * jnp.exp(b), st)
             + jnp.einsum('bhts,bhse->bhte', a, vc))
        bl = b[:, :, -1:, :]
        st_new = (jnp.exp(bl[:, :, 0, :])[..., None] * st
                  + jnp.einsum('bhsd,bhse->bhde', kc * jnp.exp(bl - b), vc))
        return st_new.astype(s0.dtype), o.astype(v.dtype)

    s_fin, o = lax.scan(step, s0, (to_chunks(q), to_chunks(k), to_chunks(v), to_chunks(g)))
    o = o.transpose(1, 0, 3, 2, 4).reshape(B, S, H, dv)
    return o, s_fin


def dilated_band(q, k, v, r, span):
    B, S, H, d = q.shape
    L = S // r
    nb = -(-L // span)
    Lp = nb * span

    def fold(a):
        a = a.reshape(B, L, r, H, d).transpose(0, 2, 1, 3, 4).reshape(B * r, L, H, d)
        a = jnp.pad(a, ((0, 0), (0, Lp - L), (0, 0), (0, 0)))
        return a.reshape(B * r, nb, span, H, d)

    def prev(a):
        return jnp.pad(a, ((0, 0), (1, 0), (0, 0), (0, 0), (0, 0)))[:, :-1]

    qb, kb, vb = fold(q), fold(k), fold(v)
    kk = jnp.concatenate([prev(kb), kb], axis=2)
    vv = jnp.concatenate([prev(vb), vb], axis=2)
    s = jnp.einsum('nbqhd,nbkhd->nbhqk', qb, kk).astype(jnp.float32) * (d ** -0.5)
    mq = jnp.arange(span)[:, None]
    mk = jnp.arange(2 * span)[None, :]
    dist = mq + span - mk
    band = (dist >= 0) & (dist <= span)
    blk = jnp.arange(nb)[:, None, None]
    mask = band[None] & ((blk > 0) | (mk >= span)[None])
    p, lse = masked_softmax(s, mask[None, :, None])
    o = jnp.einsum('nbhqk,nbkhd->nbqhd', p.astype(v.dtype), vv)
    o = o.reshape(B, r, Lp, H, d)[:, :, :L].transpose(0, 2, 1, 3, 4).reshape(B, S, H, d)
    lse = lse.transpose(0, 1, 3, 2).reshape(B, r, Lp, H)[:, :, :L].transpose(0, 2, 1, 3).reshape(B, S, H)
    return o, lse


def dilated_gather(q, k_buf, v_buf, k_new, v_new, r, span, window):
    Bd, T, H, d = q.shape
    wb = k_buf.shape[1]
    kk = jnp.concatenate([k_buf, k_new], axis=1)
    vv = jnp.concatenate([v_buf, v_new], axis=1)
    idx = wb + jnp.arange(T)[:, None] - r * jnp.arange(span + 1)[None, :]
    valid = idx >= 0
    idxc = jnp.maximum(idx, 0)
    kg = kk[:, idxc]
    vg = vv[:, idxc]
    s = jnp.einsum('bthd,btjhd->bhtj', q, kg).astype(jnp.float32) * (d ** -0.5)
    p, lse = masked_softmax(s, valid[None, None])
    o = jnp.einsum('bhtj,btjhd->bthd', p.astype(v_new.dtype), vg)
    keep = min(window, wb + T)
    return o, lse.transpose(0, 2, 1), kk[:, -keep:], vv[:, -keep:]


def peer(n, wq, keys, u_tab, v_tab):
    T, D = n.shape
    nblk = -(-T // PEER_BLOCK)
    xp = jnp.pad(n, ((0, nblk * PEER_BLOCK - T), (0, 0))).reshape(nblk, PEER_BLOCK, D)

    def blk(xb):
        qh = (xb @ wq).reshape(PEER_BLOCK, PEER_HEADS, 2, PEER_DKEY // 2)
        s = jnp.einsum('thcd,hckd->thck', qh, keys).astype(jnp.float32)
        sv, si = lax.top_k(s, PEER_TOPK)
        cand = (sv[:, :, 0, :, None] + sv[:, :, 1, None, :]).reshape(PEER_BLOCK, PEER_HEADS, -1)
        cidx = (si[:, :, 0, :, None] * PEER_NKEYS + si[:, :, 1, None, :]).reshape(PEER_BLOCK, PEER_HEADS, -1)
        sc, pos = lax.top_k(cand, PEER_TOPK)
        eidx = jnp.take_along_axis(cidx, pos, axis=-1)
        gw = jax.nn.softmax(sc, axis=-1)
        hpre = jnp.einsum('td,thkd->thk', xb, u_tab[eidx])
        act = (jax.nn.gelu(hpre.astype(jnp.float32), approximate=False) * gw).astype(xb.dtype)
        return jnp.einsum('thk,thkd->td', act, v_tab[eidx])

    out = lax.map(blk, xp).reshape(nblk * PEER_BLOCK, D)
    return out[:T]


def mem_kv(mem, mem_norm_g, w_mem_kv, mem_kn_g):
    B, M, _ = mem.shape
    kv = rms_norm(mem, mem_norm_g) @ w_mem_kv
    k, v = jnp.split(kv, 2, axis=-1)
    k = rms_norm(k.reshape(B, M, MEM_HEADS, MEM_HD), mem_kn_g)
    return k, v.reshape(B, M, MEM_HEADS, MEM_HD)


def layer(x, pos, mem_k, mem_v, gla_s0, gla_chunk, dil_bufs, w):
    (ln1_g, w_in, gla_wg2, gla_bg, gla_onorm_g, dil_qn_g, dil_kn_g, mem_qn_g,
     w_br_gla, w_br_dil, w_br_mem, w_out, ln2_g, peer_wq, peer_keys, peer_u, peer_v) = w
    B, S, D = x.shape
    n = rms_norm(x, ln1_g)
    q_g, k_g, v_g, r_g, a_g, dil_qkv, q_m, gates = split_cols(n @ w_in)

    qg = q_g.reshape(B, S, GLA_HEADS, GLA_DK) * (GLA_DK ** -0.5)
    kg = k_g.reshape(B, S, GLA_HEADS, GLA_DK)
    vg = v_g.reshape(B, S, GLA_HEADS, GLA_DV)
    lg = jax.nn.log_sigmoid((a_g @ gla_wg2 + gla_bg).astype(jnp.float32)) / GLA_TAU
    lg = lg.reshape(B, S, GLA_HEADS, GLA_DK)
    og, gla_state = gla_chunked(qg, kg, vg, lg, gla_s0, gla_chunk)
    og = rms_norm(og, gla_onorm_g) * jax.nn.silu(r_g.reshape(B, S, GLA_HEADS, GLA_DV))
    br_gla = og.reshape(B, S, -1) @ w_br_gla

    dq = dil_qkv.reshape(B, S, N_DIL, 3, DIL_HEADS, DIL_HD)
    qd = rope(rms_norm(dq[:, :, :, 0], dil_qn_g), pos)
    kd = rope(rms_norm(dq[:, :, :, 1], dil_kn_g), pos)
    vd = dq[:, :, :, 2]
    outs, lses, new_bufs = [], [], []
    for gi, (wd, r) in enumerate(DIL_GROUPS):
        span = wd // r
        qi, ki, vi = qd[:, :, gi], kd[:, :, gi], vd[:, :, gi]
        if dil_bufs is None:
            o, l = dilated_band(qi, ki, vi, r, span)
            keep = min(wd, S)
            nk, nv = ki[:, -keep:], vi[:, -keep:]
        else:
            kb, vb = dil_bufs[gi]
            o, l, nk, nv = dilated_gather(qi, kb, vb, ki, vi, r, span, wd)
        outs.append(o)
        lses.append(l)
        new_bufs.append((nk, nv))
    wts = jax.nn.softmax(jnp.stack(lses, axis=0), axis=0)
    od = jnp.sum(wts[..., None] * jnp.stack(outs, axis=0).astype(jnp.float32), axis=0).astype(x.dtype)
    br_dil = od.reshape(B, S, -1) @ w_br_dil

    qm = rms_norm(q_m.reshape(B, S, MEM_HEADS, MEM_HD), mem_qn_g)
    sm = jnp.einsum('bshd,bmhd->bhsm', qm, mem_k).astype(jnp.float32) * (MEM_HD ** -0.5)
    pm = jax.nn.softmax(sm, axis=-1).astype(x.dtype)
    om = jnp.einsum('bhsm,bmhd->bshd', pm, mem_v)
    br_mem = om.reshape(B, S, -1) @ w_br_mem

    gt = jax.nn.sigmoid(gates.reshape(B, S, N_BRANCH, D))
    merged = gt[:, :, 0] * br_gla + gt[:, :, 1] * br_dil + gt[:, :, 2] * br_mem
    h = x + merged @ w_out

    n2 = rms_norm(h, ln2_g)
    y = h + peer(n2.reshape(B * S, D), peer_wq, peer_keys, peer_u, peer_v).reshape(B, S, D)
    return y, gla_state, new_bufs


def setup_inputs(seed: int = 0) -> dict:
    key = jax.random.key(seed)
    ks = iter(jax.random.split(key, 64))
    f32 = jnp.float32

    def nrm(shape, scale=1.0):
        return jax.random.normal(next(ks), shape, f32) * scale

    def gain(dim):
        return 1.0 + 0.02 * jax.random.normal(next(ks), (dim,), f32)

    d = D_MODEL
    inp = {}
    inp["x_prompt"] = nrm((BATCH, SEQ, d))
    inp["x_sample"] = nrm((DEC_BATCH, DEC_SEQ, d))
    inp["mem_prompt"] = nrm((BATCH, MEM_LEN, d))
    inp["state_gla"] = nrm((DEC_BATCH, GLA_HEADS, GLA_DK, GLA_DV))
    for gi, (wd, r) in enumerate(DIL_GROUPS):
        wb = min(wd, PAST_LEN)
        inp["cache_dil_k%d" % gi] = nrm((DEC_BATCH, wb, DIL_HEADS, DIL_HD))
        inp["cache_dil_v%d" % gi] = nrm((DEC_BATCH, wb, DIL_HEADS, DIL_HD))
    inp["cache_mem_k"] = nrm((DEC_BATCH, MEM_LEN, MEM_HEADS, MEM_HD))
    inp["cache_mem_v"] = nrm((DEC_BATCH, MEM_LEN, MEM_HEADS, MEM_HD))
    inp["ln1_g"] = gain(d)
    inp["w_in"] = nrm((d, W_IN_COLS), d ** -0.5)
    inp["gla_wg2"] = nrm((GLA_RANK, GLA_HEADS * GLA_DK), GLA_RANK ** -0.5)
    inp["gla_bg"] = nrm((GLA_HEADS * GLA_DK,), 0.1)
    inp["gla_onorm_g"] = gain(GLA_DV)
    inp["dil_qn_g"] = gain(DIL_HD)
    inp["dil_kn_g"] = gain(DIL_HD)
    inp["mem_norm_g"] = gain(d)
    inp["w_mem_kv"] = nrm((d, 2 * MEM_HEADS * MEM_HD), d ** -0.5)
    inp["mem_qn_g"] = gain(MEM_HD)
    inp["mem_kn_g"] = gain(MEM_HD)
    inp["w_br_gla"] = nrm((GLA_HEADS * GLA_DV, d), (GLA_HEADS * GLA_DV) ** -0.5)
    inp["w_br_dil"] = nrm((DIL_HEADS * DIL_HD, d), (DIL_HEADS * DIL_HD) ** -0.5)
    inp["w_br_mem"] = nrm((MEM_HEADS * MEM_HD, d), (MEM_HEADS * MEM_HD) ** -0.5)
    inp["w_out"] = nrm((d, d), d ** -0.5)
    inp["ln2_g"] = gain(d)
    inp["peer_wq"] = nrm((d, PEER_HEADS * PEER_DKEY), d ** -0.5)
    inp["peer_keys"] = nrm((PEER_HEADS, 2, PEER_NKEYS, PEER_DKEY // 2), (PEER_DKEY // 2) ** -0.5)
    inp["peer_u"] = nrm((PEER_N, d), d ** -0.5)
    inp["peer_v"] = nrm((PEER_N, d), 0.5)
    return inp


def reference(x_prompt, x_sample, mem_prompt, state_gla,
              cache_dil_k0, cache_dil_v0, cache_dil_k1, cache_dil_v1, cache_dil_k2, cache_dil_v2,
              cache_mem_k, cache_mem_v,
              ln1_g, w_in, gla_wg2, gla_bg, gla_onorm_g, dil_qn_g, dil_kn_g,
              mem_norm_g, w_mem_kv, mem_qn_g, mem_kn_g,
              w_br_gla, w_br_dil, w_br_mem, w_out, ln2_g,
              peer_wq, peer_keys, peer_u, peer_v):
    w = (ln1_g, w_in, gla_wg2, gla_bg, gla_onorm_g, dil_qn_g, dil_kn_g, mem_qn_g,
         w_br_gla, w_br_dil, w_br_mem, w_out, ln2_g, peer_wq, peer_keys, peer_u, peer_v)
    B, S, _ = x_prompt.shape
    Bd, T, _ = x_sample.shape

    mem_k_p, mem_v_p = mem_kv(mem_prompt, mem_norm_g, w_mem_kv, mem_kn_g)
    s0 = jnp.zeros((B, GLA_HEADS, GLA_DK, GLA_DV), x_prompt.dtype)
    y_prompt = x_prompt
    for _ in range(DEPTH):
        y_prompt, gla_state_p, bufs_p = layer(y_prompt, jnp.arange(S, dtype=jnp.int32), mem_k_p, mem_v_p,
                                              s0, min(GLA_CHUNK, S), None, w)
    (dk0_p, dv0_p), (dk1_p, dv1_p), (dk2_p, dv2_p) = bufs_p

    bufs_in = ((cache_dil_k0, cache_dil_v0), (cache_dil_k1, cache_dil_v1), (cache_dil_k2, cache_dil_v2))
    pos_s = PAST_LEN + jnp.arange(T, dtype=jnp.int32)
    y_sample = x_sample
    for _ in range(DEPTH):
        y_sample, gla_state_s, bufs_s = layer(y_sample, pos_s, cache_mem_k, cache_mem_v,
                                              state_gla, T, bufs_in, w)
    (dk0_s, dv0_s), (dk1_s, dv1_s), (dk2_s, dv2_s) = bufs_s

    return (y_prompt, y_sample,
            gla_state_p, dk0_p, dv0_p, dk1_p, dv1_p, dk2_p, dv2_p, mem_k_p, mem_v_p,
            gla_state_s, dk0_s, dv0_s, dk1_s, dv1_s, dk2_s, dv2_s)
```

```python
import functools

import jax
import jax.numpy as jnp
from jax import lax
from jax.experimental import pallas as pl
from jax.experimental.pallas import tpu as pltpu

F32 = jnp.float32
BF16 = jnp.bfloat16

D_MODEL = 1024
PAST_LEN = 8192
GLA_HEADS = 4
GLA_DK = 128
GLA_DV = 256
GLA_RANK = 16
GLA_TAU = 16.0
DIL_GROUPS = ((128, 1), (512, 4), (2048, 16))
N_DIL = 3
HEADS = 4
HD = 128
SPAN = 128
MEM_LEN = 256
ROPE_THETA = 10000.0
PEER_HEADS = 8
PEER_NKEYS = 128
PEER_TOPK = 16
EPS = 1e-6
NEG = -1e30

LANES = 128
GLA_COLS = 2 * GLA_HEADS * GLA_DK + 2 * GLA_HEADS * GLA_DV
DIL_COLS = N_DIL * 3 * HEADS * HD
ATT_COLS = HEADS * HD
VMEM_LIMIT = 56 * 1024 * 1024

PEER_T = 256
PEER_EB = 1024
PEER_STRIDE = PEER_T + 8


def _cparams(*sem):
    return pltpu.CompilerParams(dimension_semantics=sem, vmem_limit_bytes=VMEM_LIMIT)


def _rms(x, g):
    return x * lax.rsqrt(jnp.mean(x * x, axis=-1, keepdims=True) + EPS) * g


def _dot(a, b):
    return jnp.dot(a, b, preferred_element_type=F32)


def _dot_nt(a, b):
    return lax.dot_general(a, b, (((1,), (1,)), ((), ())), preferred_element_type=F32)


def _split3(x):
    hi = x.astype(BF16)
    r1 = x - hi.astype(F32)
    mid = r1.astype(BF16)
    lo = (r1 - mid.astype(F32)).astype(BF16)
    return hi, mid, lo


def _proj_body(kind, *refs):
    if kind == "dil":
        x_ref, g_ref, w_ref, hg_ref, cos_ref, sin_ref, o_ref, n_scr = refs
    elif kind in ("mem_q", "mem_kv"):
        x_ref, g_ref, w_ref, hg_ref, o_ref, n_scr = refs
    else:
        x_ref, g_ref, w_ref, o_ref, n_scr = refs
    j = pl.program_id(1)

    @pl.when(j == 0)
    def _():
        n_scr[...] = _rms(x_ref[...], g_ref[...]).astype(BF16)

    z = _dot(n_scr[...], w_ref[...])
    if kind == "gla":
        o_ref[...] = z * jnp.where(j == 0, GLA_DK ** -0.5, 1.0).astype(F32)
    elif kind == "sigmoid":
        o_ref[...] = jax.nn.sigmoid(z)
    elif kind == "mem_q":
        hg = hg_ref[...]
        o_ref[...] = jnp.concatenate([_rms(z[:, h * HD:(h + 1) * HD], hg) for h in range(HEADS)], axis=-1)
    elif kind == "mem_kv":
        @pl.when(j == 0)
        def _():
            hg = hg_ref[...]
            o_ref[...] = jnp.concatenate([_rms(z[:, h * HD:(h + 1) * HD], hg) for h in range(HEADS)], axis=-1)

        @pl.when(j != 0)
        def _():
            o_ref[...] = z
    elif kind == "dil":
        c = j % 3

        @pl.when(c == 2)
        def _():
            o_ref[...] = z

        @pl.when(c != 2)
        def _():
            hg = jnp.where(c == 0, hg_ref[0:1, :], hg_ref[1:2, :])
            cs = cos_ref[...]
            sn = sin_ref[...]
            parts = []
            for h in range(HEADS):
                zh = _rms(z[:, h * HD:(h + 1) * HD], hg)
                parts.append(zh * cs + pltpu.roll(zh, HD // 2, 1) * sn)
            o_ref[...] = jnp.concatenate(parts, axis=-1)
    else:
        raise ValueError(kind)


def _proj(kind, x, g, w, extra=(), tm=512, tn=512):
    nt, d = x.shape
    ncol = w.shape[1]
    tm = min(tm, nt)
    assert nt % tm == 0 and ncol % tn == 0
    in_specs = [
        pl.BlockSpec((tm, d), lambda i, j: (i, 0)),
        pl.BlockSpec((1, d), lambda i, j: (0, 0)),
        pl.BlockSpec((d, tn), lambda i, j: (0, j)),
    ]
    if kind == "dil":
        in_specs += [
            pl.BlockSpec((2, HD), lambda i, j: (0, 0)),
            pl.BlockSpec((tm, HD), lambda i, j: (i, 0)),
            pl.BlockSpec((tm, HD), lambda i, j: (i, 0)),
        ]
    elif kind in ("mem_q", "mem_kv"):
        in_specs += [pl.BlockSpec((1, HD), lambda i, j: (0, 0))]
    return pl.pallas_call(
        functools.partial(_proj_body, kind),
        grid=(nt // tm, ncol // tn),
        in_specs=in_specs,
        out_specs=pl.BlockSpec((tm, tn), lambda i, j: (i, j)),
        out_shape=jax.ShapeDtypeStruct((nt, ncol), F32),
        scratch_shapes=[pltpu.VMEM((tm, d), BF16)],
        compiler_params=_cparams("parallel", "arbitrary"),
        name="proj_" + kind,
    )(x, g.reshape(1, d), w, *extra)


def _gate_body(x_ref, g_ref, wa_ref, wg2_ref, bg_ref, o_ref):
    n = _rms(x_ref[...], g_ref[...]).astype(BF16)
    a = _dot(n, wa_ref[...])
    pre = _dot(a.astype(BF16), wg2_ref[...]) + bg_ref[...]
    o_ref[...] = jax.nn.log_sigmoid(pre) / GLA_TAU


def _gla_gate(x, g, wa, wg2, bg, tm=512):
    nt, d = x.shape
    tm = min(tm, nt)
    ncol = wg2.shape[1]
    return pl.pallas_call(
        _gate_body,
        grid=(nt // tm,),
        in_specs=[
            pl.BlockSpec((tm, d), lambda i: (i, 0)),
            pl.BlockSpec((1, d), lambda i: (0, 0)),
            pl.BlockSpec((d, LANES), lambda i: (0, 0)),
            pl.BlockSpec((LANES, ncol), lambda i: (0, 0)),
            pl.BlockSpec((1, ncol), lambda i: (0, 0)),
        ],
        out_specs=pl.BlockSpec((tm, ncol), lambda i: (i, 0)),
        out_shape=jax.ShapeDtypeStruct((nt, ncol), F32),
        compiler_params=_cparams("parallel"),
        name="gla_gate",
    )(x, g.reshape(1, d), wa, wg2, bg.reshape(1, ncol))


GLA_CHUNK = 128


def _gla_body(tc, q_ref, k_ref, v_ref, g_ref, s0_ref, o_ref, sf_ref, s_scr, qp, kp, vp, gp):
    c = pl.program_id(2)
    P = GLA_CHUNK

    @pl.when(c == 0)
    def _():
        s_scr[...] = s0_ref[...]

    if tc == P:
        q, k, v, g = q_ref[...], k_ref[...], v_ref[...], g_ref[...]
    else:
        for pad, ref in ((qp, q_ref), (kp, k_ref), (vp, v_ref), (gp, g_ref)):
            pad[...] = jnp.zeros(pad.shape, F32)
            pad[0:tc, :] = ref[...]
        q, k, v, g = qp[...], kp[...], vp[...], gp[...]

    row = lax.broadcasted_iota(jnp.int32, (P, P), 0)
    col = lax.broadcasted_iota(jnp.int32, (P, P), 1)
    tri = row >= col
    trib = tri.astype(BF16)
    g1, g2, g3 = _split3(g)
    b = _dot(trib, g1) + _dot(trib, g2) + _dot(trib, g3)
    b_mid = b[P // 2 - 1:P // 2, :]
    b_end = b[P - 1:P, :]
    qe = (q * jnp.exp(b)).astype(BF16)
    qm = (q * jnp.exp(b - b_mid)).astype(BF16)
    km = (k * jnp.exp(b_mid - b)).astype(BF16)
    kd = k * jnp.exp(b_end - b)
    vb = v.astype(BF16)
    a = jnp.where(tri, _dot_nt(qm, km), 0.0)
    s = s_scr[...]
    o = _dot(qe, s.astype(BF16)) + _dot(a.astype(BF16), vb)
    o_ref[...] = o[0:tc, :]
    decay = jnp.exp(jnp.sum(jnp.transpose(g), axis=1, keepdims=True))
    s_new = decay * s + _dot(jnp.transpose(kd).astype(BF16), vb)
    s_scr[...] = s_new

    @pl.when(c == pl.num_programs(2) - 1)
    def _():
        sf_ref[...] = s_new


def _gla(zg, lg, s0, tc):
    B, S, _ = zg.shape
    H, dk, dv = GLA_HEADS, GLA_DK, GLA_DV
    assert S % tc == 0
    kblk = H
    vblk = 2 * H * dk // dv
    P = GLA_CHUNK
    return pl.pallas_call(
        functools.partial(_gla_body, tc),
        grid=(B, H, S // tc),
        in_specs=[
            pl.BlockSpec((None, tc, dk), lambda b, h, c: (b, c, h)),
            pl.BlockSpec((None, tc, dk), lambda b, h, c: (b, c, kblk + h)),
            pl.BlockSpec((None, tc, dv), lambda b, h, c: (b, c, vblk + h)),
            pl.BlockSpec((None, tc, dk), lambda b, h, c: (b, c, h)),
            pl.BlockSpec((None, None, dk, dv), lambda b, h, c: (b, h, 0, 0)),
        ],
        out_specs=[
            pl.BlockSpec((None, tc, dv), lambda b, h, c: (b, c, h)),
            pl.BlockSpec((None, None, dk, dv), lambda b, h, c: (b, h, 0, 0)),
        ],
        out_shape=[
            jax.ShapeDtypeStruct((B, S, H * dv), F32),
            jax.ShapeDtypeStruct((B, H, dk, dv), F32),
        ],
        scratch_shapes=[
            pltpu.VMEM((dk, dv), F32),
            pltpu.VMEM((P, dk), F32), pltpu.VMEM((P, dk), F32), pltpu.VMEM((P, dv), F32), pltpu.VMEM((P, dk), F32),
        ],
        compiler_params=_cparams("parallel", "parallel", "arbitrary"),
        name="gla",
    )(zg, zg, zg, lg, s0)


def _dil_body(q_ref, kc_ref, kp_ref, vc_ref, vp_ref, o_ref, l_ref):
    i = pl.program_id(2)
    scale = HD ** -0.5
    mq = lax.broadcasted_iota(jnp.int32, (SPAN, SPAN), 0)
    mk = lax.broadcasted_iota(jnp.int32, (SPAN, SPAN), 1)
    mask_c = mk <= mq
    mask_p = jnp.logical_and(mk >= mq, i > 0)
    for h in range(HEADS):
        sl = slice(h * HD, (h + 1) * HD)
        q = q_ref[:, sl].astype(BF16)
        sc = jnp.where(mask_c, _dot_nt(q, kc_ref[:, sl].astype(BF16)) * scale, NEG)
        sp = jnp.where(mask_p, _dot_nt(q, kp_ref[:, sl].astype(BF16)) * scale, NEG)
        m = jnp.maximum(jnp.max(sc, axis=-1, keepdims=True), jnp.max(sp, axis=-1, keepdims=True))
        ec = jnp.exp(sc - m)
        ep = jnp.exp(sp - m)
        den = jnp.sum(ec, axis=-1, keepdims=True) + jnp.sum(ep, axis=-1, keepdims=True)
        o = (_dot((ec / den).astype(BF16), vc_ref[:, sl].astype(BF16))
             + _dot((ep / den).astype(BF16), vp_ref[:, sl].astype(BF16)))
        o_ref[:, sl] = o
        l_ref[:, sl] = jnp.broadcast_to(m + jnp.log(den), (SPAN, HD))


def _dil_prompt(zd, gi):
    B, S, _ = zd.shape
    r = DIL_GROUPS[gi][1]
    L = S // r
    assert L % SPAN == 0
    nb = L // SPAN
    zv = zd.reshape(B, L, r * DIL_COLS)
    ncb = DIL_COLS // ATT_COLS
    base = gi * 3

    def spec(c, prev):
        if prev:
            return pl.BlockSpec((None, SPAN, ATT_COLS), lambda b, p, i: (b, jnp.maximum(i - 1, 0), p * ncb + base + c))
        return pl.BlockSpec((None, SPAN, ATT_COLS), lambda b, p, i: (b, i, p * ncb + base + c))

    out_spec = pl.BlockSpec((None, SPAN, ATT_COLS), lambda b, p, i: (b, i, p))
    o, lse = pl.pallas_call(
        _dil_body,
        grid=(B, r, nb),
        in_specs=[spec(0, False), spec(1, False), spec(1, True), spec(2, False), spec(2, True)],
        out_specs=[out_spec, out_spec],
        out_shape=[jax.ShapeDtypeStruct((B, L, r * ATT_COLS), F32)] * 2,
        compiler_params=_cparams("parallel", "parallel", "arbitrary"),
        name="dil_prompt%d" % gi,
    )(zv, zv, zv, zv, zv)
    return o.reshape(B * S, ATT_COLS), lse.reshape(B * S, ATT_COLS)


def _dil_mix_body(o0, l0, o1, l1, o2, l2, od_ref):
    a0, a1, a2 = l0[...], l1[...], l2[...]
    m = jnp.maximum(jnp.maximum(a0, a1), a2)
    e0, e1, e2 = jnp.exp(a0 - m), jnp.exp(a1 - m), jnp.exp(a2 - m)
    den = e0 + e1 + e2
    od_ref[...] = (e0 / den) * o0[...] + (e1 / den) * o1[...] + (e2 / den) * o2[...]


def _dil_mix(parts, tm=1024):
    nt = parts[0].shape[0]
    spec = pl.BlockSpec((tm, ATT_COLS), lambda i: (i, 0))
    return pl.pallas_call(
        _dil_mix_body,
        grid=(nt // tm,),
        in_specs=[spec] * 6,
        out_specs=spec,
        out_shape=jax.ShapeDtypeStruct((nt, ATT_COLS), F32),
        compiler_params=_cparams("parallel"),
        name="dil_mix",
    )(*parts)


def _dil_dec_body(T, z_ref, k0, v0, k1, v1, k2, v2, od_ref):
    scale = HD ** -0.5
    caches = ((k0, v0), (k1, v1), (k2, v2))
    lrow = lax.broadcasted_iota(jnp.int32, (SPAN, 1), 0)
    trow = lax.broadcasted_iota(jnp.int32, (T, 1), 0)
    for t in range(T):
        outs = [[None] * N_DIL for _ in range(HEADS)]
        lses = [[None] * N_DIL for _ in range(HEADS)]
        for gi, (_, r) in enumerate(DIL_GROUPS):
            base = gi * 3 * ATT_COLS
            q = z_ref[t:t + 1, base:base + ATT_COLS]
            kn = z_ref[:, base + ATT_COLS:base + 2 * ATT_COLS]
            vn = z_ref[:, base + 2 * ATT_COLS:base + 3 * ATT_COLS]
            kc, vc = caches[gi]
            if r == 1:
                kb, vb = kc[...], vc[...]
                mask_b = lrow >= t
                mask_n = trow <= t
            else:
                kb = kc[:, t * ATT_COLS:(t + 1) * ATT_COLS]
                vb = vc[:, t * ATT_COLS:(t + 1) * ATT_COLS]
                mask_b = lrow >= 0
                mask_n = trow == t
            pb = kb * q
            pn = kn * q
            for h in range(HEADS):
                sl = slice(h * HD, (h + 1) * HD)
                sb = jnp.where(mask_b, jnp.sum(pb[:, sl], axis=-1, keepdims=True) * scale, NEG)
                sn = jnp.where(mask_n, jnp.sum(pn[:, sl], axis=-1, keepdims=True) * scale, NEG)
                m = jnp.maximum(jnp.max(sb, axis=0, keepdims=True), jnp.max(sn, axis=0, keepdims=True))
                eb = jnp.exp(sb - m)
                en = jnp.exp(sn - m)
                den = jnp.sum(eb, axis=0, keepdims=True) + jnp.sum(en, axis=0, keepdims=True)
                acc = (jnp.sum((eb / den) * vb[:, sl], axis=0, keepdims=True)
                       + jnp.sum((en / den) * vn[:, sl], axis=0, keepdims=True))
                outs[h][gi] = acc
                lses[h][gi] = m + jnp.log(den)
        row = []
        for h in range(HEADS):
            l0, l1, l2 = lses[h]
            m = jnp.maximum(jnp.maximum(l0, l1), l2)
            e0, e1, e2 = jnp.exp(l0 - m), jnp.exp(l1 - m), jnp.exp(l2 - m)
            den = e0 + e1 + e2
            row.append((e0 / den) * outs[h][0] + (e1 / den) * outs[h][1] + (e2 / den) * outs[h][2])
        od_ref[t:t + 1, :] = jnp.concatenate(row, axis=-1)


def _dil_decode(zd, caches):
    Bd, T, _ = zd.shape
    args, specs = [zd], [pl.BlockSpec((None, T, DIL_COLS), lambda b: (b, 0, 0))]
    for (wd, r), (kc, vc) in zip(DIL_GROUPS, caches):
        assert kc.shape[1] == wd and T <= r or r == 1
        used = min(r, T) if r > 1 else 1
        for a in (kc, vc):
            args.append(a.reshape(Bd, wd // r, r * ATT_COLS))
            specs.append(pl.BlockSpec((None, wd // r, used * ATT_COLS), lambda b: (b, 0, 0)))
    return pl.pallas_call(
        functools.partial(_dil_dec_body, T),
        grid=(Bd,),
        in_specs=specs,
        out_specs=pl.BlockSpec((None, T, ATT_COLS), lambda b: (b, 0, 0)),
        out_shape=jax.ShapeDtypeStruct((Bd, T, ATT_COLS), F32),
        compiler_params=_cparams("parallel"),
        name="dil_decode",
    )(*args)


def _mem_body(q_ref, k_ref, v_ref, o_ref):
    scale = HD ** -0.5
    for h in range(HEADS):
        sl = slice(h * HD, (h + 1) * HD)
        s = _dot_nt(q_ref[:, sl].astype(BF16), k_ref[:, sl].astype(BF16)) * scale
        e = jnp.exp(s - jnp.max(s, axis=-1, keepdims=True))
        p = e / jnp.sum(e, axis=-1, keepdims=True)
        o_ref[:, sl] = _dot(p.astype(BF16), v_ref[:, sl].astype(BF16))


def _mem_attn(qm, mk, mv, tq):
    B, S, _ = qm.shape
    M = mk.shape[1]
    return pl.pallas_call(
        _mem_body,
        grid=(B, S // tq),
        in_specs=[
            pl.BlockSpec((None, tq, ATT_COLS), lambda b, i: (b, i, 0)),
            pl.BlockSpec((None, M, ATT_COLS), lambda b, i: (b, 0, 0)),
            pl.BlockSpec((None, M, ATT_COLS), lambda b, i: (b, 0, 0)),
        ],
        out_specs=pl.BlockSpec((None, tq, ATT_COLS), lambda b, i: (b, i, 0)),
        out_shape=jax.ShapeDtypeStruct((B, S, ATT_COLS), F32),
        compiler_params=_cparams("parallel", "arbitrary"),
        name="mem_attn",
    )(qm, mk, mv)


def _merge_body(x_ref, og_ref, r_ref, od_ref, om_ref, gt_ref, ong_ref, wbg_ref, wbd_ref, wbm_ref, wo_ref,
                ln2_ref, wq_ref, h_ref, n2_ref, qh_ref):
    og = og_ref[...]
    r = r_ref[...]
    ong = ong_ref[...]
    parts = []
    for h in range(GLA_HEADS):
        sl = slice(h * GLA_DV, (h + 1) * GLA_DV)
        parts.append(_rms(og[:, sl], ong) * jax.nn.silu(r[:, sl]))
    br_gla = _dot(jnp.concatenate(parts, axis=-1).astype(BF16), wbg_ref[...])
    br_dil = _dot(od_ref[...].astype(BF16), wbd_ref[...])
    br_mem = _dot(om_ref[...].astype(BF16), wbm_ref[...])
    d = D_MODEL
    merged = gt_ref[:, 0:d] * br_gla + gt_ref[:, d:2 * d] * br_dil + gt_ref[:, 2 * d:3 * d] * br_mem
    hres = x_ref[...] + _dot(merged.astype(BF16), wo_ref[...])
    h_ref[...] = hres
    n2 = _rms(hres, ln2_ref[...]).astype(BF16)
    n2_ref[...] = n2
    qh_ref[...] = _dot(n2, wq_ref[...])


def _merge(x, og, zg, od, om, gt, ong, wbg, wbd, wbm, wo, ln2, wq, tm=256):
    nt, d = x.shape
    tm = min(tm, nt)
    nq = wq.shape[1]
    rblk = (2 * GLA_HEADS * GLA_DK + GLA_HEADS * GLA_DV) // d

    def tok(cols, blk=0):
        return pl.BlockSpec((tm, cols), lambda i: (i, blk))

    def const(shape):
        return pl.BlockSpec(shape, lambda i: (0, 0), pipeline_mode=pl.Buffered(1))

    return pl.pallas_call(
        _merge_body,
        grid=(nt // tm,),
        in_specs=[
            tok(d), tok(d), tok(d, rblk), tok(ATT_COLS), tok(ATT_COLS), tok(3 * d),
            const((1, GLA_DV)), const(wbg.shape), const(wbd.shape), const(wbm.shape), const(wo.shape),
            const((1, d)), const(wq.shape),
        ],
        out_specs=[tok(d), tok(d), tok(nq)],
        out_shape=[
            jax.ShapeDtypeStruct((nt, d), F32),
            jax.ShapeDtypeStruct((nt, d), BF16),
            jax.ShapeDtypeStruct((nt, nq), F32),
        ],
        compiler_params=_cparams("parallel"),
        name="merge",
    )(x, og, zg, od, om, gt, ong.reshape(1, GLA_DV), wbg, wbd, wbm, wo, ln2.reshape(1, d), wq)


def _topk_rows(s, lane, nk):
    m_rows = s.shape[0]
    n = s.shape[1]
    lane_n = lax.broadcasted_iota(jnp.int32, s.shape, 1).astype(F32)

    def step(k, carry):
        s, sv, si = carry
        mx = jnp.max(s, axis=-1, keepdims=True)
        idx = jnp.min(jnp.where(s == mx, lane_n, float(n)), axis=-1, keepdims=True)
        sv = jnp.where(lane == k, mx, sv)
        si = jnp.where(lane == k, idx, si)
        s = jnp.where(lane_n == idx, -jnp.inf, s)
        return s, sv, si

    z = jnp.zeros((m_rows, LANES), F32)
    _, sv, si = lax.fori_loop(0, nk, step, (s, z, z))
    return sv, si


def _peer_topk_body(qh_ref, keys_ref, a_ref, b_ref, gw_ref):
    tp = qh_ref.shape[0]
    K = PEER_TOPK
    lane = lax.broadcasted_iota(jnp.int32, (tp, LANES), 1)
    rr = lax.broadcasted_iota(jnp.int32, (LANES, K * K), 0)
    cc = lax.broadcasted_iota(jnp.int32, (LANES, K * K), 1)
    r1 = (cc // K == rr).astype(BF16)
    r2 = (cc % K == rr).astype(BF16)
    lane_c = lax.broadcasted_iota(jnp.int32, (tp, K * K), 1).astype(F32)

    def spread(x, rmat):
        hi, mid, lo = _split3(x)
        return _dot(hi, rmat) + _dot(mid, rmat) + _dot(lo, rmat)

    a_out = jnp.zeros((tp, LANES), F32)
    b_out = jnp.zeros((tp, LANES), F32)
    g_out = jnp.zeros((tp, LANES), F32)
    for h in range(PEER_HEADS):
        svs, sis = [], []
        for c in range(2):
            hc = 2 * h + c
            s = _dot_nt(qh_ref[:, hc * LANES:(hc + 1) * LANES].astype(BF16), keys_ref[hc])
            sv, si = _topk_rows(s, lane, K)
            svs.append(sv)
            sis.append(si)
        cand = spread(svs[0], r1) + spread(svs[1], r2)
        ci1 = _dot(sis[0].astype(BF16), r1)
        ci2 = _dot(sis[1].astype(BF16), r2)

        def step(k, carry):
            cand, sc, av, bv = carry
            mx = jnp.max(cand, axis=-1, keepdims=True)
            pos = jnp.min(jnp.where(cand == mx, lane_c, float(K * K)), axis=-1, keepdims=True)
            hit = lane_c == pos
            a_k = jnp.sum(jnp.where(hit, ci1, 0.0), axis=-1, keepdims=True)
            b_k = jnp.sum(jnp.where(hit, ci2, 0.0), axis=-1, keepdims=True)
            sc = jnp.where(lane == k, mx, sc)
            av = jnp.where(lane == k, a_k, av)
            bv = jnp.where(lane == k, b_k, bv)
            cand = jnp.where(hit, -jnp.inf, cand)
            return cand, sc, av, bv

        z = jnp.zeros((tp, LANES), F32)
        _, sc, av, bv = lax.fori_loop(0, K, step, (cand, z, z, z))
        valid = lane < K
        scm = jnp.where(valid, sc, -jnp.inf)
        e = jnp.exp(scm - jnp.max(scm, axis=-1, keepdims=True))
        gw = e / jnp.sum(e, axis=-1, keepdims=True)
        here = lane // K == h
        if h:
            av, bv, gw = (pltpu.roll(t, h * K, 1) for t in (av, bv, gw))
        a_out = jnp.where(here, av, a_out)
        b_out = jnp.where(here, bv, b_out)
        g_out = jnp.where(here, gw, g_out)
    a_ref[...] = a_out
    b_ref[...] = b_out
    gw_ref[...] = g_out


def _peer_topk(qh, keys, tp=128):
    nt, nq = qh.shape
    tp = min(tp, nt)
    spec = pl.BlockSpec((tp, LANES), lambda i: (i, 0))
    shp = jax.ShapeDtypeStruct((nt, LANES), F32)
    return pl.pallas_call(
        _peer_topk_body,
        grid=(nt // tp,),
        in_specs=[
            pl.BlockSpec((tp, nq), lambda i: (i, 0)),
            pl.BlockSpec(keys.shape, lambda i: (0, 0, 0)),
        ],
        out_specs=[spec, spec, spec],
        out_shape=[shp, shp, shp],
        compiler_params=_cparams("parallel"),
        name="peer_topk",
    )(qh, keys)


def _peer_mix_body(n2_ref, a_ref, b_ref, gw_ref, h_ref, u_ref, v_ref, y_ref, w_scr, acc):
    j = pl.program_id(1)
    T = n2_ref.shape[0]
    NK = PEER_NKEYS
    rows = PEER_EB // NK

    @pl.when(j == 0)
    def _():
        acc[...] = jnp.zeros(acc.shape, F32)
        sub = lax.broadcasted_iota(jnp.int32, (NK, LANES), 0).astype(F32)

        def tok(t, carry):
            arow = a_ref[pl.ds(t, 1), :]
            brow = b_ref[pl.ds(t, 1), :]
            grow = gw_ref[pl.ds(t, 1), :]
            ghi = grow.astype(BF16).astype(F32)
            glo = grow - ghi
            oa = (arow == sub).astype(BF16)
            eqb = brow == sub
            lhs = jnp.concatenate([oa, oa], axis=1)
            rhs = jnp.concatenate([jnp.where(eqb, ghi, 0.0).astype(BF16), jnp.where(eqb, glo, 0.0).astype(BF16)], axis=1)
            w_scr[pl.ds(t, NK, stride=PEER_STRIDE), :] = _dot_nt(lhs, rhs)
            return carry

        lax.fori_loop(0, T, tok, 0)

    hpre = _dot_nt(n2_ref[...], u_ref[...])
    w = jnp.concatenate(
        [w_scr[pl.ds(pl.multiple_of((j * rows + ii) * PEER_STRIDE, 8), T), :] for ii in range(rows)], axis=1)
    gelu = 0.5 * hpre * (1.0 + lax.erf(hpre * (2.0 ** -0.5)))
    act = (gelu * w).astype(BF16)
    acc[...] += _dot(act, v_ref[...])

    @pl.when(j == pl.num_programs(1) - 1)
    def _():
        y_ref[...] = h_ref[...] + acc[...]


def _peer_mix(n2, a, b, gw, h, u, v):
    nt, d = n2.shape
    T = PEER_T
    assert nt % T == 0 and u.shape[0] % PEER_EB == 0
    tok = lambda cols: pl.BlockSpec((T, cols), lambda i, j: (i, 0))
    tab = pl.BlockSpec((PEER_EB, d), lambda i, j: (j, 0))
    return pl.pallas_call(
        _peer_mix_body,
        grid=(nt // T, u.shape[0] // PEER_EB),
        in_specs=[tok(d), tok(LANES), tok(LANES), tok(LANES), tok(d), tab, tab],
        out_specs=tok(d),
        out_shape=jax.ShapeDtypeStruct((nt, d), F32),
        scratch_shapes=[pltpu.VMEM((PEER_NKEYS * PEER_STRIDE, LANES), F32), pltpu.VMEM((T, d), F32)],
        compiler_params=_cparams("parallel", "arbitrary"),
        name="peer_mix",
    )(n2, a, b, gw, h, u, v)


def _rope_tables(pos):
    half = HD // 2
    inv = ROPE_THETA ** (-jnp.arange(half, dtype=F32) / half)
    ang = pos.astype(F32)[:, None] * inv[None, :]
    cos, sin = jnp.cos(ang), jnp.sin(ang)
    return jnp.concatenate([cos, cos], axis=-1), jnp.concatenate([-sin, sin], axis=-1)


def _prep_weights(w_in, gla_wg2, peer_wq, peer_keys, peer_u, peer_v, w_br_gla, w_br_dil, w_br_mem, w_out):
    c0 = GLA_COLS
    c1 = c0 + GLA_RANK
    c2 = c1 + DIL_COLS
    c3 = c2 + ATT_COLS
    wa = jnp.pad(w_in[:, c0:c1], ((0, 0), (0, LANES - GLA_RANK))).astype(BF16)
    wg2 = jnp.pad(gla_wg2, ((0, LANES - GLA_RANK), (0, 0))).astype(BF16)
    return dict(
        w_gla=w_in[:, :c0].astype(BF16), wa=wa, wg2=wg2,
        w_dil=w_in[:, c1:c2].astype(BF16), w_qm=w_in[:, c2:c3].astype(BF16), w_gt=w_in[:, c3:].astype(BF16),
        wbg=w_br_gla.astype(BF16), wbd=w_br_dil.astype(BF16), wbm=w_br_mem.astype(BF16), wo=w_out.astype(BF16),
        wq=peer_wq.astype(BF16),
        keys=peer_keys.reshape(PEER_HEADS * 2, PEER_NKEYS, LANES).astype(BF16),
        u=peer_u.astype(BF16), v=peer_v.astype(BF16),
    )


def _layer(x, pos, mem_k, mem_v, s0, caches, p, W):
    B, S, d = x.shape
    nt = B * S
    xt = x.reshape(nt, d)
    ln1 = p["ln1_g"]
    zg = _proj("gla", xt, ln1, W["w_gla"])
    lg = _gla_gate(xt, ln1, W["wa"], W["wg2"], p["gla_bg"])
    cosf, sinf = _rope_tables(pos)
    cosf = jnp.broadcast_to(cosf[None], (B, S, HD)).reshape(nt, HD)
    sinf = jnp.broadcast_to(sinf[None], (B, S, HD)).reshape(nt, HD)
    hg = jnp.stack([p["dil_qn_g"], p["dil_kn_g"]])
    zd = _proj("dil", xt, ln1, W["w_dil"], extra=(hg, cosf, sinf))
    qm = _proj("mem_q", xt, ln1, W["w_qm"], extra=(p["mem_qn_g"].reshape(1, HD),))
    gt = _proj("sigmoid", xt, ln1, W["w_gt"])

    prompt = caches is None
    og, s_fin = _gla(zg.reshape(B, S, GLA_COLS), lg.reshape(B, S, GLA_HEADS * GLA_DK), s0,
                     GLA_CHUNK if prompt else S)

    zd3 = zd.reshape(B, S, DIL_COLS)
    new_bufs = []
    if prompt:
        parts = []
        for gi, (wd, r) in enumerate(DIL_GROUPS):
            parts.extend(_dil_prompt(zd3, gi))
            keep = min(wd, S)
            kv = zd3[:, S - keep:, gi * 3 * ATT_COLS + ATT_COLS:(gi + 1) * 3 * ATT_COLS]
            new_bufs.append((kv[..., :ATT_COLS].reshape(B, keep, HEADS, HD), kv[..., ATT_COLS:].reshape(B, keep, HEADS, HD)))
        od = _dil_mix(parts)
    else:
        od = _dil_decode(zd3, caches).reshape(nt, ATT_COLS)
        for gi, (wd, r) in enumerate(DIL_GROUPS):
            kb, vb = caches[gi]
            kn = zd3[..., gi * 3 * ATT_COLS + ATT_COLS:gi * 3 * ATT_COLS + 2 * ATT_COLS].reshape(B, S, HEADS, HD)
            vn = zd3[..., gi * 3 * ATT_COLS + 2 * ATT_COLS:(gi + 1) * 3 * ATT_COLS].reshape(B, S, HEADS, HD)
            keep = min(wd, kb.shape[1] + S)
            new_bufs.append((jnp.concatenate([kb, kn], axis=1)[:, -keep:], jnp.concatenate([vb, vn], axis=1)[:, -keep:]))

    M = mem_k.shape[1]
    qm3 = qm.reshape(B, S, ATT_COLS)
    if prompt:
        om = _mem_attn(qm3, mem_k.reshape(B, M, ATT_COLS), mem_v.reshape(B, M, ATT_COLS), 512)
    else:
        sp = 16
        qp = jnp.pad(qm3, ((0, 0), (0, sp - S), (0, 0)))
        om = _mem_attn(qp, mem_k.reshape(B, M, ATT_COLS), mem_v.reshape(B, M, ATT_COLS), sp)[:, :S]
    om = om.reshape(nt, ATT_COLS)

    h, n2, qh = _merge(xt, og.reshape(nt, d), zg, od, om, gt, p["gla_onorm_g"], W["wbg"], W["wbd"], W["wbm"],
                       W["wo"], p["ln2_g"], W["wq"])
    a, b, gw = _peer_topk(qh, W["keys"])
    y = _peer_mix(n2, a, b, gw, h, W["u"], W["v"])
    return y.reshape(B, S, d), s_fin, new_bufs


def kernel(x_prompt, x_sample, mem_prompt, state_gla, cache_dil_k0, cache_dil_v0, cache_dil_k1, cache_dil_v1, cache_dil_k2, cache_dil_v2, cache_mem_k, cache_mem_v, ln1_g, w_in, gla_wg2, gla_bg, gla_onorm_g, dil_qn_g, dil_kn_g, mem_norm_g, w_mem_kv, mem_qn_g, mem_kn_g, w_br_gla, w_br_dil, w_br_mem, w_out, ln2_g, peer_wq, peer_keys, peer_u, peer_v):
    B, S, d = x_prompt.shape
    Bd, T, _ = x_sample.shape
    p = dict(ln1_g=ln1_g, gla_bg=gla_bg, gla_onorm_g=gla_onorm_g, dil_qn_g=dil_qn_g, dil_kn_g=dil_kn_g,
             mem_qn_g=mem_qn_g, ln2_g=ln2_g)
    W = _prep_weights(w_in, gla_wg2, peer_wq, peer_keys, peer_u, peer_v, w_br_gla, w_br_dil, w_br_mem, w_out)

    M = mem_prompt.shape[1]
    mkv = _proj("mem_kv", mem_prompt.reshape(B * M, d), mem_norm_g, w_mem_kv.astype(BF16),
                extra=(mem_kn_g.reshape(1, HD),))
    mem_k_p = mkv[:, :ATT_COLS].reshape(B, M, HEADS, HD)
    mem_v_p = mkv[:, ATT_COLS:].reshape(B, M, HEADS, HD)

    s0 = jnp.zeros((B, GLA_HEADS, GLA_DK, GLA_DV), F32)
    y_prompt, gla_state_p, bufs_p = _layer(x_prompt, jnp.arange(S, dtype=jnp.int32), mem_k_p, mem_v_p, s0, None, p, W)

    caches = ((cache_dil_k0, cache_dil_v0), (cache_dil_k1, cache_dil_v1), (cache_dil_k2, cache_dil_v2))
    pos_s = PAST_LEN + jnp.arange(T, dtype=jnp.int32)
    y_sample, gla_state_s, bufs_s = _layer(x_sample, pos_s, cache_mem_k, cache_mem_v, state_gla, caches, p, W)

    (dk0_p, dv0_p), (dk1_p, dv1_p), (dk2_p, dv2_p) = bufs_p
    (dk0_s, dv0_s), (dk1_s, dv1_s), (dk2_s, dv2_s) = bufs_s
    return (y_prompt, y_sample,
            gla_state_p, dk0_p, dv0_p, dk1_p, dv1_p, dk2_p, dv2_p, mem_k_p, mem_v_p,
            gla_state_s, dk0_s, dv0_s, dk1_s, dv1_s, dk2_s, dv2_s)
```

```python
import functools

import jax
import jax.numpy as jnp
from jax import lax
from jax.experimental import pallas as pl
from jax.experimental.pallas import tpu as pltpu

F32 = jnp.float32
BF16 = jnp.bfloat16

D_MODEL = 1024
PAST_LEN = 8192
GLA_HEADS = 4
GLA_DK = 128
GLA_DV = 256
GLA_RANK = 16
GLA_TAU = 16.0
DIL_GROUPS = ((128, 1), (512, 4), (2048, 16))
N_DIL = 3
HEADS = 4
HD = 128
SPAN = 128
MEM_LEN = 256
ROPE_THETA = 10000.0
PEER_HEADS = 8
PEER_NKEYS = 128
PEER_TOPK = 16
EPS = 1e-6
NEG = -1e30

LANES = 128
GLA_COLS = 2 * GLA_HEADS * GLA_DK + 2 * GLA_HEADS * GLA_DV
DIL_COLS = N_DIL * 3 * HEADS * HD
ATT_COLS = HEADS * HD
VMEM_LIMIT = 56 * 1024 * 1024

PEER_T = 1024
PEER_EB = 1024


def _cparams(*sem):
    return pltpu.CompilerParams(dimension_semantics=sem, vmem_limit_bytes=VMEM_LIMIT)


def _rms(x, g):
    return x * lax.rsqrt(jnp.mean(x * x, axis=-1, keepdims=True) + EPS) * g


def _dot(a, b):
    return jnp.dot(a, b, preferred_element_type=F32)


def _dot_nt(a, b):
    return lax.dot_general(a, b, (((1,), (1,)), ((), ())), preferred_element_type=F32)


def _split3(x):
    hi = x.astype(BF16)
    r1 = x - hi.astype(F32)
    mid = r1.astype(BF16)
    lo = (r1 - mid.astype(F32)).astype(BF16)
    return hi, mid, lo


def _proj_body(kind, *refs):
    if kind == "dil":
        x_ref, g_ref, w_ref, hg_ref, cos_ref, sin_ref, o_ref, n_scr = refs
    elif kind in ("mem_q", "mem_kv"):
        x_ref, g_ref, w_ref, hg_ref, o_ref, n_scr = refs
    else:
        x_ref, g_ref, w_ref, o_ref, n_scr = refs
    j = pl.program_id(1)

    @pl.when(j == 0)
    def _():
        n_scr[...] = _rms(x_ref[...], g_ref[...]).astype(BF16)

    z = _dot(n_scr[...], w_ref[...])
    if kind == "gla":
        o_ref[...] = z * jnp.where(j == 0, GLA_DK ** -0.5, 1.0).astype(F32)
    elif kind == "sigmoid":
        o_ref[...] = jax.nn.sigmoid(z)
    elif kind == "mem_q":
        hg = hg_ref[...]
        o_ref[...] = jnp.concatenate([_rms(z[:, h * HD:(h + 1) * HD], hg) for h in range(HEADS)], axis=-1)
    elif kind == "mem_kv":
        @pl.when(j == 0)
        def _():
            hg = hg_ref[...]
            o_ref[...] = jnp.concatenate([_rms(z[:, h * HD:(h + 1) * HD], hg) for h in range(HEADS)], axis=-1)

        @pl.when(j != 0)
        def _():
            o_ref[...] = z
    elif kind == "dil":
        c = j % 3

        @pl.when(c == 2)
        def _():
            o_ref[...] = z

        @pl.when(c != 2)
        def _():
            hg = jnp.where(c == 0, hg_ref[0:1, :], hg_ref[1:2, :])
            cs = cos_ref[...]
            sn = sin_ref[...]
            parts = []
            for h in range(HEADS):
                zh = _rms(z[:, h * HD:(h + 1) * HD], hg)
                parts.append(zh * cs + pltpu.roll(zh, HD // 2, 1) * sn)
            o_ref[...] = jnp.concatenate(parts, axis=-1)
    else:
        raise ValueError(kind)


def _proj(kind, x, g, w, extra=(), tm=512, tn=512):
    nt, d = x.shape
    ncol = w.shape[1]
    tm = min(tm, nt)
    assert nt % tm == 0 and ncol % tn == 0
    in_specs = [
        pl.BlockSpec((tm, d), lambda i, j: (i, 0)),
        pl.BlockSpec((1, d), lambda i, j: (0, 0)),
        pl.BlockSpec((d, tn), lambda i, j: (0, j)),
    ]
    if kind == "dil":
        in_specs += [
            pl.BlockSpec((2, HD), lambda i, j: (0, 0)),
            pl.BlockSpec((tm, HD), lambda i, j: (i, 0)),
            pl.BlockSpec((tm, HD), lambda i, j: (i, 0)),
        ]
    elif kind in ("mem_q", "mem_kv"):
        in_specs += [pl.BlockSpec((1, HD), lambda i, j: (0, 0))]
    return pl.pallas_call(
        functools.partial(_proj_body, kind),
        grid=(nt // tm, ncol // tn),
        in_specs=in_specs,
        out_specs=pl.BlockSpec((tm, tn), lambda i, j: (i, j)),
        out_shape=jax.ShapeDtypeStruct((nt, ncol), F32),
        scratch_shapes=[pltpu.VMEM((tm, d), BF16)],
        compiler_params=_cparams("parallel", "arbitrary"),
        name="proj_" + kind,
    )(x, g.reshape(1, d), w, *extra)


def _gate_body(x_ref, g_ref, wa_ref, wg2_ref, bg_ref, o_ref):
    n = _rms(x_ref[...], g_ref[...]).astype(BF16)
    a = _dot(n, wa_ref[...])
    pre = _dot(a.astype(BF16), wg2_ref[...]) + bg_ref[...]
    o_ref[...] = jax.nn.log_sigmoid(pre) / GLA_TAU


def _gla_gate(x, g, wa, wg2, bg, tm=512):
    nt, d = x.shape
    tm = min(tm, nt)
    ncol = wg2.shape[1]
    return pl.pallas_call(
        _gate_body,
        grid=(nt // tm,),
        in_specs=[
            pl.BlockSpec((tm, d), lambda i: (i, 0)),
            pl.BlockSpec((1, d), lambda i: (0, 0)),
            pl.BlockSpec((d, LANES), lambda i: (0, 0)),
            pl.BlockSpec((LANES, ncol), lambda i: (0, 0)),
            pl.BlockSpec((1, ncol), lambda i: (0, 0)),
        ],
        out_specs=pl.BlockSpec((tm, ncol), lambda i: (i, 0)),
        out_shape=jax.ShapeDtypeStruct((nt, ncol), F32),
        compiler_params=_cparams("parallel"),
        name="gla_gate",
    )(x, g.reshape(1, d), wa, wg2, bg.reshape(1, ncol))


GLA_CHUNK = 128


def _gla_body(tc, q_ref, k_ref, v_ref, g_ref, s0_ref, o_ref, sf_ref, s_scr, qp, kp, vp, gp):
    c = pl.program_id(2)
    P = GLA_CHUNK

    @pl.when(c == 0)
    def _():
        s_scr[...] = s0_ref[...]

    if tc == P:
        q, k, v, g = q_ref[...], k_ref[...], v_ref[...], g_ref[...]
    else:
        for pad, ref in ((qp, q_ref), (kp, k_ref), (vp, v_ref), (gp, g_ref)):
            pad[...] = jnp.zeros(pad.shape, F32)
            pad[0:tc, :] = ref[...]
        q, k, v, g = qp[...], kp[...], vp[...], gp[...]

    row = lax.broadcasted_iota(jnp.int32, (P, P), 0)
    col = lax.broadcasted_iota(jnp.int32, (P, P), 1)
    tri = row >= col
    trib = tri.astype(BF16)
    g1, g2, g3 = _split3(g)
    b = _dot(trib, g1) + _dot(trib, g2) + _dot(trib, g3)
    b_mid = b[P // 2 - 1:P // 2, :]
    b_end = b[P - 1:P, :]
    qe = (q * jnp.exp(b)).astype(BF16)
    qm = (q * jnp.exp(b - b_mid)).astype(BF16)
    km = (k * jnp.exp(b_mid - b)).astype(BF16)
    kd = k * jnp.exp(b_end - b)
    vb = v.astype(BF16)
    a = jnp.where(tri, _dot_nt(qm, km), 0.0)
    s = s_scr[...]
    o = _dot(qe, s.astype(BF16)) + _dot(a.astype(BF16), vb)
    o_ref[...] = o[0:tc, :]
    decay = jnp.exp(jnp.sum(jnp.transpose(g), axis=1, keepdims=True))
    s_new = decay * s + _dot(jnp.transpose(kd).astype(BF16), vb)
    s_scr[...] = s_new

    @pl.when(c == pl.num_programs(2) - 1)
    def _():
        sf_ref[...] = s_new


def _gla(zg, lg, s0, tc):
    B, S, _ = zg.shape
    H, dk, dv = GLA_HEADS, GLA_DK, GLA_DV
    assert S % tc == 0
    kblk = H
    vblk = 2 * H * dk // dv
    P = GLA_CHUNK
    return pl.pallas_call(
        functools.partial(_gla_body, tc),
        grid=(B, H, S // tc),
        in_specs=[
            pl.BlockSpec((None, tc, dk), lambda b, h, c: (b, c, h)),
            pl.BlockSpec((None, tc, dk), lambda b, h, c: (b, c, kblk + h)),
            pl.BlockSpec((None, tc, dv), lambda b, h, c: (b, c, vblk + h)),
            pl.BlockSpec((None, tc, dk), lambda b, h, c: (b, c, h)),
            pl.BlockSpec((None, None, dk, dv), lambda b, h, c: (b, h, 0, 0)),
        ],
        out_specs=[
            pl.BlockSpec((None, tc, dv), lambda b, h, c: (b, c, h)),
            pl.BlockSpec((None, None, dk, dv), lambda b, h, c: (b, h, 0, 0)),
        ],
        out_shape=[
            jax.ShapeDtypeStruct((B, S, H * dv), F32),
            jax.ShapeDtypeStruct((B, H, dk, dv), F32),
        ],
        scratch_shapes=[
            pltpu.VMEM((dk, dv), F32),
            pltpu.VMEM((P, dk), F32), pltpu.VMEM((P, dk), F32), pltpu.VMEM((P, dv), F32), pltpu.VMEM((P, dk), F32),
        ],
        compiler_params=_cparams("parallel", "parallel", "arbitrary"),
        name="gla",
    )(zg, zg, zg, lg, s0)


def _dil_body(q_ref, kc_ref, kp_ref, vc_ref, vp_ref, o_ref, l_ref):
    i = pl.program_id(2)
    scale = HD ** -0.5
    mq = lax.broadcasted_iota(jnp.int32, (SPAN, SPAN), 0)
    mk = lax.broadcasted_iota(jnp.int32, (SPAN, SPAN), 1)
    mask_c = mk <= mq
    mask_p = jnp.logical_and(mk >= mq, i > 0)
    for h in range(HEADS):
        sl = slice(h * HD, (h + 1) * HD)
        q = q_ref[:, sl].astype(BF16)
        sc = jnp.where(mask_c, _dot_nt(q, kc_ref[:, sl].astype(BF16)) * scale, NEG)
        sp = jnp.where(mask_p, _dot_nt(q, kp_ref[:, sl].astype(BF16)) * scale, NEG)
        m = jnp.maximum(jnp.max(sc, axis=-1, keepdims=True), jnp.max(sp, axis=-1, keepdims=True))
        ec = jnp.exp(sc - m)
        ep = jnp.exp(sp - m)
        den = jnp.sum(ec, axis=-1, keepdims=True) + jnp.sum(ep, axis=-1, keepdims=True)
        o = (_dot((ec / den).astype(BF16), vc_ref[:, sl].astype(BF16))
             + _dot((ep / den).astype(BF16), vp_ref[:, sl].astype(BF16)))
        o_ref[:, sl] = o
        l_ref[:, sl] = jnp.broadcast_to(m + jnp.log(den), (SPAN, HD))


def _dil_prompt(zd, gi):
    B, S, _ = zd.shape
    r = DIL_GROUPS[gi][1]
    L = S // r
    assert L % SPAN == 0
    nb = L // SPAN
    zv = zd.reshape(B, L, r * DIL_COLS)
    ncb = DIL_COLS // ATT_COLS
    base = gi * 3

    def spec(c, prev):
        if prev:
            return pl.BlockSpec((None, SPAN, ATT_COLS), lambda b, p, i: (b, jnp.maximum(i - 1, 0), p * ncb + base + c))
        return pl.BlockSpec((None, SPAN, ATT_COLS), lambda b, p, i: (b, i, p * ncb + base + c))

    out_spec = pl.BlockSpec((None, SPAN, ATT_COLS), lambda b, p, i: (b, i, p))
    o, lse = pl.pallas_call(
        _dil_body,
        grid=(B, r, nb),
        in_specs=[spec(0, False), spec(1, False), spec(1, True), spec(2, False), spec(2, True)],
        out_specs=[out_spec, out_spec],
        out_shape=[jax.ShapeDtypeStruct((B, L, r * ATT_COLS), F32)] * 2,
        compiler_params=_cparams("parallel", "parallel", "arbitrary"),
        name="dil_prompt%d" % gi,
    )(zv, zv, zv, zv, zv)
    return o.reshape(B * S, ATT_COLS), lse.reshape(B * S, ATT_COLS)


def _dil_mix_body(o0, l0, o1, l1, o2, l2, od_ref):
    a0, a1, a2 = l0[...], l1[...], l2[...]
    m = jnp.maximum(jnp.maximum(a0, a1), a2)
    e0, e1, e2 = jnp.exp(a0 - m), jnp.exp(a1 - m), jnp.exp(a2 - m)
    den = e0 + e1 + e2
    od_ref[...] = (e0 / den) * o0[...] + (e1 / den) * o1[...] + (e2 / den) * o2[...]


def _dil_mix(parts, tm=1024):
    nt = parts[0].shape[0]
    spec = pl.BlockSpec((tm, ATT_COLS), lambda i: (i, 0))
    return pl.pallas_call(
        _dil_mix_body,
        grid=(nt // tm,),
        in_specs=[spec] * 6,
        out_specs=spec,
        out_shape=jax.ShapeDtypeStruct((nt, ATT_COLS), F32),
        compiler_params=_cparams("parallel"),
        name="dil_mix",
    )(*parts)


def _dil_dec_body(T, z_ref, k0, v0, k1, v1, k2, v2, od_ref):
    scale = HD ** -0.5
    caches = ((k0, v0), (k1, v1), (k2, v2))
    lrow = lax.broadcasted_iota(jnp.int32, (SPAN, 1), 0)
    trow = lax.broadcasted_iota(jnp.int32, (T, 1), 0)
    for t in range(T):
        outs = [[None] * N_DIL for _ in range(HEADS)]
        lses = [[None] * N_DIL for _ in range(HEADS)]
        for gi, (_, r) in enumerate(DIL_GROUPS):
            base = gi * 3 * ATT_COLS
            q = z_ref[t:t + 1, base:base + ATT_COLS]
            kn = z_ref[:, base + ATT_COLS:base + 2 * ATT_COLS]
            vn = z_ref[:, base + 2 * ATT_COLS:base + 3 * ATT_COLS]
            kc, vc = caches[gi]
            if r == 1:
                kb, vb = kc[...], vc[...]
                mask_b = lrow >= t
                mask_n = trow <= t
            else:
                kb = kc[:, t * ATT_COLS:(t + 1) * ATT_COLS]
                vb = vc[:, t * ATT_COLS:(t + 1) * ATT_COLS]
                mask_b = lrow >= 0
                mask_n = trow == t
            pb = kb * q
            pn = kn * q
            for h in range(HEADS):
                sl = slice(h * HD, (h + 1) * HD)
                sb = jnp.where(mask_b, jnp.sum(pb[:, sl], axis=-1, keepdims=True) * scale, NEG)
                sn = jnp.where(mask_n, jnp.sum(pn[:, sl], axis=-1, keepdims=True) * scale, NEG)
                m = jnp.maximum(jnp.max(sb, axis=0, keepdims=True), jnp.max(sn, axis=0, keepdims=True))
                eb = jnp.exp(sb - m)
                en = jnp.exp(sn - m)
                den = jnp.sum(eb, axis=0, keepdims=True) + jnp.sum(en, axis=0, keepdims=True)
                acc = (jnp.sum((eb / den) * vb[:, sl], axis=0, keepdims=True)
                       + jnp.sum((en / den) * vn[:, sl], axis=0, keepdims=True))
                outs[h][gi] = acc
                lses[h][gi] = m + jnp.log(den)
        row = []
        for h in range(HEADS):
            l0, l1, l2 = lses[h]
            m = jnp.maximum(jnp.maximum(l0, l1), l2)
            e0, e1, e2 = jnp.exp(l0 - m), jnp.exp(l1 - m), jnp.exp(l2 - m)
            den = e0 + e1 + e2
            row.append((e0 / den) * outs[h][0] + (e1 / den) * outs[h][1] + (e2 / den) * outs[h][2])
        od_ref[t:t + 1, :] = jnp.concatenate(row, axis=-1)


def _dil_decode(zd, caches):
    Bd, T, _ = zd.shape
    args, specs = [zd], [pl.BlockSpec((None, T, DIL_COLS), lambda b: (b, 0, 0))]
    for (wd, r), (kc, vc) in zip(DIL_GROUPS, caches):
        assert kc.shape[1] == wd and T <= r or r == 1
        used = min(r, T) if r > 1 else 1
        for a in (kc, vc):
            args.append(a.reshape(Bd, wd // r, r * ATT_COLS))
            specs.append(pl.BlockSpec((None, wd // r, used * ATT_COLS), lambda b: (b, 0, 0)))
    return pl.pallas_call(
        functools.partial(_dil_dec_body, T),
        grid=(Bd,),
        in_specs=specs,
        out_specs=pl.BlockSpec((None, T, ATT_COLS), lambda b: (b, 0, 0)),
        out_shape=jax.ShapeDtypeStruct((Bd, T, ATT_COLS), F32),
        compiler_params=_cparams("parallel"),
        name="dil_decode",
    )(*args)


def _mem_body(q_ref, k_ref, v_ref, o_ref):
    scale = HD ** -0.5
    for h in range(HEADS):
        sl = slice(h * HD, (h + 1) * HD)
        s = _dot_nt(q_ref[:, sl].astype(BF16), k_ref[:, sl].astype(BF16)) * scale
        e = jnp.exp(s - jnp.max(s, axis=-1, keepdims=True))
        p = e / jnp.sum(e, axis=-1, keepdims=True)
        o_ref[:, sl] = _dot(p.astype(BF16), v_ref[:, sl].astype(BF16))


def _mem_attn(qm, mk, mv, tq):
    B, S, _ = qm.shape
    M = mk.shape[1]
    return pl.pallas_call(
        _mem_body,
        grid=(B, S // tq),
        in_specs=[
            pl.BlockSpec((None, tq, ATT_COLS), lambda b, i: (b, i, 0)),
            pl.BlockSpec((None, M, ATT_COLS), lambda b, i: (b, 0, 0)),
            pl.BlockSpec((None, M, ATT_COLS), lambda b, i: (b, 0, 0)),
        ],
        out_specs=pl.BlockSpec((None, tq, ATT_COLS), lambda b, i: (b, i, 0)),
        out_shape=jax.ShapeDtypeStruct((B, S, ATT_COLS), F32),
        compiler_params=_cparams("parallel", "arbitrary"),
        name="mem_attn",
    )(qm, mk, mv)


def _merge_body(x_ref, og_ref, r_ref, od_ref, om_ref, gt_ref, ong_ref, wbg_ref, wbd_ref, wbm_ref, wo_ref,
                ln2_ref, wq_ref, h_ref, n2_ref, qh_ref):
    og = og_ref[...]
    r = r_ref[...]
    ong = ong_ref[...]
    parts = []
    for h in range(GLA_HEADS):
        sl = slice(h * GLA_DV, (h + 1) * GLA_DV)
        parts.append(_rms(og[:, sl], ong) * jax.nn.silu(r[:, sl]))
    br_gla = _dot(jnp.concatenate(parts, axis=-1).astype(BF16), wbg_ref[...])
    br_dil = _dot(od_ref[...].astype(BF16), wbd_ref[...])
    br_mem = _dot(om_ref[...].astype(BF16), wbm_ref[...])
    d = D_MODEL
    merged = gt_ref[:, 0:d] * br_gla + gt_ref[:, d:2 * d] * br_dil + gt_ref[:, 2 * d:3 * d] * br_mem
    hres = x_ref[...] + _dot(merged.astype(BF16), wo_ref[...])
    h_ref[...] = hres
    n2 = _rms(hres, ln2_ref[...]).astype(BF16)
    n2_ref[...] = n2
    qh_ref[...] = _dot(n2, wq_ref[...])


def _merge(x, og, zg, od, om, gt, ong, wbg, wbd, wbm, wo, ln2, wq, tm=256):
    nt, d = x.shape
    tm = min(tm, nt)
    nq = wq.shape[1]
    rblk = (2 * GLA_HEADS * GLA_DK + GLA_HEADS * GLA_DV) // d

    def tok(cols, blk=0):
        return pl.BlockSpec((tm, cols), lambda i: (i, blk))

    def const(shape):
        return pl.BlockSpec(shape, lambda i: (0, 0), pipeline_mode=pl.Buffered(1))

    return pl.pallas_call(
        _merge_body,
        grid=(nt // tm,),
        in_specs=[
            tok(d), tok(d), tok(d, rblk), tok(ATT_COLS), tok(ATT_COLS), tok(3 * d),
            const((1, GLA_DV)), const(wbg.shape), const(wbd.shape), const(wbm.shape), const(wo.shape),
            const((1, d)), const(wq.shape),
        ],
        out_specs=[tok(d), tok(d), tok(nq)],
        out_shape=[
            jax.ShapeDtypeStruct((nt, d), F32),
            jax.ShapeDtypeStruct((nt, d), BF16),
            jax.ShapeDtypeStruct((nt, nq), F32),
        ],
        compiler_params=_cparams("parallel"),
        name="merge",
    )(x, og, zg, od, om, gt, ong.reshape(1, GLA_DV), wbg, wbd, wbm, wo, ln2.reshape(1, d), wq)


PEER_CAND_GROUPS = 10
SUBLANES = 8


def _peer_cand_layout(tp):
    r = lax.broadcasted_iota(jnp.int32, (PEER_CAND_GROUPS * SUBLANES, tp), 0)
    grp = r // SUBLANES
    p = r % SUBLANES
    K = PEER_TOPK
    k1 = jnp.where(grp < 2, 0, jnp.where(grp < 9, grp - 1, SUBLANES + p))
    k2 = jnp.where(grp < 2, r, jnp.where(grp < 9, p, 0))
    valid = (k1 + 1) * (k2 + 1) <= K
    return (k1 * K + k2).astype(F32), valid


def _peer_cands(x1, x2):
    top, bot = x2[0:SUBLANES], x2[SUBLANES:2 * SUBLANES]
    g1 = [jnp.broadcast_to(x1[0:1], top.shape)] * 2 + [jnp.broadcast_to(x1[k:k + 1], top.shape) for k in range(1, 8)]
    g1.append(x1[SUBLANES:2 * SUBLANES])
    g2 = [top, bot] + [top] * 7 + [jnp.broadcast_to(x2[0:1], top.shape)]
    return jnp.concatenate(g1, axis=0), jnp.concatenate(g2, axis=0)


def _peer_topk_body(qh_ref, keys_ref, a_ref, b_ref, gw_ref):
    tp = qh_ref.shape[0]
    K = PEER_TOPK
    NK = PEER_NKEYS
    rowf = lax.broadcasted_iota(jnp.int32, (NK, tp), 0).astype(F32)
    row16 = lax.broadcasted_iota(jnp.int32, (K, tp), 0)
    pos, valid = _peer_cand_layout(tp)
    a_rows, b_rows, g_rows = [], [], []
    for h in range(PEER_HEADS):
        s0 = _dot_nt(keys_ref[2 * h], qh_ref[:, 2 * h * LANES:(2 * h + 1) * LANES].astype(BF16))
        s1 = _dot_nt(keys_ref[2 * h + 1], qh_ref[:, (2 * h + 1) * LANES:(2 * h + 2) * LANES].astype(BF16))

        def step1(k, carry):
            out = []
            for s, sv, si in (carry[0:3], carry[3:6]):
                mx = jnp.max(s, axis=0, keepdims=True)
                idx = jnp.min(jnp.where(s == mx, rowf, float(NK)), axis=0, keepdims=True)
                out += [jnp.where(rowf == idx, -jnp.inf, s), jnp.where(row16 == k, mx, sv), jnp.where(row16 == k, idx, si)]
            return tuple(out)

        z = jnp.zeros((K, tp), F32)
        _, sv0, si0, _, sv1, si1 = lax.fori_loop(0, K, step1, (s0, z, z, s1, z, z))

        c1, c2 = _peer_cands(sv0, sv1)
        e1, e2 = _peer_cands(si0, si1)
        cand = jnp.where(valid, c1 + c2, -jnp.inf)
        eid = e1 * float(NK) + e2

        def step2(k, carry):
            cand, sc, ev = carry
            mx = jnp.max(cand, axis=0, keepdims=True)
            first = jnp.min(jnp.where(cand == mx, pos, float(K * K)), axis=0, keepdims=True)
            hit = pos == first
            e_k = jnp.max(jnp.where(hit, eid, -1.0), axis=0, keepdims=True)
            return jnp.where(hit, -jnp.inf, cand), jnp.where(row16 == k, mx, sc), jnp.where(row16 == k, e_k, ev)

        _, sc, ev = lax.fori_loop(0, K, step2, (cand, z, z))
        e = jnp.exp(sc - jnp.max(sc, axis=0, keepdims=True))
        g_rows.append(e / jnp.sum(e, axis=0, keepdims=True))
        a = jnp.floor(ev * (1.0 / NK))
        a_rows.append(a)
        b_rows.append(ev - a * float(NK))
    a_ref[...] = jnp.transpose(jnp.concatenate(a_rows, axis=0))
    b_ref[...] = jnp.transpose(jnp.concatenate(b_rows, axis=0))
    gw_ref[...] = jnp.transpose(jnp.concatenate(g_rows, axis=0))


def _peer_topk(qh, keys, tp=128):
    nt, nq = qh.shape
    tp = min(tp, nt)
    spec = pl.BlockSpec((tp, LANES), lambda i: (i, 0))
    shp = jax.ShapeDtypeStruct((nt, LANES), F32)
    return pl.pallas_call(
        _peer_topk_body,
        grid=(nt // tp,),
        in_specs=[
            pl.BlockSpec((tp, nq), lambda i: (i, 0)),
            pl.BlockSpec(keys.shape, lambda i: (0, 0, 0)),
        ],
        out_specs=[spec, spec, spec],
        out_shape=[shp, shp, shp],
        compiler_params=_cparams("parallel"),
        name="peer_topk",
    )(qh, keys)


def _peer_w_body(a_ref, b_ref, gw_ref, w_ref, scr):
    tw = a_ref.shape[0]
    NK = PEER_NKEYS
    stride = scr.shape[0] // NK
    sub = lax.broadcasted_iota(jnp.int32, (NK, LANES), 0).astype(F32)

    def tok(t, carry):
        arow = a_ref[pl.ds(t, 1), :]
        brow = b_ref[pl.ds(t, 1), :]
        grow = gw_ref[pl.ds(t, 1), :]
        oa = (arow == sub).astype(BF16)
        gb = jnp.where(brow == sub, grow, 0.0).astype(BF16)
        scr[pl.ds(t, NK, stride=stride), :] = _dot_nt(oa, gb)
        return carry

    lax.fori_loop(0, tw, tok, 0, unroll=8)
    for i1 in range(NK):
        w_ref[:, i1 * NK:(i1 + 1) * NK] = scr[i1 * stride:i1 * stride + tw, :].astype(BF16)


def _peer_weights(a, b, gw, n_experts, tw=128):
    nt = a.shape[0]
    tw = min(tw, nt)
    stride = tw + SUBLANES
    tok = pl.BlockSpec((tw, LANES), lambda i: (i, 0))
    return pl.pallas_call(
        _peer_w_body,
        grid=(nt // tw,),
        in_specs=[tok, tok, tok],
        out_specs=pl.BlockSpec((tw, n_experts), lambda i: (i, 0)),
        out_shape=jax.ShapeDtypeStruct((nt, n_experts), BF16),
        scratch_shapes=[pltpu.VMEM((PEER_NKEYS * stride, LANES), F32)],
        compiler_params=_cparams("parallel"),
        name="peer_weights",
    )(a, b, gw)


def _peer_mix_body(n2_ref, w_ref, h_ref, u_ref, v_ref, y_ref, acc):
    j = pl.program_id(1)

    @pl.when(j == 0)
    def _():
        acc[...] = jnp.zeros(acc.shape, F32)

    hpre = _dot_nt(n2_ref[...], u_ref[...])
    gelu = 0.5 * hpre * (1.0 + lax.erf(hpre * (2.0 ** -0.5)))
    act = (gelu * w_ref[...].astype(F32)).astype(BF16)
    acc[...] += _dot(act, v_ref[...])

    @pl.when(j == pl.num_programs(1) - 1)
    def _():
        y_ref[...] = h_ref[...] + acc[...]


def _peer_mix(n2, w, h, u, v):
    nt, d = n2.shape
    T = min(PEER_T, nt)
    assert nt % T == 0 and u.shape[0] % PEER_EB == 0
    tok = pl.BlockSpec((T, d), lambda i, j: (i, 0))
    tab = pl.BlockSpec((PEER_EB, d), lambda i, j: (j, 0))
    return pl.pallas_call(
        _peer_mix_body,
        grid=(nt // T, u.shape[0] // PEER_EB),
        in_specs=[tok, pl.BlockSpec((T, PEER_EB), lambda i, j: (i, j)), tok, tab, tab],
        out_specs=tok,
        out_shape=jax.ShapeDtypeStruct((nt, d), F32),
        scratch_shapes=[pltpu.VMEM((T, d), F32)],
        compiler_params=_cparams("parallel", "arbitrary"),
        name="peer_mix",
    )(n2, w, h, u, v)


def _rope_tables(pos):
    half = HD // 2
    inv = ROPE_THETA ** (-jnp.arange(half, dtype=F32) / half)
    ang = pos.astype(F32)[:, None] * inv[None, :]
    cos, sin = jnp.cos(ang), jnp.sin(ang)
    return jnp.concatenate([cos, cos], axis=-1), jnp.concatenate([-sin, sin], axis=-1)


def _prep_weights(w_in, gla_wg2, peer_wq, peer_keys, peer_u, peer_v, w_br_gla, w_br_dil, w_br_mem, w_out):
    c0 = GLA_COLS
    c1 = c0 + GLA_RANK
    c2 = c1 + DIL_COLS
    c3 = c2 + ATT_COLS
    wa = jnp.pad(w_in[:, c0:c1], ((0, 0), (0, LANES - GLA_RANK))).astype(BF16)
    wg2 = jnp.pad(gla_wg2, ((0, LANES - GLA_RANK), (0, 0))).astype(BF16)
    return dict(
        w_gla=w_in[:, :c0].astype(BF16), wa=wa, wg2=wg2,
        w_dil=w_in[:, c1:c2].astype(BF16), w_qm=w_in[:, c2:c3].astype(BF16), w_gt=w_in[:, c3:].astype(BF16),
        wbg=w_br_gla.astype(BF16), wbd=w_br_dil.astype(BF16), wbm=w_br_mem.astype(BF16), wo=w_out.astype(BF16),
        wq=peer_wq.astype(BF16),
        keys=peer_keys.reshape(PEER_HEADS * 2, PEER_NKEYS, LANES).astype(BF16),
        u=peer_u.astype(BF16), v=peer_v.astype(BF16),
    )


def _layer(x, pos, mem_k, mem_v, s0, caches, p, W):
    B, S, d = x.shape
    nt = B * S
    xt = x.reshape(nt, d)
    ln1 = p["ln1_g"]
    zg = _proj("gla", xt, ln1, W["w_gla"])
    lg = _gla_gate(xt, ln1, W["wa"], W["wg2"], p["gla_bg"])
    cosf, sinf = _rope_tables(pos)
    cosf = jnp.broadcast_to(cosf[None], (B, S, HD)).reshape(nt, HD)
    sinf = jnp.broadcast_to(sinf[None], (B, S, HD)).reshape(nt, HD)
    hg = jnp.stack([p["dil_qn_g"], p["dil_kn_g"]])
    zd = _proj("dil", xt, ln1, W["w_dil"], extra=(hg, cosf, sinf))
    qm = _proj("mem_q", xt, ln1, W["w_qm"], extra=(p["mem_qn_g"].reshape(1, HD),))
    gt = _proj("sigmoid", xt, ln1, W["w_gt"])

    prompt = caches is None
    og, s_fin = _gla(zg.reshape(B, S, GLA_COLS), lg.reshape(B, S, GLA_HEADS * GLA_DK), s0,
                     GLA_CHUNK if prompt else S)

    zd3 = zd.reshape(B, S, DIL_COLS)
    new_bufs = []
    if prompt:
        parts = []
        for gi, (wd, r) in enumerate(DIL_GROUPS):
            parts.extend(_dil_prompt(zd3, gi))
            keep = min(wd, S)
            kv = zd3[:, S - keep:, gi * 3 * ATT_COLS + ATT_COLS:(gi + 1) * 3 * ATT_COLS]
            new_bufs.append((kv[..., :ATT_COLS].reshape(B, keep, HEADS, HD), kv[..., ATT_COLS:].reshape(B, keep, HEADS, HD)))
        od = _dil_mix(parts)
    else:
        od = _dil_decode(zd3, caches).reshape(nt, ATT_COLS)
        for gi, (wd, r) in enumerate(DIL_GROUPS):
            kb, vb = caches[gi]
            kn = zd3[..., gi * 3 * ATT_COLS + ATT_COLS:gi * 3 * ATT_COLS + 2 * ATT_COLS].reshape(B, S, HEADS, HD)
            vn = zd3[..., gi * 3 * ATT_COLS + 2 * ATT_COLS:(gi + 1) * 3 * ATT_COLS].reshape(B, S, HEADS, HD)
            keep = min(wd, kb.shape[1] + S)
            new_bufs.append((jnp.concatenate([kb, kn], axis=1)[:, -keep:], jnp.concatenate([vb, vn], axis=1)[:, -keep:]))

    M = mem_k.shape[1]
    qm3 = qm.reshape(B, S, ATT_COLS)
    if prompt:
        om = _mem_attn(qm3, mem_k.reshape(B, M, ATT_COLS), mem_v.reshape(B, M, ATT_COLS), 512)
    else:
        sp = 16
        qp = jnp.pad(qm3, ((0, 0), (0, sp - S), (0, 0)))
        om = _mem_attn(qp, mem_k.reshape(B, M, ATT_COLS), mem_v.reshape(B, M, ATT_COLS), sp)[:, :S]
    om = om.reshape(nt, ATT_COLS)

    h, n2, qh = _merge(xt, og.reshape(nt, d), zg, od, om, gt, p["gla_onorm_g"], W["wbg"], W["wbd"], W["wbm"],
                       W["wo"], p["ln2_g"], W["wq"])
    a, b, gw = _peer_topk(qh, W["keys"])
    y = _peer_mix(n2, _peer_weights(a, b, gw, W["u"].shape[0]), h, W["u"], W["v"])
    return y.reshape(B, S, d), s_fin, new_bufs


def kernel(x_prompt, x_sample, mem_prompt, state_gla, cache_dil_k0, cache_dil_v0, cache_dil_k1, cache_dil_v1, cache_dil_k2, cache_dil_v2, cache_mem_k, cache_mem_v, ln1_g, w_in, gla_wg2, gla_bg, gla_onorm_g, dil_qn_g, dil_kn_g, mem_norm_g, w_mem_kv, mem_qn_g, mem_kn_g, w_br_gla, w_br_dil, w_br_mem, w_out, ln2_g, peer_wq, peer_keys, peer_u, peer_v):
    B, S, d = x_prompt.shape
    Bd, T, _ = x_sample.shape
    p = dict(ln1_g=ln1_g, gla_bg=gla_bg, gla_onorm_g=gla_onorm_g, dil_qn_g=dil_qn_g, dil_kn_g=dil_kn_g,
             mem_qn_g=mem_qn_g, ln2_g=ln2_g)
    W = _prep_weights(w_in, gla_wg2, peer_wq, peer_keys, peer_u, peer_v, w_br_gla, w_br_dil, w_br_mem, w_out)

    M = mem_prompt.shape[1]
    mkv = _proj("mem_kv", mem_prompt.reshape(B * M, d), mem_norm_g, w_mem_kv.astype(BF16),
                extra=(mem_kn_g.reshape(1, HD),))
    mem_k_p = mkv[:, :ATT_COLS].reshape(B, M, HEADS, HD)
    mem_v_p = mkv[:, ATT_COLS:].reshape(B, M, HEADS, HD)

    s0 = jnp.zeros((B, GLA_HEADS, GLA_DK, GLA_DV), F32)
    y_prompt, gla_state_p, bufs_p = _layer(x_prompt, jnp.arange(S, dtype=jnp.int32), mem_k_p, mem_v_p, s0, None, p, W)

    caches = ((cache_dil_k0, cache_dil_v0), (cache_dil_k1, cache_dil_v1), (cache_dil_k2, cache_dil_v2))
    pos_s = PAST_LEN + jnp.arange(T, dtype=jnp.int32)
    y_sample, gla_state_s, bufs_s = _layer(x_sample, pos_s, cache_mem_k, cache_mem_v, state_gla, caches, p, W)

    (dk0_p, dv0_p), (dk1_p, dv1_p), (dk2_p, dv2_p) = bufs_p
    (dk0_s, dv0_s), (dk1_s, dv1_s), (dk2_s, dv2_s) = bufs_s
    return (y_prompt, y_sample,
            gla_state_p, dk0_p, dv0_p, dk1_p, dv1_p, dk2_p, dv2_p, mem_k_p, mem_v_p,
            gla_state_s, dk0_s, dv0_s, dk1_s, dv1_s, dk2_s, dv2_s)
```

```python
import functools

import jax
import jax.numpy as jnp
from jax import lax
from jax.experimental import pallas as pl
from jax.experimental.pallas import tpu as pltpu

F32 = jnp.float32
BF16 = jnp.bfloat16

D_MODEL = 1024
PAST_LEN = 8192
GLA_HEADS = 4
GLA_DK = 128
GLA_DV = 256
GLA_RANK = 16
GLA_TAU = 16.0
DIL_GROUPS = ((128, 1), (512, 4), (2048, 16))
N_DIL = 3
HEADS = 4
HD = 128
SPAN = 128
MEM_LEN = 256
ROPE_THETA = 10000.0
PEER_HEADS = 8
PEER_NKEYS = 128
PEER_TOPK = 16
EPS = 1e-6
NEG = -1e30

LANES = 128
SUBLANES = 8
GLA_COLS = 2 * GLA_HEADS * GLA_DK + 2 * GLA_HEADS * GLA_DV
DIL_COLS = N_DIL * 3 * HEADS * HD
ATT_COLS = HEADS * HD
VMEM_LIMIT = 56 * 1024 * 1024

PEER_T = 1024
PEER_EB = 1024


def _cparams(*sem):
    return pltpu.CompilerParams(dimension_semantics=sem, vmem_limit_bytes=VMEM_LIMIT)


def _rms(x, g):
    return x * lax.rsqrt(jnp.mean(x * x, axis=-1, keepdims=True) + EPS) * g


def _dot(a, b):
    return jnp.dot(a, b, preferred_element_type=F32)


def _dot_nt(a, b):
    return lax.dot_general(a, b, (((1,), (1,)), ((), ())), preferred_element_type=F32)


def _split3(x):
    hi = x.astype(BF16)
    r1 = x - hi.astype(F32)
    mid = r1.astype(BF16)
    lo = (r1 - mid.astype(F32)).astype(BF16)
    return hi, mid, lo


def _proj_body(kind, *refs):
    if kind == "dil":
        x_ref, g_ref, w_ref, hg_ref, cos_ref, sin_ref, o_ref, n_scr = refs
    elif kind in ("mem_q", "mem_kv"):
        x_ref, g_ref, w_ref, hg_ref, o_ref, n_scr = refs
    else:
        x_ref, g_ref, w_ref, o_ref, n_scr = refs
    j = pl.program_id(1)

    @pl.when(j == 0)
    def _():
        n_scr[...] = _rms(x_ref[...], g_ref[...]).astype(BF16)

    z = _dot(n_scr[...], w_ref[...])
    if kind == "gla":
        col = lax.broadcasted_iota(jnp.int32, (1, z.shape[1]), 1) + j * z.shape[1]
        o_ref[...] = z * jnp.where(col < GLA_HEADS * GLA_DK, GLA_DK ** -0.5, 1.0).astype(F32)
    elif kind == "sigmoid":
        o_ref[...] = jax.nn.sigmoid(z)
    elif kind == "mem_q":
        hg = hg_ref[...]
        o_ref[...] = jnp.concatenate([_rms(z[:, h * HD:(h + 1) * HD], hg) for h in range(HEADS)], axis=-1)
    elif kind == "mem_kv":
        @pl.when(j == 0)
        def _():
            hg = hg_ref[...]
            o_ref[...] = jnp.concatenate([_rms(z[:, h * HD:(h + 1) * HD], hg) for h in range(HEADS)], axis=-1)

        @pl.when(j != 0)
        def _():
            o_ref[...] = z
    elif kind == "dil":
        c = j % 3

        @pl.when(c == 2)
        def _():
            o_ref[...] = z

        @pl.when(c != 2)
        def _():
            hg = jnp.where(c == 0, hg_ref[0:1, :], hg_ref[1:2, :])
            cs = cos_ref[...]
            sn = sin_ref[...]
            parts = []
            for h in range(HEADS):
                zh = _rms(z[:, h * HD:(h + 1) * HD], hg)
                parts.append(zh * cs + pltpu.roll(zh, HD // 2, 1) * sn)
            o_ref[...] = jnp.concatenate(parts, axis=-1)
    else:
        raise ValueError(kind)


def _proj(kind, x, g, w, extra=(), tm=1024, tn=1024):
    nt, d = x.shape
    ncol = w.shape[1]
    tm = min(tm, nt)
    if kind in ("dil", "mem_q", "mem_kv"):
        tn = ATT_COLS
    assert nt % tm == 0 and ncol % tn == 0
    in_specs = [
        pl.BlockSpec((tm, d), lambda i, j: (i, 0)),
        pl.BlockSpec((1, d), lambda i, j: (0, 0)),
        pl.BlockSpec((d, tn), lambda i, j: (0, j)),
    ]
    if kind == "dil":
        in_specs += [
            pl.BlockSpec((2, HD), lambda i, j: (0, 0)),
            pl.BlockSpec((tm, HD), lambda i, j: (i, 0)),
            pl.BlockSpec((tm, HD), lambda i, j: (i, 0)),
        ]
    elif kind in ("mem_q", "mem_kv"):
        in_specs += [pl.BlockSpec((1, HD), lambda i, j: (0, 0))]
    return pl.pallas_call(
        functools.partial(_proj_body, kind),
        grid=(nt // tm, ncol // tn),
        in_specs=in_specs,
        out_specs=pl.BlockSpec((tm, tn), lambda i, j: (i, j)),
        out_shape=jax.ShapeDtypeStruct((nt, ncol), F32),
        scratch_shapes=[pltpu.VMEM((tm, d), BF16)],
        compiler_params=_cparams("parallel", "arbitrary"),
        name="proj_" + kind,
    )(x, g.reshape(1, d), w, *extra)


def _gate_body(x_ref, g_ref, wa_ref, wg2_ref, bg_ref, o_ref):
    n = _rms(x_ref[...], g_ref[...]).astype(BF16)
    a = _dot(n, wa_ref[...])
    pre = _dot(a.astype(BF16), wg2_ref[...]) + bg_ref[...]
    o_ref[...] = jax.nn.log_sigmoid(pre) / GLA_TAU


def _gla_gate(x, g, wa, wg2, bg, tm=512):
    nt, d = x.shape
    tm = min(tm, nt)
    ncol = wg2.shape[1]
    return pl.pallas_call(
        _gate_body,
        grid=(nt // tm,),
        in_specs=[
            pl.BlockSpec((tm, d), lambda i: (i, 0)),
            pl.BlockSpec((1, d), lambda i: (0, 0)),
            pl.BlockSpec((d, LANES), lambda i: (0, 0)),
            pl.BlockSpec((LANES, ncol), lambda i: (0, 0)),
            pl.BlockSpec((1, ncol), lambda i: (0, 0)),
        ],
        out_specs=pl.BlockSpec((tm, ncol), lambda i: (i, 0)),
        out_shape=jax.ShapeDtypeStruct((nt, ncol), F32),
        compiler_params=_cparams("parallel"),
        name="gla_gate",
    )(x, g.reshape(1, d), wa, wg2, bg.reshape(1, ncol))


GLA_CHUNK = 128


def _gla_body(tc, q_ref, k_ref, v_ref, g_ref, s0_ref, o_ref, sf_ref, s_scr, qp, kp, vp, gp):
    c = pl.program_id(1)
    P = GLA_CHUNK
    H, dk, dv = GLA_HEADS, GLA_DK, GLA_DV

    @pl.when(c == 0)
    def _():
        s_scr[...] = s0_ref[...]

    if tc == P:
        q, k, v, g = q_ref[...], k_ref[...], v_ref[...], g_ref[...]
    else:
        for pad, ref in ((qp, q_ref), (kp, k_ref), (vp, v_ref), (gp, g_ref)):
            pad[...] = jnp.zeros(pad.shape, F32)
            pad[0:tc, :] = ref[...]
        q, k, v, g = qp[...], kp[...], vp[...], gp[...]

    row = lax.broadcasted_iota(jnp.int32, (P, P), 0)
    col = lax.broadcasted_iota(jnp.int32, (P, P), 1)
    tri = row >= col
    trib = tri.astype(BF16)
    g1, g2, g3 = _split3(g)
    b = _dot(trib, g1) + _dot(trib, g2) + _dot(trib, g3)
    b_mid = b[P // 2 - 1:P // 2, :]
    b_end = b[P - 1:P, :]
    qe = (q * jnp.exp(b)).astype(BF16)
    qm = (q * jnp.exp(b - b_mid)).astype(BF16)
    km = (k * jnp.exp(b_mid - b)).astype(BF16)
    kd = k * jnp.exp(b_end - b)
    vb = v.astype(BF16)
    for h in range(H):
        ks = slice(h * dk, (h + 1) * dk)
        vs = slice(h * dv, (h + 1) * dv)
        a = jnp.where(tri, _dot_nt(qm[:, ks], km[:, ks]), 0.0)
        s = s_scr[h]
        o = _dot(qe[:, ks], s.astype(BF16)) + _dot(a.astype(BF16), vb[:, vs])
        o_ref[:, vs] = o[0:tc, :]
        decay = jnp.exp(jnp.sum(jnp.transpose(g[:, ks]), axis=1, keepdims=True))
        s_new = decay * s + _dot(jnp.transpose(kd[:, ks]).astype(BF16), vb[:, vs])
        s_scr[h] = s_new

        @pl.when(c == pl.num_programs(1) - 1)
        def _():
            sf_ref[h] = s_new


def _gla(zg, lg, s0, tc):
    B, S, _ = zg.shape
    H, dk, dv = GLA_HEADS, GLA_DK, GLA_DV
    assert S % tc == 0
    P = GLA_CHUNK
    qk = H * dk
    st = pl.BlockSpec((None, H, dk, dv), lambda b, c: (b, 0, 0, 0))
    return pl.pallas_call(
        functools.partial(_gla_body, tc),
        grid=(B, S // tc),
        in_specs=[
            pl.BlockSpec((None, tc, qk), lambda b, c: (b, c, 0)),
            pl.BlockSpec((None, tc, qk), lambda b, c: (b, c, 1)),
            pl.BlockSpec((None, tc, H * dv), lambda b, c: (b, c, 2 * qk // (H * dv))),
            pl.BlockSpec((None, tc, qk), lambda b, c: (b, c, 0)),
            st,
        ],
        out_specs=[pl.BlockSpec((None, tc, H * dv), lambda b, c: (b, c, 0)), st],
        out_shape=[
            jax.ShapeDtypeStruct((B, S, H * dv), F32),
            jax.ShapeDtypeStruct((B, H, dk, dv), F32),
        ],
        scratch_shapes=[
            pltpu.VMEM((H, dk, dv), F32),
            pltpu.VMEM((P, qk), F32), pltpu.VMEM((P, qk), F32), pltpu.VMEM((P, H * dv), F32), pltpu.VMEM((P, qk), F32),
        ],
        compiler_params=_cparams("parallel", "arbitrary"),
        name="gla",
    )(zg, zg, zg, lg, s0)


def _proj_dil_body(r, x_ref, g_ref, w_ref, hg_ref, cos_ref, sin_ref, o_ref, n_scr, z_scr):
    j = pl.program_id(1)

    @pl.when(j == 0)
    def _():
        n_scr[...] = _rms(x_ref[...], g_ref[...]).astype(BF16)

    z = _dot(n_scr[...], w_ref[...])

    @pl.when(j == 2)
    def _():
        for h in range(HEADS):
            z_scr[h] = z[:, h * HD:(h + 1) * HD]

    @pl.when(j != 2)
    def _():
        hg = jnp.where(j == 0, hg_ref[0:1, :], hg_ref[1:2, :])
        cs = cos_ref[...]
        sn = sin_ref[...]
        for h in range(HEADS):
            zh = _rms(z[:, h * HD:(h + 1) * HD], hg)
            z_scr[h] = zh * cs + pltpu.roll(zh, HD // 2, 1) * sn

    rows = z_scr.shape[1] // r
    for rho in range(r):
        for h in range(HEADS):
            zh = z_scr[h, pl.ds(rho, rows, stride=r), :] if r > 1 else z_scr[h]
            o_ref[rho, :, h * HD:(h + 1) * HD] = zh


def _proj_dil_group(x, g, w, hg, cosf, sinf, B, S, r, tm=1024):
    nt, d = x.shape
    assert S % tm == 0 and tm % (r * SUBLANES) == 0
    spb = S // tm
    return pl.pallas_call(
        functools.partial(_proj_dil_body, r),
        grid=(nt // tm, 3),
        in_specs=[
            pl.BlockSpec((tm, d), lambda i, j: (i, 0)),
            pl.BlockSpec((1, d), lambda i, j: (0, 0)),
            pl.BlockSpec((d, ATT_COLS), lambda i, j: (0, j)),
            pl.BlockSpec((2, HD), lambda i, j: (0, 0)),
            pl.BlockSpec((tm, HD), lambda i, j: (i, 0)),
            pl.BlockSpec((tm, HD), lambda i, j: (i, 0)),
        ],
        out_specs=pl.BlockSpec((None, r, tm // r, ATT_COLS), lambda i, j: (i // spb, 0, i % spb, j)),
        out_shape=jax.ShapeDtypeStruct((B, r, S // r, 3 * ATT_COLS), F32),
        scratch_shapes=[pltpu.VMEM((tm, d), BF16), pltpu.VMEM((HEADS, tm, HD), F32)],
        compiler_params=_cparams("parallel", "arbitrary"),
        name="proj_dil_r%d" % r,
    )(x, g.reshape(1, d), w, hg, cosf, sinf)


def _dil_body(r, q_ref, kc_ref, kp_ref, vc_ref, vp_ref, o_ref, l_ref):
    i = pl.program_id(1)
    rho = pl.program_id(2)
    scale = HD ** -0.5
    mq = lax.broadcasted_iota(jnp.int32, (SPAN, SPAN), 0)
    mk = lax.broadcasted_iota(jnp.int32, (SPAN, SPAN), 1)
    mask_c = mk <= mq
    mask_p = jnp.logical_and(mk >= mq, i > 0)
    for h in range(HEADS):
        sl = slice(h * HD, (h + 1) * HD)
        q = q_ref[:, sl].astype(BF16)
        sc = jnp.where(mask_c, _dot_nt(q, kc_ref[:, sl].astype(BF16)) * scale, NEG)
        sp = jnp.where(mask_p, _dot_nt(q, kp_ref[:, sl].astype(BF16)) * scale, NEG)
        m = jnp.maximum(jnp.max(sc, axis=-1, keepdims=True), jnp.max(sp, axis=-1, keepdims=True))
        ec = jnp.exp(sc - m)
        ep = jnp.exp(sp - m)
        den = jnp.sum(ec, axis=-1, keepdims=True) + jnp.sum(ep, axis=-1, keepdims=True)
        o = (_dot((ec / den).astype(BF16), vc_ref[:, sl].astype(BF16))
             + _dot((ep / den).astype(BF16), vp_ref[:, sl].astype(BF16)))
        lse = jnp.broadcast_to(m + jnp.log(den), (SPAN, HD))
        if r == 1:
            o_ref[h] = o
            l_ref[h] = lse
        else:
            o_ref[h, pl.ds(rho, SPAN, stride=r), :] = o
            l_ref[h, pl.ds(rho, SPAN, stride=r), :] = lse


def _dil_prompt(zr, S):
    B, r, L, _ = zr.shape
    assert L % SPAN == 0
    nb = L // SPAN

    def spec(c, prev):
        if prev:
            return pl.BlockSpec((None, None, SPAN, ATT_COLS), lambda b, i, p: (b, p, jnp.maximum(i - 1, 0), c))
        return pl.BlockSpec((None, None, SPAN, ATT_COLS), lambda b, i, p: (b, p, i, c))

    out_spec = pl.BlockSpec((None, HEADS, SPAN * r, HD), lambda b, i, p: (b, 0, i, 0))
    return pl.pallas_call(
        functools.partial(_dil_body, r),
        grid=(B, nb, r),
        in_specs=[spec(0, False), spec(1, False), spec(1, True), spec(2, False), spec(2, True)],
        out_specs=[out_spec, out_spec],
        out_shape=[jax.ShapeDtypeStruct((B, HEADS, S, HD), F32)] * 2,
        compiler_params=_cparams("parallel", "parallel", "arbitrary"),
        name="dil_prompt_r%d" % r,
    )(zr, zr, zr, zr, zr)


def _dil_mix_body(o0, l0, o1, l1, o2, l2, od_ref):
    for h in range(HEADS):
        a0, a1, a2 = l0[h], l1[h], l2[h]
        m = jnp.maximum(jnp.maximum(a0, a1), a2)
        e0, e1, e2 = jnp.exp(a0 - m), jnp.exp(a1 - m), jnp.exp(a2 - m)
        den = e0 + e1 + e2
        od_ref[:, h * HD:(h + 1) * HD] = (e0 / den) * o0[h] + (e1 / den) * o1[h] + (e2 / den) * o2[h]


def _dil_mix(parts, tm=1024):
    B, _, S, _ = parts[0].shape
    spb = S // tm
    spec = pl.BlockSpec((None, HEADS, tm, HD), lambda i: (i // spb, 0, i % spb, 0))
    return pl.pallas_call(
        _dil_mix_body,
        grid=(B * spb,),
        in_specs=[spec] * 6,
        out_specs=pl.BlockSpec((tm, ATT_COLS), lambda i: (i, 0)),
        out_shape=jax.ShapeDtypeStruct((B * S, ATT_COLS), F32),
        compiler_params=_cparams("parallel"),
        name="dil_mix",
    )(*parts)


def _dil_dec_body(T, z_ref, k0, v0, k1, v1, k2, v2, od_ref):
    scale = HD ** -0.5
    caches = ((k0, v0), (k1, v1), (k2, v2))
    lrow = lax.broadcasted_iota(jnp.int32, (SPAN, 1), 0)
    trow = lax.broadcasted_iota(jnp.int32, (T, 1), 0)
    for t in range(T):
        outs = [[None] * N_DIL for _ in range(HEADS)]
        lses = [[None] * N_DIL for _ in range(HEADS)]
        for gi, (_, r) in enumerate(DIL_GROUPS):
            base = gi * 3 * ATT_COLS
            kc, vc = caches[gi]
            if r == 1:
                res = 0
                mask_b = lrow >= t
                mask_n = trow <= t
            else:
                res = t
                mask_b = lrow >= 0
                mask_n = trow == t
            for h in range(HEADS):
                c0 = base + h * HD
                q = z_ref[t:t + 1, c0:c0 + HD]
                kn = z_ref[:, c0 + ATT_COLS:c0 + ATT_COLS + HD]
                vn = z_ref[:, c0 + 2 * ATT_COLS:c0 + 2 * ATT_COLS + HD]
                kb = kc[:, res, h, :]
                vb = vc[:, res, h, :]
                sb = jnp.where(mask_b, jnp.sum(kb * q, axis=-1, keepdims=True) * scale, NEG)
                sn = jnp.where(mask_n, jnp.sum(kn * q, axis=-1, keepdims=True) * scale, NEG)
                m = jnp.maximum(jnp.max(sb, axis=0, keepdims=True), jnp.max(sn, axis=0, keepdims=True))
                eb = jnp.exp(sb - m)
                en = jnp.exp(sn - m)
                den = jnp.sum(eb, axis=0, keepdims=True) + jnp.sum(en, axis=0, keepdims=True)
                acc = (jnp.sum((eb / den) * vb, axis=0, keepdims=True)
                       + jnp.sum((en / den) * vn, axis=0, keepdims=True))
                outs[h][gi] = acc
                lses[h][gi] = m + jnp.log(den)
        row = []
        for h in range(HEADS):
            l0, l1, l2 = lses[h]
            m = jnp.maximum(jnp.maximum(l0, l1), l2)
            e0, e1, e2 = jnp.exp(l0 - m), jnp.exp(l1 - m), jnp.exp(l2 - m)
            den = e0 + e1 + e2
            row.append((e0 / den) * outs[h][0] + (e1 / den) * outs[h][1] + (e2 / den) * outs[h][2])
        od_ref[t:t + 1, :] = jnp.concatenate(row, axis=-1)


def _dil_decode(zd, caches):
    Bd, T, _ = zd.shape
    args, specs = [zd], [pl.BlockSpec((None, T, DIL_COLS), lambda b: (b, 0, 0))]
    for (wd, r), (kc, vc) in zip(DIL_GROUPS, caches):
        assert kc.shape[1] == wd and (r == 1 or T <= r)
        used = min(r, T)
        for a in (kc, vc):
            args.append(a.reshape(Bd, wd // r, r, HEADS, HD))
            specs.append(pl.BlockSpec((None, wd // r, used, HEADS, HD), lambda b: (b, 0, 0, 0, 0)))
    return pl.pallas_call(
        functools.partial(_dil_dec_body, T),
        grid=(Bd,),
        in_specs=specs,
        out_specs=pl.BlockSpec((None, T, ATT_COLS), lambda b: (b, 0, 0)),
        out_shape=jax.ShapeDtypeStruct((Bd, T, ATT_COLS), F32),
        compiler_params=_cparams("parallel"),
        name="dil_decode",
    )(*args)


def _mem_body(per_head, q_ref, k_ref, v_ref, o_ref):
    scale = HD ** -0.5
    for h in range(HEADS):
        sl = slice(h * HD, (h + 1) * HD)
        k = k_ref[:, h, :] if per_head else k_ref[:, sl]
        v = v_ref[:, h, :] if per_head else v_ref[:, sl]
        s = _dot_nt(q_ref[:, sl].astype(BF16), k.astype(BF16)) * scale
        e = jnp.exp(s - jnp.max(s, axis=-1, keepdims=True))
        p = e / jnp.sum(e, axis=-1, keepdims=True)
        o_ref[:, sl] = _dot(p.astype(BF16), v.astype(BF16))


def _mem_attn(qm, mk, mv, tq):
    B, S, _ = qm.shape
    M = mk.shape[1]
    per_head = mk.ndim == 4
    if per_head:
        kspec = vspec = pl.BlockSpec((None, M, HEADS, HD), lambda b, i: (b, 0, 0, 0))
    else:
        kspec = pl.BlockSpec((None, M, ATT_COLS), lambda b, i: (b, 0, 0))
        vspec = pl.BlockSpec((None, M, ATT_COLS), lambda b, i: (b, 0, 1))
    return pl.pallas_call(
        functools.partial(_mem_body, per_head),
        grid=(B, S // tq),
        in_specs=[pl.BlockSpec((None, tq, ATT_COLS), lambda b, i: (b, i, 0)), kspec, vspec],
        out_specs=pl.BlockSpec((None, tq, ATT_COLS), lambda b, i: (b, i, 0)),
        out_shape=jax.ShapeDtypeStruct((B, S, ATT_COLS), F32),
        compiler_params=_cparams("parallel", "arbitrary"),
        name="mem_attn",
    )(qm, mk, mv)


def _merge_body(x_ref, og_ref, r_ref, od_ref, om_ref, gt_ref, ong_ref, wbg_ref, wbd_ref, wbm_ref, wo_ref,
                ln2_ref, wq_ref, h_ref, n2_ref, qh_ref):
    og = og_ref[...]
    r = r_ref[...]
    ong = ong_ref[...]
    parts = []
    for h in range(GLA_HEADS):
        sl = slice(h * GLA_DV, (h + 1) * GLA_DV)
        parts.append(_rms(og[:, sl], ong) * jax.nn.silu(r[:, sl]))
    br_gla = _dot(jnp.concatenate(parts, axis=-1).astype(BF16), wbg_ref[...])
    br_dil = _dot(od_ref[...].astype(BF16), wbd_ref[...])
    br_mem = _dot(om_ref[...].astype(BF16), wbm_ref[...])
    d = D_MODEL
    merged = gt_ref[:, 0:d] * br_gla + gt_ref[:, d:2 * d] * br_dil + gt_ref[:, 2 * d:3 * d] * br_mem
    hres = x_ref[...] + _dot(merged.astype(BF16), wo_ref[...])
    h_ref[...] = hres
    n2 = _rms(hres, ln2_ref[...]).astype(BF16)
    n2_ref[...] = n2
    qh_ref[...] = _dot(n2, wq_ref[...])


def _merge(x, og, zg, od, om, gt, ong, wbg, wbd, wbm, wo, ln2, wq, tm=256):
    nt, d = x.shape
    tm = min(tm, nt)
    nq = wq.shape[1]
    rblk = (2 * GLA_HEADS * GLA_DK + GLA_HEADS * GLA_DV) // d

    def tok(cols, blk=0):
        return pl.BlockSpec((tm, cols), lambda i: (i, blk))

    def const(shape):
        return pl.BlockSpec(shape, lambda i: (0, 0), pipeline_mode=pl.Buffered(1))

    return pl.pallas_call(
        _merge_body,
        grid=(nt // tm,),
        in_specs=[
            tok(d), tok(d), tok(d, rblk), tok(ATT_COLS), tok(ATT_COLS), tok(3 * d),
            const((1, GLA_DV)), const(wbg.shape), const(wbd.shape), const(wbm.shape), const(wo.shape),
            const((1, d)), const(wq.shape),
        ],
        out_specs=[tok(d), tok(d), tok(nq)],
        out_shape=[
            jax.ShapeDtypeStruct((nt, d), F32),
            jax.ShapeDtypeStruct((nt, d), BF16),
            jax.ShapeDtypeStruct((nt, nq), F32),
        ],
        compiler_params=_cparams("parallel"),
        name="merge",
    )(x, og, zg, od, om, gt, ong.reshape(1, GLA_DV), wbg, wbd, wbm, wo, ln2.reshape(1, d), wq)


PEER_CAND_GROUPS = 10


def _peer_cand_layout(tp):
    r = lax.broadcasted_iota(jnp.int32, (PEER_CAND_GROUPS * SUBLANES, tp), 0)
    grp = r // SUBLANES
    p = r % SUBLANES
    K = PEER_TOPK
    k1 = jnp.where(grp < 2, 0, jnp.where(grp < 9, grp - 1, SUBLANES + p))
    k2 = jnp.where(grp < 2, r, jnp.where(grp < 9, p, 0))
    valid = (k1 + 1) * (k2 + 1) <= K
    return (k1 * K + k2).astype(F32), valid


def _peer_cands(x1, x2):
    top, bot = x2[0:SUBLANES], x2[SUBLANES:2 * SUBLANES]
    g1 = [jnp.broadcast_to(x1[0:1], top.shape)] * 2 + [jnp.broadcast_to(x1[k:k + 1], top.shape) for k in range(1, 8)]
    g1.append(x1[SUBLANES:2 * SUBLANES])
    g2 = [top, bot] + [top] * 7 + [jnp.broadcast_to(x2[0:1], top.shape)]
    return jnp.concatenate(g1, axis=0), jnp.concatenate(g2, axis=0)


def _peer_topk_body(qh_ref, keys_ref, a_ref, b_ref, gw_ref):
    tp = qh_ref.shape[0]
    K = PEER_TOPK
    NK = PEER_NKEYS
    rowf = lax.broadcasted_iota(jnp.int32, (NK, tp), 0).astype(F32)
    row16 = lax.broadcasted_iota(jnp.int32, (K, tp), 0)
    pos, valid = _peer_cand_layout(tp)
    a_rows, b_rows, g_rows = [], [], []
    for h in range(PEER_HEADS):
        s0 = _dot_nt(keys_ref[2 * h], qh_ref[:, 2 * h * LANES:(2 * h + 1) * LANES].astype(BF16))
        s1 = _dot_nt(keys_ref[2 * h + 1], qh_ref[:, (2 * h + 1) * LANES:(2 * h + 2) * LANES].astype(BF16))

        def step1(k, carry):
            out = []
            for s, sv, si in (carry[0:3], carry[3:6]):
                mx = jnp.max(s, axis=0, keepdims=True)
                idx = jnp.min(jnp.where(s == mx, rowf, float(NK)), axis=0, keepdims=True)
                out += [jnp.where(rowf == idx, -jnp.inf, s), jnp.where(row16 == k, mx, sv), jnp.where(row16 == k, idx, si)]
            return tuple(out)

        z = jnp.zeros((K, tp), F32)
        _, sv0, si0, _, sv1, si1 = lax.fori_loop(0, K, step1, (s0, z, z, s1, z, z))

        c1, c2 = _peer_cands(sv0, sv1)
        e1, e2 = _peer_cands(si0, si1)
        cand = jnp.where(valid, c1 + c2, -jnp.inf)
        eid = e1 * float(NK) + e2

        def step2(k, carry):
            cand, sc, ev = carry
            mx = jnp.max(cand, axis=0, keepdims=True)
            first = jnp.min(jnp.where(cand == mx, pos, float(K * K)), axis=0, keepdims=True)
            hit = pos == first
            e_k = jnp.max(jnp.where(hit, eid, -1.0), axis=0, keepdims=True)
            return jnp.where(hit, -jnp.inf, cand), jnp.where(row16 == k, mx, sc), jnp.where(row16 == k, e_k, ev)

        _, sc, ev = lax.fori_loop(0, K, step2, (cand, z, z))
        e = jnp.exp(sc - jnp.max(sc, axis=0, keepdims=True))
        g_rows.append(e / jnp.sum(e, axis=0, keepdims=True))
        a = jnp.floor(ev * (1.0 / NK))
        a_rows.append(a)
        b_rows.append(ev - a * float(NK))
    a_ref[...] = jnp.transpose(jnp.concatenate(a_rows, axis=0))
    b_ref[...] = jnp.transpose(jnp.concatenate(b_rows, axis=0))
    gw_ref[...] = jnp.transpose(jnp.concatenate(g_rows, axis=0))


def _peer_topk(qh, keys, tp=128):
    nt, nq = qh.shape
    tp = min(tp, nt)
    spec = pl.BlockSpec((tp, LANES), lambda i: (i, 0))
    shp = jax.ShapeDtypeStruct((nt, LANES), F32)
    return pl.pallas_call(
        _peer_topk_body,
        grid=(nt // tp,),
        in_specs=[
            pl.BlockSpec((tp, nq), lambda i: (i, 0)),
            pl.BlockSpec(keys.shape, lambda i: (0, 0, 0)),
        ],
        out_specs=[spec, spec, spec],
        out_shape=[shp, shp, shp],
        compiler_params=_cparams("parallel"),
        name="peer_topk",
    )(qh, keys)


def _peer_w_body(a_ref, b_ref, gw_ref, w_ref, scr):
    tw = a_ref.shape[0]
    NK = PEER_NKEYS
    stride = scr.shape[0] // NK
    sub = lax.broadcasted_iota(jnp.int32, (NK, LANES), 0).astype(F32)

    def tok(t, carry):
        arow = a_ref[pl.ds(t, 1), :]
        brow = b_ref[pl.ds(t, 1), :]
        grow = gw_ref[pl.ds(t, 1), :]
        oa = (arow == sub).astype(BF16)
        gb = jnp.where(brow == sub, grow, 0.0).astype(BF16)
        scr[pl.ds(t, NK, stride=stride), :] = _dot_nt(oa, gb)
        return carry

    lax.fori_loop(0, tw, tok, 0, unroll=8)
    for i1 in range(NK):
        w_ref[:, i1 * NK:(i1 + 1) * NK] = scr[i1 * stride:i1 * stride + tw, :].astype(BF16)


def _peer_weights(a, b, gw, n_experts, tw=128):
    nt = a.shape[0]
    tw = min(tw, nt)
    stride = tw + SUBLANES
    tok = pl.BlockSpec((tw, LANES), lambda i: (i, 0))
    return pl.pallas_call(
        _peer_w_body,
        grid=(nt // tw,),
        in_specs=[tok, tok, tok],
        out_specs=pl.BlockSpec((tw, n_experts), lambda i: (i, 0)),
        out_shape=jax.ShapeDtypeStruct((nt, n_experts), BF16),
        scratch_shapes=[pltpu.VMEM((PEER_NKEYS * stride, LANES), F32)],
        compiler_params=_cparams("parallel"),
        name="peer_weights",
    )(a, b, gw)


def _peer_mix_body(n2_ref, w_ref, h_ref, u_ref, v_ref, y_ref, acc):
    j = pl.program_id(1)

    @pl.when(j == 0)
    def _():
        acc[...] = jnp.zeros(acc.shape, F32)

    hpre = _dot_nt(n2_ref[...], u_ref[...])
    gelu = 0.5 * hpre * (1.0 + lax.erf(hpre * (2.0 ** -0.5)))
    act = (gelu * w_ref[...].astype(F32)).astype(BF16)
    acc[...] += _dot(act, v_ref[...])

    @pl.when(j == pl.num_programs(1) - 1)
    def _():
        y_ref[...] = h_ref[...] + acc[...]


def _peer_mix(n2, w, h, u, v):
    nt, d = n2.shape
    T = min(PEER_T, nt)
    assert nt % T == 0 and u.shape[0] % PEER_EB == 0
    tok = pl.BlockSpec((T, d), lambda i, j: (i, 0))
    tab = pl.BlockSpec((PEER_EB, d), lambda i, j: (j, 0))
    return pl.pallas_call(
        _peer_mix_body,
        grid=(nt // T, u.shape[0] // PEER_EB),
        in_specs=[tok, pl.BlockSpec((T, PEER_EB), lambda i, j: (i, j)), tok, tab, tab],
        out_specs=tok,
        out_shape=jax.ShapeDtypeStruct((nt, d), F32),
        scratch_shapes=[pltpu.VMEM((T, d), F32)],
        compiler_params=_cparams("parallel", "arbitrary"),
        name="peer_mix",
    )(n2, w, h, u, v)


def _rope_tables(pos):
    half = HD // 2
    inv = ROPE_THETA ** (-jnp.arange(half, dtype=F32) / half)
    ang = pos.astype(F32)[:, None] * inv[None, :]
    cos, sin = jnp.cos(ang), jnp.sin(ang)
    return jnp.concatenate([cos, cos], axis=-1), jnp.concatenate([-sin, sin], axis=-1)


def _prep_weights(w_in, gla_wg2, peer_wq, peer_keys, peer_u, peer_v, w_br_gla, w_br_dil, w_br_mem, w_out):
    c0 = GLA_COLS
    c1 = c0 + GLA_RANK
    c2 = c1 + DIL_COLS
    c3 = c2 + ATT_COLS
    wa = jnp.pad(w_in[:, c0:c1], ((0, 0), (0, LANES - GLA_RANK))).astype(BF16)
    wg2 = jnp.pad(gla_wg2, ((0, LANES - GLA_RANK), (0, 0))).astype(BF16)
    return dict(
        w_gla=w_in[:, :c0].astype(BF16), wa=wa, wg2=wg2,
        w_dil=w_in[:, c1:c2].astype(BF16), w_qm=w_in[:, c2:c3].astype(BF16), w_gt=w_in[:, c3:].astype(BF16),
        wbg=w_br_gla.astype(BF16), wbd=w_br_dil.astype(BF16), wbm=w_br_mem.astype(BF16), wo=w_out.astype(BF16),
        wq=peer_wq.astype(BF16),
        keys=peer_keys.reshape(PEER_HEADS * 2, PEER_NKEYS, LANES).astype(BF16),
        u=peer_u.astype(BF16), v=peer_v.astype(BF16),
    )


def _layer(x, pos, mem_k, mem_v, s0, caches, p, W):
    B, S, d = x.shape
    nt = B * S
    xt = x.reshape(nt, d)
    ln1 = p["ln1_g"]
    zg = _proj("gla", xt, ln1, W["w_gla"])
    lg = _gla_gate(xt, ln1, W["wa"], W["wg2"], p["gla_bg"])
    cosf, sinf = _rope_tables(pos)
    cosf = jnp.broadcast_to(cosf[None], (B, S, HD)).reshape(nt, HD)
    sinf = jnp.broadcast_to(sinf[None], (B, S, HD)).reshape(nt, HD)
    hg = jnp.stack([p["dil_qn_g"], p["dil_kn_g"]])
    qm = _proj("mem_q", xt, ln1, W["w_qm"], extra=(p["mem_qn_g"].reshape(1, HD),))
    gt = _proj("sigmoid", xt, ln1, W["w_gt"])

    prompt = caches is None
    og, s_fin = _gla(zg.reshape(B, S, GLA_COLS), lg.reshape(B, S, GLA_HEADS * GLA_DK), s0,
                     GLA_CHUNK if prompt else S)

    new_bufs = []
    gcols = 3 * ATT_COLS
    if prompt:
        parts = []
        for gi, (wd, r) in enumerate(DIL_GROUPS):
            zr = _proj_dil_group(xt, ln1, W["w_dil"][:, gi * gcols:(gi + 1) * gcols], hg, cosf, sinf, B, S, r)
            parts.extend(_dil_prompt(zr, S))
            keep = min(wd, S)
            kv = zr[:, :, (S - keep) // r:, ATT_COLS:]
            kv = jnp.swapaxes(kv, 1, 2).reshape(B, keep, 2, HEADS, HD)
            new_bufs.append((kv[:, :, 0], kv[:, :, 1]))
        od = _dil_mix(parts)
    else:
        zd3 = _proj("dil", xt, ln1, W["w_dil"], extra=(hg, cosf, sinf)).reshape(B, S, DIL_COLS)
        od = _dil_decode(zd3, caches).reshape(nt, ATT_COLS)
        for gi, (wd, r) in enumerate(DIL_GROUPS):
            kb, vb = caches[gi]
            kn = zd3[..., gi * gcols + ATT_COLS:gi * gcols + 2 * ATT_COLS].reshape(B, S, HEADS, HD)
            vn = zd3[..., gi * gcols + 2 * ATT_COLS:(gi + 1) * gcols].reshape(B, S, HEADS, HD)
            keep = min(wd, kb.shape[1] + S)
            new_bufs.append((jnp.concatenate([kb, kn], axis=1)[:, -keep:], jnp.concatenate([vb, vn], axis=1)[:, -keep:]))

    qm3 = qm.reshape(B, S, ATT_COLS)
    if prompt:
        om = _mem_attn(qm3, mem_k, mem_v, 512)
    else:
        sp = 16
        qp = jnp.pad(qm3, ((0, 0), (0, sp - S), (0, 0)))
        om = _mem_attn(qp, mem_k, mem_v, sp)[:, :S]
    om = om.reshape(nt, ATT_COLS)

    h, n2, qh = _merge(xt, og.reshape(nt, d), zg, od, om, gt, p["gla_onorm_g"], W["wbg"], W["wbd"], W["wbm"],
                       W["wo"], p["ln2_g"], W["wq"])
    a, b, gw = _peer_topk(qh, W["keys"])
    y = _peer_mix(n2, _peer_weights(a, b, gw, W["u"].shape[0]), h, W["u"], W["v"])
    return y.reshape(B, S, d), s_fin, new_bufs


def kernel(x_prompt, x_sample, mem_prompt, state_gla, cache_dil_k0, cache_dil_v0, cache_dil_k1, cache_dil_v1, cache_dil_k2, cache_dil_v2, cache_mem_k, cache_mem_v, ln1_g, w_in, gla_wg2, gla_bg, gla_onorm_g, dil_qn_g, dil_kn_g, mem_norm_g, w_mem_kv, mem_qn_g, mem_kn_g, w_br_gla, w_br_dil, w_br_mem, w_out, ln2_g, peer_wq, peer_keys, peer_u, peer_v):
    B, S, d = x_prompt.shape
    Bd, T, _ = x_sample.shape
    p = dict(ln1_g=ln1_g, gla_bg=gla_bg, gla_onorm_g=gla_onorm_g, dil_qn_g=dil_qn_g, dil_kn_g=dil_kn_g,
             mem_qn_g=mem_qn_g, ln2_g=ln2_g)
    W = _prep_weights(w_in, gla_wg2, peer_wq, peer_keys, peer_u, peer_v, w_br_gla, w_br_dil, w_br_mem, w_out)

    M = mem_prompt.shape[1]
    mkv = _proj("mem_kv", mem_prompt.reshape(B * M, d), mem_norm_g, w_mem_kv.astype(BF16),
                extra=(mem_kn_g.reshape(1, HD),))
    mem_k_p = mkv[:, :ATT_COLS].reshape(B, M, HEADS, HD)
    mem_v_p = mkv[:, ATT_COLS:].reshape(B, M, HEADS, HD)

    s0 = jnp.zeros((B, GLA_HEADS, GLA_DK, GLA_DV), F32)
    mkv3 = mkv.reshape(B, M, 2 * ATT_COLS)
    y_prompt, gla_state_p, bufs_p = _layer(x_prompt, jnp.arange(S, dtype=jnp.int32), mkv3, mkv3, s0, None, p, W)

    caches = ((cache_dil_k0, cache_dil_v0), (cache_dil_k1, cache_dil_v1), (cache_dil_k2, cache_dil_v2))
    pos_s = PAST_LEN + jnp.arange(T, dtype=jnp.int32)
    y_sample, gla_state_s, bufs_s = _layer(x_sample, pos_s, cache_mem_k, cache_mem_v, state_gla, caches, p, W)

    (dk0_p, dv0_p), (dk1_p, dv1_p), (dk2_p, dv2_p) = bufs_p
    (dk0_s, dv0_s), (dk1_s, dv1_s), (dk2_s, dv2_s) = bufs_s
    return (y_prompt, y_sample,
            gla_state_p, dk0_p, dv0_p, dk1_p, dv1_p, dk2_p, dv2_p, mem_k_p, mem_v_p,
            gla_state_s, dk0_s, dv0_s, dk1_s, dv1_s, dk2_s, dv2_s)
```

```python
import functools

import jax
import jax.numpy as jnp
from jax import lax
from jax.experimental import pallas as pl
from jax.experimental.pallas import tpu as pltpu

F32 = jnp.float32
BF16 = jnp.bfloat16

D_MODEL = 1024
PAST_LEN = 8192
GLA_HEADS = 4
GLA_DK = 128
GLA_DV = 256
GLA_RANK = 16
GLA_TAU = 16.0
DIL_GROUPS = ((128, 1), (512, 4), (2048, 16))
N_DIL = 3
HEADS = 4
HD = 128
SPAN = 128
MEM_LEN = 256
ROPE_THETA = 10000.0
PEER_HEADS = 8
PEER_NKEYS = 128
PEER_TOPK = 16
EPS = 1e-6
NEG = -1e30

LANES = 128
SUBLANES = 8
GLA_COLS = 2 * GLA_HEADS * GLA_DK + 2 * GLA_HEADS * GLA_DV
DIL_COLS = N_DIL * 3 * HEADS * HD
ATT_COLS = HEADS * HD
VMEM_LIMIT = 56 * 1024 * 1024

PEER_T = 1024
PEER_EB = 1024


def _cparams(*sem):
    return pltpu.CompilerParams(dimension_semantics=sem, vmem_limit_bytes=VMEM_LIMIT)


def _rms(x, g):
    return x * lax.rsqrt(jnp.mean(x * x, axis=-1, keepdims=True) + EPS) * g


def _dot(a, b):
    return jnp.dot(a, b, preferred_element_type=F32)


def _dot_nt(a, b):
    return lax.dot_general(a, b, (((1,), (1,)), ((), ())), preferred_element_type=F32)


def _split3(x):
    hi = x.astype(BF16)
    r1 = x - hi.astype(F32)
    mid = r1.astype(BF16)
    lo = (r1 - mid.astype(F32)).astype(BF16)
    return hi, mid, lo


def _proj_body(kind, *refs):
    if kind == "dil":
        x_ref, g_ref, w_ref, hg_ref, cos_ref, sin_ref, o_ref, n_scr = refs
    elif kind in ("mem_q", "mem_kv"):
        x_ref, g_ref, w_ref, hg_ref, o_ref, n_scr = refs
    else:
        x_ref, g_ref, w_ref, o_ref, n_scr = refs
    j = pl.program_id(1)

    @pl.when(j == 0)
    def _():
        n_scr[...] = _rms(x_ref[...], g_ref[...]).astype(BF16)

    z = _dot(n_scr[...], w_ref[...])
    if kind == "gla":
        col = lax.broadcasted_iota(jnp.int32, (1, z.shape[1]), 1) + j * z.shape[1]
        o_ref[...] = z * jnp.where(col < GLA_HEADS * GLA_DK, GLA_DK ** -0.5, 1.0).astype(F32)
    elif kind == "sigmoid":
        o_ref[...] = jax.nn.sigmoid(z)
    elif kind == "mem_q":
        hg = hg_ref[...]
        o_ref[...] = jnp.concatenate([_rms(z[:, h * HD:(h + 1) * HD], hg) for h in range(HEADS)], axis=-1)
    elif kind == "mem_kv":
        @pl.when(j == 0)
        def _():
            hg = hg_ref[...]
            o_ref[...] = jnp.concatenate([_rms(z[:, h * HD:(h + 1) * HD], hg) for h in range(HEADS)], axis=-1)

        @pl.when(j != 0)
        def _():
            o_ref[...] = z
    elif kind == "dil":
        c = j % 3

        @pl.when(c == 2)
        def _():
            o_ref[...] = z

        @pl.when(c != 2)
        def _():
            hg = jnp.where(c == 0, hg_ref[0:1, :], hg_ref[1:2, :])
            cs = cos_ref[...]
            sn = sin_ref[...]
            parts = []
            for h in range(HEADS):
                zh = _rms(z[:, h * HD:(h + 1) * HD], hg)
                parts.append(zh * cs + pltpu.roll(zh, HD // 2, 1) * sn)
            o_ref[...] = jnp.concatenate(parts, axis=-1)
    else:
        raise ValueError(kind)


def _proj(kind, x, g, w, extra=(), tm=1024, tn=1024):
    nt, d = x.shape
    ncol = w.shape[1]
    tm = min(tm, nt)
    if kind in ("dil", "mem_q", "mem_kv"):
        tn = ATT_COLS
    assert nt % tm == 0 and ncol % tn == 0
    in_specs = [
        pl.BlockSpec((tm, d), lambda i, j: (i, 0)),
        pl.BlockSpec((1, d), lambda i, j: (0, 0)),
        pl.BlockSpec((d, tn), lambda i, j: (0, j)),
    ]
    if kind == "dil":
        in_specs += [
            pl.BlockSpec((2, HD), lambda i, j: (0, 0)),
            pl.BlockSpec((tm, HD), lambda i, j: (i, 0)),
            pl.BlockSpec((tm, HD), lambda i, j: (i, 0)),
        ]
    elif kind in ("mem_q", "mem_kv"):
        in_specs += [pl.BlockSpec((1, HD), lambda i, j: (0, 0))]
    return pl.pallas_call(
        functools.partial(_proj_body, kind),
        grid=(nt // tm, ncol // tn),
        in_specs=in_specs,
        out_specs=pl.BlockSpec((tm, tn), lambda i, j: (i, j)),
        out_shape=jax.ShapeDtypeStruct((nt, ncol), F32),
        scratch_shapes=[pltpu.VMEM((tm, d), BF16)],
        compiler_params=_cparams("parallel", "arbitrary"),
        name="proj_" + kind,
    )(x, g.reshape(1, d), w, *extra)


def _gate_body(x_ref, g_ref, wa_ref, wg2_ref, bg_ref, o_ref):
    n = _rms(x_ref[...], g_ref[...]).astype(BF16)
    a = _dot(n, wa_ref[...])
    pre = _dot(a.astype(BF16), wg2_ref[...]) + bg_ref[...]
    o_ref[...] = jax.nn.log_sigmoid(pre) / GLA_TAU


def _gla_gate(x, g, wa, wg2, bg, tm=512):
    nt, d = x.shape
    tm = min(tm, nt)
    ncol = wg2.shape[1]
    return pl.pallas_call(
        _gate_body,
        grid=(nt // tm,),
        in_specs=[
            pl.BlockSpec((tm, d), lambda i: (i, 0)),
            pl.BlockSpec((1, d), lambda i: (0, 0)),
            pl.BlockSpec((d, LANES), lambda i: (0, 0)),
            pl.BlockSpec((LANES, ncol), lambda i: (0, 0)),
            pl.BlockSpec((1, ncol), lambda i: (0, 0)),
        ],
        out_specs=pl.BlockSpec((tm, ncol), lambda i: (i, 0)),
        out_shape=jax.ShapeDtypeStruct((nt, ncol), F32),
        compiler_params=_cparams("parallel"),
        name="gla_gate",
    )(x, g.reshape(1, d), wa, wg2, bg.reshape(1, ncol))


GLA_CHUNK = 128


def _gla_body(tc, q_ref, k_ref, v_ref, g_ref, s0_ref, o_ref, sf_ref, s_scr, qp, kp, vp, gp):
    c = pl.program_id(1)
    P = GLA_CHUNK
    H, dk, dv = GLA_HEADS, GLA_DK, GLA_DV

    @pl.when(c == 0)
    def _():
        s_scr[...] = s0_ref[...]

    if tc == P:
        q, k, v, g = q_ref[...], k_ref[...], v_ref[...], g_ref[...]
    else:
        for pad, ref in ((qp, q_ref), (kp, k_ref), (vp, v_ref), (gp, g_ref)):
            pad[...] = jnp.zeros(pad.shape, F32)
            pad[0:tc, :] = ref[...]
        q, k, v, g = qp[...], kp[...], vp[...], gp[...]

    row = lax.broadcasted_iota(jnp.int32, (P, P), 0)
    col = lax.broadcasted_iota(jnp.int32, (P, P), 1)
    tri = row >= col
    trib = tri.astype(BF16)
    g1, g2, g3 = _split3(g)
    b = _dot(trib, g1) + _dot(trib, g2) + _dot(trib, g3)
    b_mid = b[P // 2 - 1:P // 2, :]
    b_end = b[P - 1:P, :]
    qe = (q * jnp.exp(b)).astype(BF16)
    qm = (q * jnp.exp(b - b_mid)).astype(BF16)
    km = (k * jnp.exp(b_mid - b)).astype(BF16)
    kd = k * jnp.exp(b_end - b)
    vb = v.astype(BF16)
    for h in range(H):
        ks = slice(h * dk, (h + 1) * dk)
        vs = slice(h * dv, (h + 1) * dv)
        a = jnp.where(tri, _dot_nt(qm[:, ks], km[:, ks]), 0.0)
        s = s_scr[h]
        o = _dot(qe[:, ks], s.astype(BF16)) + _dot(a.astype(BF16), vb[:, vs])
        o_ref[:, vs] = o[0:tc, :]
        decay = jnp.exp(jnp.sum(jnp.transpose(g[:, ks]), axis=1, keepdims=True))
        s_new = decay * s + _dot(jnp.transpose(kd[:, ks]).astype(BF16), vb[:, vs])
        s_scr[h] = s_new

        @pl.when(c == pl.num_programs(1) - 1)
        def _():
            sf_ref[h] = s_new


def _gla(zg, lg, s0, tc):
    B, S, _ = zg.shape
    H, dk, dv = GLA_HEADS, GLA_DK, GLA_DV
    assert S % tc == 0
    P = GLA_CHUNK
    qk = H * dk
    st = pl.BlockSpec((None, H, dk, dv), lambda b, c: (b, 0, 0, 0))
    return pl.pallas_call(
        functools.partial(_gla_body, tc),
        grid=(B, S // tc),
        in_specs=[
            pl.BlockSpec((None, tc, qk), lambda b, c: (b, c, 0)),
            pl.BlockSpec((None, tc, qk), lambda b, c: (b, c, 1)),
            pl.BlockSpec((None, tc, H * dv), lambda b, c: (b, c, 2 * qk // (H * dv))),
            pl.BlockSpec((None, tc, qk), lambda b, c: (b, c, 0)),
            st,
        ],
        out_specs=[pl.BlockSpec((None, tc, H * dv), lambda b, c: (b, c, 0)), st],
        out_shape=[
            jax.ShapeDtypeStruct((B, S, H * dv), F32),
            jax.ShapeDtypeStruct((B, H, dk, dv), F32),
        ],
        scratch_shapes=[
            pltpu.VMEM((H, dk, dv), F32),
            pltpu.VMEM((P, qk), F32), pltpu.VMEM((P, qk), F32), pltpu.VMEM((P, H * dv), F32), pltpu.VMEM((P, qk), F32),
        ],
        compiler_params=_cparams("parallel", "arbitrary"),
        name="gla",
    )(zg, zg, zg, lg, s0)


def _proj_dil_body(r, x_ref, g_ref, w_ref, hg_ref, cos_ref, sin_ref, o_ref, n_scr, z_scr):
    j = pl.program_id(1)

    @pl.when(j == 0)
    def _():
        n_scr[...] = _rms(x_ref[...], g_ref[...]).astype(BF16)

    z = _dot(n_scr[...], w_ref[...])

    @pl.when(j == 2)
    def _():
        for h in range(HEADS):
            z_scr[h] = z[:, h * HD:(h + 1) * HD]

    @pl.when(j != 2)
    def _():
        hg = jnp.where(j == 0, hg_ref[0:1, :], hg_ref[1:2, :])
        cs = cos_ref[...]
        sn = sin_ref[...]
        for h in range(HEADS):
            zh = _rms(z[:, h * HD:(h + 1) * HD], hg)
            z_scr[h] = zh * cs + pltpu.roll(zh, HD // 2, 1) * sn

    rows = z_scr.shape[1] // r
    for rho in range(r):
        for h in range(HEADS):
            zh = z_scr[h, pl.ds(rho, rows, stride=r), :] if r > 1 else z_scr[h]
            o_ref[rho, :, h * HD:(h + 1) * HD] = zh


def _proj_dil_group(x, g, w, hg, cosf, sinf, B, S, r, tm=1024):
    nt, d = x.shape
    assert S % tm == 0 and tm % (r * SUBLANES) == 0
    spb = S // tm
    return pl.pallas_call(
        functools.partial(_proj_dil_body, r),
        grid=(nt // tm, 3),
        in_specs=[
            pl.BlockSpec((tm, d), lambda i, j: (i, 0)),
            pl.BlockSpec((1, d), lambda i, j: (0, 0)),
            pl.BlockSpec((d, ATT_COLS), lambda i, j: (0, j)),
            pl.BlockSpec((2, HD), lambda i, j: (0, 0)),
            pl.BlockSpec((tm, HD), lambda i, j: (i, 0)),
            pl.BlockSpec((tm, HD), lambda i, j: (i, 0)),
        ],
        out_specs=pl.BlockSpec((None, r, tm // r, ATT_COLS), lambda i, j: (i // spb, 0, i % spb, j)),
        out_shape=jax.ShapeDtypeStruct((B, r, S // r, 3 * ATT_COLS), F32),
        scratch_shapes=[pltpu.VMEM((tm, d), BF16), pltpu.VMEM((HEADS, tm, HD), F32)],
        compiler_params=_cparams("parallel", "arbitrary"),
        name="proj_dil_r%d" % r,
    )(x, g.reshape(1, d), w, hg, cosf, sinf)


def _dil_body(r, q_ref, kc_ref, kp_ref, vc_ref, vp_ref, o_ref, l_ref):
    i = pl.program_id(1)
    rho = pl.program_id(2)
    scale = HD ** -0.5
    mq = lax.broadcasted_iota(jnp.int32, (SPAN, SPAN), 0)
    mk = lax.broadcasted_iota(jnp.int32, (SPAN, SPAN), 1)
    mask_c = mk <= mq
    mask_p = jnp.logical_and(mk >= mq, i > 0)
    for h in range(HEADS):
        sl = slice(h * HD, (h + 1) * HD)
        q = q_ref[:, sl].astype(BF16)
        sc = jnp.where(mask_c, _dot_nt(q, kc_ref[:, sl].astype(BF16)) * scale, NEG)
        sp = jnp.where(mask_p, _dot_nt(q, kp_ref[:, sl].astype(BF16)) * scale, NEG)
        m = jnp.maximum(jnp.max(sc, axis=-1, keepdims=True), jnp.max(sp, axis=-1, keepdims=True))
        ec = jnp.exp(sc - m)
        ep = jnp.exp(sp - m)
        den = jnp.sum(ec, axis=-1, keepdims=True) + jnp.sum(ep, axis=-1, keepdims=True)
        o = (_dot((ec / den).astype(BF16), vc_ref[:, sl].astype(BF16))
             + _dot((ep / den).astype(BF16), vp_ref[:, sl].astype(BF16)))
        lse = jnp.broadcast_to(m + jnp.log(den), (SPAN, HD))
        if r == 1:
            o_ref[h] = o
            l_ref[h] = lse
        else:
            o_ref[h, pl.ds(rho, SPAN, stride=r), :] = o
            l_ref[h, pl.ds(rho, SPAN, stride=r), :] = lse


def _dil_prompt(zr, S):
    B, r, L, _ = zr.shape
    assert L % SPAN == 0
    nb = L // SPAN

    def spec(c, prev):
        if prev:
            return pl.BlockSpec((None, None, SPAN, ATT_COLS), lambda b, i, p: (b, p, jnp.maximum(i - 1, 0), c))
        return pl.BlockSpec((None, None, SPAN, ATT_COLS), lambda b, i, p: (b, p, i, c))

    out_spec = pl.BlockSpec((None, HEADS, SPAN * r, HD), lambda b, i, p: (b, 0, i, 0))
    return pl.pallas_call(
        functools.partial(_dil_body, r),
        grid=(B, nb, r),
        in_specs=[spec(0, False), spec(1, False), spec(1, True), spec(2, False), spec(2, True)],
        out_specs=[out_spec, out_spec],
        out_shape=[jax.ShapeDtypeStruct((B, HEADS, S, HD), F32)] * 2,
        compiler_params=_cparams("parallel", "parallel", "arbitrary"),
        name="dil_prompt_r%d" % r,
    )(zr, zr, zr, zr, zr)


def _dil_mix_body(o0, l0, o1, l1, o2, l2, od_ref):
    for h in range(HEADS):
        a0, a1, a2 = l0[h], l1[h], l2[h]
        m = jnp.maximum(jnp.maximum(a0, a1), a2)
        e0, e1, e2 = jnp.exp(a0 - m), jnp.exp(a1 - m), jnp.exp(a2 - m)
        den = e0 + e1 + e2
        od_ref[:, h * HD:(h + 1) * HD] = (e0 / den) * o0[h] + (e1 / den) * o1[h] + (e2 / den) * o2[h]


def _dil_mix(parts, tm=1024):
    B, _, S, _ = parts[0].shape
    spb = S // tm
    spec = pl.BlockSpec((None, HEADS, tm, HD), lambda i: (i // spb, 0, i % spb, 0))
    return pl.pallas_call(
        _dil_mix_body,
        grid=(B * spb,),
        in_specs=[spec] * 6,
        out_specs=pl.BlockSpec((tm, ATT_COLS), lambda i: (i, 0)),
        out_shape=jax.ShapeDtypeStruct((B * S, ATT_COLS), F32),
        compiler_params=_cparams("parallel"),
        name="dil_mix",
    )(*parts)


def _masked_attn(q, k, v, kn, vn, mask_b, mask_n, scale):
    sb = jnp.where(mask_b, _dot_nt(q, k) * scale, NEG)
    sn = jnp.where(mask_n, _dot_nt(q, kn) * scale, NEG)
    m = jnp.maximum(jnp.max(sb, axis=-1, keepdims=True), jnp.max(sn, axis=-1, keepdims=True))
    eb = jnp.exp(sb - m)
    en = jnp.exp(sn - m)
    den = jnp.sum(eb, axis=-1, keepdims=True) + jnp.sum(en, axis=-1, keepdims=True)
    o = _dot((eb / den).astype(BF16), v) + _dot((en / den).astype(BF16), vn)
    return o, m + jnp.log(den)


def _dil_dec_body(T, q_ref, kn_ref, vn_ref, k0, v0, k1, v1, k2, v2, od_ref):
    scale = HD ** -0.5
    R = T * HEADS
    caches = ((k0, v0), (k1, v1), (k2, v2))
    rn = lax.broadcasted_iota(jnp.int32, (R, R), 0)
    cn = lax.broadcasted_iota(jnp.int32, (R, R), 1)
    outs, lses = [], []
    for gi, (_, r) in enumerate(DIL_GROUPS):
        kc, vc = caches[gi]
        n = kc.shape[0] if len(kc.shape) == 2 else kc.shape[0] * kc.shape[1]
        kb = kc[...].reshape(n, HD).astype(BF16)
        vb = vc[...].reshape(n, HD).astype(BF16)
        rb = lax.broadcasted_iota(jnp.int32, (R, n), 0)
        cb = lax.broadcasted_iota(jnp.int32, (R, n), 1)
        if r == 1:
            mask_b = jnp.logical_and(cb % HEADS == rb % HEADS, cb // HEADS >= rb // HEADS)
            mask_n = jnp.logical_and(cn % HEADS == rn % HEADS, cn // HEADS <= rn // HEADS)
        else:
            mask_b = cb % R == rb
            mask_n = cn == rn
        o, lse = _masked_attn(q_ref[gi].astype(BF16), kb, vb, kn_ref[gi].astype(BF16), vn_ref[gi].astype(BF16),
                              mask_b, mask_n, scale)
        outs.append(o)
        lses.append(lse)
    l0, l1, l2 = lses
    m = jnp.maximum(jnp.maximum(l0, l1), l2)
    e0, e1, e2 = jnp.exp(l0 - m), jnp.exp(l1 - m), jnp.exp(l2 - m)
    den = e0 + e1 + e2
    od_ref[...] = (e0 / den) * outs[0] + (e1 / den) * outs[1] + (e2 / den) * outs[2]


def _dil_decode(zd, caches):
    Bd, T, _ = zd.shape
    R = T * HEADS
    z6 = zd.reshape(Bd, T, N_DIL, 3, HEADS, HD)
    qkv = [jnp.transpose(z6[:, :, :, c], (0, 2, 1, 3, 4)).reshape(Bd, N_DIL, R, HD) for c in range(3)]
    new = pl.BlockSpec((None, N_DIL, R, HD), lambda b: (b, 0, 0, 0))
    args, specs = list(qkv), [new, new, new]
    for (wd, r), (kc, vc) in zip(DIL_GROUPS, caches):
        assert kc.shape[1] == wd and (r == 1 or (T <= r and R % SUBLANES == 0))
        for a in (kc, vc):
            if r <= T:
                args.append(a.reshape(Bd, wd * HEADS, HD))
                specs.append(pl.BlockSpec((None, wd * HEADS, HD), lambda b: (b, 0, 0)))
            else:
                args.append(a.reshape(Bd, wd // r, r * HEADS, HD))
                specs.append(pl.BlockSpec((None, wd // r, R, HD), lambda b: (b, 0, 0, 0)))
    od = pl.pallas_call(
        functools.partial(_dil_dec_body, T),
        grid=(Bd,),
        in_specs=specs,
        out_specs=pl.BlockSpec((None, R, HD), lambda b: (b, 0, 0)),
        out_shape=jax.ShapeDtypeStruct((Bd, R, HD), F32),
        compiler_params=_cparams("parallel"),
        name="dil_decode",
    )(*args)
    return od.reshape(Bd * T, ATT_COLS)


def _mem_body(q_ref, k_ref, v_ref, o_ref):
    scale = HD ** -0.5
    for h in range(HEADS):
        sl = slice(h * HD, (h + 1) * HD)
        s = _dot_nt(q_ref[:, sl].astype(BF16), k_ref[:, sl].astype(BF16)) * scale
        e = jnp.exp(s - jnp.max(s, axis=-1, keepdims=True))
        p = e / jnp.sum(e, axis=-1, keepdims=True)
        o_ref[:, sl] = _dot(p.astype(BF16), v_ref[:, sl].astype(BF16))


def _mem_attn(qm, mkv, tq):
    B, S, _ = qm.shape
    M = mkv.shape[1]
    return pl.pallas_call(
        _mem_body,
        grid=(B, S // tq),
        in_specs=[
            pl.BlockSpec((None, tq, ATT_COLS), lambda b, i: (b, i, 0)),
            pl.BlockSpec((None, M, ATT_COLS), lambda b, i: (b, 0, 0)),
            pl.BlockSpec((None, M, ATT_COLS), lambda b, i: (b, 0, 1)),
        ],
        out_specs=pl.BlockSpec((None, tq, ATT_COLS), lambda b, i: (b, i, 0)),
        out_shape=jax.ShapeDtypeStruct((B, S, ATT_COLS), F32),
        compiler_params=_cparams("parallel", "arbitrary"),
        name="mem_attn",
    )(qm, mkv, mkv)


def _mem_dec_body(q_ref, k_ref, v_ref, o_ref):
    R, n = q_ref.shape[0], k_ref.shape[0]
    rb = lax.broadcasted_iota(jnp.int32, (R, n), 0)
    cb = lax.broadcasted_iota(jnp.int32, (R, n), 1)
    s = _dot_nt(q_ref[...].astype(BF16), k_ref[...].astype(BF16)) * (HD ** -0.5)
    s = jnp.where(cb % HEADS == rb % HEADS, s, -jnp.inf)
    e = jnp.exp(s - jnp.max(s, axis=-1, keepdims=True))
    p = e / jnp.sum(e, axis=-1, keepdims=True)
    o_ref[...] = _dot(p.astype(BF16), v_ref[...].astype(BF16))


def _mem_attn_decode(qm, mk, mv):
    Bd, T, _ = qm.shape
    M = mk.shape[1]
    R = T * HEADS
    kv = pl.BlockSpec((None, M * HEADS, HD), lambda b: (b, 0, 0))
    o = pl.pallas_call(
        _mem_dec_body,
        grid=(Bd,),
        in_specs=[pl.BlockSpec((None, R, HD), lambda b: (b, 0, 0)), kv, kv],
        out_specs=pl.BlockSpec((None, R, HD), lambda b: (b, 0, 0)),
        out_shape=jax.ShapeDtypeStruct((Bd, R, HD), F32),
        compiler_params=_cparams("parallel"),
        name="mem_attn_decode",
    )(qm.reshape(Bd, R, HD), mk.reshape(Bd, M * HEADS, HD), mv.reshape(Bd, M * HEADS, HD))
    return o.reshape(Bd * T, ATT_COLS)


def _merge_body(x_ref, og_ref, r_ref, od_ref, om_ref, gt_ref, ong_ref, wbg_ref, wbd_ref, wbm_ref, wo_ref,
                ln2_ref, wq_ref, h_ref, n2_ref, qh_ref):
    og = og_ref[...]
    r = r_ref[...]
    ong = ong_ref[...]
    parts = []
    for h in range(GLA_HEADS):
        sl = slice(h * GLA_DV, (h + 1) * GLA_DV)
        parts.append(_rms(og[:, sl], ong) * jax.nn.silu(r[:, sl]))
    br_gla = _dot(jnp.concatenate(parts, axis=-1).astype(BF16), wbg_ref[...])
    br_dil = _dot(od_ref[...].astype(BF16), wbd_ref[...])
    br_mem = _dot(om_ref[...].astype(BF16), wbm_ref[...])
    d = D_MODEL
    merged = gt_ref[:, 0:d] * br_gla + gt_ref[:, d:2 * d] * br_dil + gt_ref[:, 2 * d:3 * d] * br_mem
    hres = x_ref[...] + _dot(merged.astype(BF16), wo_ref[...])
    h_ref[...] = hres
    n2 = _rms(hres, ln2_ref[...]).astype(BF16)
    n2_ref[...] = n2
    qh_ref[...] = _dot(n2, wq_ref[...])


def _merge(x, og, zg, od, om, gt, ong, wbg, wbd, wbm, wo, ln2, wq, tm=256):
    nt, d = x.shape
    tm = min(tm, nt)
    nq = wq.shape[1]
    rblk = (2 * GLA_HEADS * GLA_DK + GLA_HEADS * GLA_DV) // d

    def tok(cols, blk=0):
        return pl.BlockSpec((tm, cols), lambda i: (i, blk))

    def const(shape):
        return pl.BlockSpec(shape, lambda i: (0, 0), pipeline_mode=pl.Buffered(1))

    return pl.pallas_call(
        _merge_body,
        grid=(nt // tm,),
        in_specs=[
            tok(d), tok(d), tok(d, rblk), tok(ATT_COLS), tok(ATT_COLS), tok(3 * d),
            const((1, GLA_DV)), const(wbg.shape), const(wbd.shape), const(wbm.shape), const(wo.shape),
            const((1, d)), const(wq.shape),
        ],
        out_specs=[tok(d), tok(d), tok(nq)],
        out_shape=[
            jax.ShapeDtypeStruct((nt, d), F32),
            jax.ShapeDtypeStruct((nt, d), BF16),
            jax.ShapeDtypeStruct((nt, nq), F32),
        ],
        compiler_params=_cparams("parallel"),
        name="merge",
    )(x, og, zg, od, om, gt, ong.reshape(1, GLA_DV), wbg, wbd, wbm, wo, ln2.reshape(1, d), wq)


PEER_CAND_GROUPS = 10


def _peer_cand_layout(tp):
    r = lax.broadcasted_iota(jnp.int32, (PEER_CAND_GROUPS * SUBLANES, tp), 0)
    grp = r // SUBLANES
    p = r % SUBLANES
    K = PEER_TOPK
    k1 = jnp.where(grp < 2, 0, jnp.where(grp < 9, grp - 1, SUBLANES + p))
    k2 = jnp.where(grp < 2, r, jnp.where(grp < 9, p, 0))
    valid = (k1 + 1) * (k2 + 1) <= K
    return (k1 * K + k2).astype(F32), valid


def _peer_cands(x1, x2):
    top, bot = x2[0:SUBLANES], x2[SUBLANES:2 * SUBLANES]
    g1 = [jnp.broadcast_to(x1[0:1], top.shape)] * 2 + [jnp.broadcast_to(x1[k:k + 1], top.shape) for k in range(1, 8)]
    g1.append(x1[SUBLANES:2 * SUBLANES])
    g2 = [top, bot] + [top] * 7 + [jnp.broadcast_to(x2[0:1], top.shape)]
    return jnp.concatenate(g1, axis=0), jnp.concatenate(g2, axis=0)


def _peer_topk_body(qh_ref, keys_ref, a_ref, b_ref, gw_ref):
    tp = qh_ref.shape[0]
    K = PEER_TOPK
    NK = PEER_NKEYS
    rowf = lax.broadcasted_iota(jnp.int32, (NK, tp), 0).astype(F32)
    row16 = lax.broadcasted_iota(jnp.int32, (K, tp), 0)
    pos, valid = _peer_cand_layout(tp)
    a_rows, b_rows, g_rows = [], [], []
    for h in range(PEER_HEADS):
        s0 = _dot_nt(keys_ref[2 * h], qh_ref[:, 2 * h * LANES:(2 * h + 1) * LANES].astype(BF16))
        s1 = _dot_nt(keys_ref[2 * h + 1], qh_ref[:, (2 * h + 1) * LANES:(2 * h + 2) * LANES].astype(BF16))

        def step1(k, carry):
            out = []
            for s, sv, si in (carry[0:3], carry[3:6]):
                mx = jnp.max(s, axis=0, keepdims=True)
                idx = jnp.min(jnp.where(s == mx, rowf, float(NK)), axis=0, keepdims=True)
                out += [jnp.where(rowf == idx, -jnp.inf, s), jnp.where(row16 == k, mx, sv), jnp.where(row16 == k, idx, si)]
            return tuple(out)

        z = jnp.zeros((K, tp), F32)
        _, sv0, si0, _, sv1, si1 = lax.fori_loop(0, K, step1, (s0, z, z, s1, z, z))

        c1, c2 = _peer_cands(sv0, sv1)
        e1, e2 = _peer_cands(si0, si1)
        cand = jnp.where(valid, c1 + c2, -jnp.inf)
        eid = e1 * float(NK) + e2

        def step2(k, carry):
            cand, sc, ev = carry
            mx = jnp.max(cand, axis=0, keepdims=True)
            first = jnp.min(jnp.where(cand == mx, pos, float(K * K)), axis=0, keepdims=True)
            hit = pos == first
            e_k = jnp.max(jnp.where(hit, eid, -1.0), axis=0, keepdims=True)
            return jnp.where(hit, -jnp.inf, cand), jnp.where(row16 == k, mx, sc), jnp.where(row16 == k, e_k, ev)

        _, sc, ev = lax.fori_loop(0, K, step2, (cand, z, z))
        e = jnp.exp(sc - jnp.max(sc, axis=0, keepdims=True))
        g_rows.append(e / jnp.sum(e, axis=0, keepdims=True))
        a = jnp.floor(ev * (1.0 / NK))
        a_rows.append(a)
        b_rows.append(ev - a * float(NK))
    a_ref[...] = jnp.transpose(jnp.concatenate(a_rows, axis=0))
    b_ref[...] = jnp.transpose(jnp.concatenate(b_rows, axis=0))
    gw_ref[...] = jnp.transpose(jnp.concatenate(g_rows, axis=0))


def _peer_topk(qh, keys, tp=128):
    nt, nq = qh.shape
    tp = min(tp, nt)
    spec = pl.BlockSpec((tp, LANES), lambda i: (i, 0))
    shp = jax.ShapeDtypeStruct((nt, LANES), F32)
    return pl.pallas_call(
        _peer_topk_body,
        grid=(nt // tp,),
        in_specs=[
            pl.BlockSpec((tp, nq), lambda i: (i, 0)),
            pl.BlockSpec(keys.shape, lambda i: (0, 0, 0)),
        ],
        out_specs=[spec, spec, spec],
        out_shape=[shp, shp, shp],
        compiler_params=_cparams("parallel"),
        name="peer_topk",
    )(qh, keys)


def _peer_w_body(a_ref, b_ref, gw_ref, w_ref, scr):
    tw = a_ref.shape[0]
    NK = PEER_NKEYS
    stride = scr.shape[0] // NK
    sub = lax.broadcasted_iota(jnp.int32, (NK, LANES), 0).astype(F32)

    zero = jnp.zeros((NK, LANES), BF16)

    def tok2(p, carry):
        oas, gbs = [], []
        for t in (2 * p, 2 * p + 1):
            arow = a_ref[pl.ds(t, 1), :]
            brow = b_ref[pl.ds(t, 1), :]
            grow = gw_ref[pl.ds(t, 1), :]
            oas.append((arow == sub).astype(BF16))
            gbs.append(jnp.where(brow == sub, grow, 0.0).astype(BF16))
        lhs = jnp.concatenate(oas, axis=1)
        rhs = jnp.concatenate([jnp.concatenate([gbs[0], zero], axis=1),
                               jnp.concatenate([zero, gbs[1]], axis=1)], axis=0)
        w2 = _dot_nt(lhs, rhs)
        scr[pl.ds(2 * p, NK, stride=stride), :] = w2[:, 0:LANES]
        scr[pl.ds(2 * p + 1, NK, stride=stride), :] = w2[:, LANES:2 * LANES]
        return carry

    lax.fori_loop(0, tw // 2, tok2, 0, unroll=16)
    for i1 in range(NK):
        w_ref[:, i1 * NK:(i1 + 1) * NK] = scr[i1 * stride:i1 * stride + tw, :].astype(BF16)


def _peer_weights(a, b, gw, n_experts, tw=128):
    nt = a.shape[0]
    tw = min(tw, nt)
    stride = tw + SUBLANES
    tok = pl.BlockSpec((tw, LANES), lambda i: (i, 0))
    return pl.pallas_call(
        _peer_w_body,
        grid=(nt // tw,),
        in_specs=[tok, tok, tok],
        out_specs=pl.BlockSpec((tw, n_experts), lambda i: (i, 0)),
        out_shape=jax.ShapeDtypeStruct((nt, n_experts), BF16),
        scratch_shapes=[pltpu.VMEM((PEER_NKEYS * stride, LANES), F32)],
        compiler_params=_cparams("parallel"),
        name="peer_weights",
    )(a, b, gw)


def _peer_mix_body(n2_ref, w_ref, h_ref, u_ref, v_ref, y_ref, acc):
    j = pl.program_id(1)

    @pl.when(j == 0)
    def _():
        acc[...] = jnp.zeros(acc.shape, F32)

    hpre = _dot_nt(n2_ref[...], u_ref[...])
    gelu = 0.5 * hpre * (1.0 + lax.erf(hpre * (2.0 ** -0.5)))
    act = (gelu * w_ref[...].astype(F32)).astype(BF16)
    acc[...] += _dot(act, v_ref[...])

    @pl.when(j == pl.num_programs(1) - 1)
    def _():
        y_ref[...] = h_ref[...] + acc[...]


def _peer_mix(n2, w, h, u, v):
    nt, d = n2.shape
    T = min(PEER_T, nt)
    assert nt % T == 0 and u.shape[0] % PEER_EB == 0
    tok = pl.BlockSpec((T, d), lambda i, j: (i, 0))
    tab = pl.BlockSpec((PEER_EB, d), lambda i, j: (j, 0))
    return pl.pallas_call(
        _peer_mix_body,
        grid=(nt // T, u.shape[0] // PEER_EB),
        in_specs=[tok, pl.BlockSpec((T, PEER_EB), lambda i, j: (i, j)), tok, tab, tab],
        out_specs=tok,
        out_shape=jax.ShapeDtypeStruct((nt, d), F32),
        scratch_shapes=[pltpu.VMEM((T, d), F32)],
        compiler_params=_cparams("parallel", "arbitrary"),
        name="peer_mix",
    )(n2, w, h, u, v)


def _rope_tables(pos):
    half = HD // 2
    inv = ROPE_THETA ** (-jnp.arange(half, dtype=F32) / half)
    ang = pos.astype(F32)[:, None] * inv[None, :]
    cos, sin = jnp.cos(ang), jnp.sin(ang)
    return jnp.concatenate([cos, cos], axis=-1), jnp.concatenate([-sin, sin], axis=-1)


def _prep_weights(w_in, gla_wg2, peer_wq, peer_keys, peer_u, peer_v, w_br_gla, w_br_dil, w_br_mem, w_out):
    c0 = GLA_COLS
    c1 = c0 + GLA_RANK
    c2 = c1 + DIL_COLS
    c3 = c2 + ATT_COLS
    wa = jnp.pad(w_in[:, c0:c1], ((0, 0), (0, LANES - GLA_RANK))).astype(BF16)
    wg2 = jnp.pad(gla_wg2, ((0, LANES - GLA_RANK), (0, 0))).astype(BF16)
    return dict(
        w_gla=w_in[:, :c0].astype(BF16), wa=wa, wg2=wg2,
        w_dil=w_in[:, c1:c2].astype(BF16), w_qm=w_in[:, c2:c3].astype(BF16), w_gt=w_in[:, c3:].astype(BF16),
        wbg=w_br_gla.astype(BF16), wbd=w_br_dil.astype(BF16), wbm=w_br_mem.astype(BF16), wo=w_out.astype(BF16),
        wq=peer_wq.astype(BF16),
        keys=peer_keys.reshape(PEER_HEADS * 2, PEER_NKEYS, LANES).astype(BF16),
        u=peer_u.astype(BF16), v=peer_v.astype(BF16),
    )


def _layer(x, pos, mem_k, mem_v, s0, caches, p, W):
    B, S, d = x.shape
    nt = B * S
    xt = x.reshape(nt, d)
    ln1 = p["ln1_g"]
    zg = _proj("gla", xt, ln1, W["w_gla"])
    lg = _gla_gate(xt, ln1, W["wa"], W["wg2"], p["gla_bg"])
    cosf, sinf = _rope_tables(pos)
    cosf = jnp.broadcast_to(cosf[None], (B, S, HD)).reshape(nt, HD)
    sinf = jnp.broadcast_to(sinf[None], (B, S, HD)).reshape(nt, HD)
    hg = jnp.stack([p["dil_qn_g"], p["dil_kn_g"]])
    qm = _proj("mem_q", xt, ln1, W["w_qm"], extra=(p["mem_qn_g"].reshape(1, HD),))
    gt = _proj("sigmoid", xt, ln1, W["w_gt"])

    prompt = caches is None
    og, s_fin = _gla(zg.reshape(B, S, GLA_COLS), lg.reshape(B, S, GLA_HEADS * GLA_DK), s0,
                     GLA_CHUNK if prompt else S)

    new_bufs = []
    gcols = 3 * ATT_COLS
    if prompt:
        parts = []
        for gi, (wd, r) in enumerate(DIL_GROUPS):
            zr = _proj_dil_group(xt, ln1, W["w_dil"][:, gi * gcols:(gi + 1) * gcols], hg, cosf, sinf, B, S, r)
            parts.extend(_dil_prompt(zr, S))
            keep = min(wd, S)
            kv = zr[:, :, (S - keep) // r:, ATT_COLS:]
            kv = jnp.swapaxes(kv, 1, 2).reshape(B, keep, 2, HEADS, HD)
            new_bufs.append((kv[:, :, 0], kv[:, :, 1]))
        od = _dil_mix(parts)
    else:
        zd3 = _proj("dil", xt, ln1, W["w_dil"], extra=(hg, cosf, sinf)).reshape(B, S, DIL_COLS)
        od = _dil_decode(zd3, caches).reshape(nt, ATT_COLS)
        for gi, (wd, r) in enumerate(DIL_GROUPS):
            kb, vb = caches[gi]
            kn = zd3[..., gi * gcols + ATT_COLS:gi * gcols + 2 * ATT_COLS].reshape(B, S, HEADS, HD)
            vn = zd3[..., gi * gcols + 2 * ATT_COLS:(gi + 1) * gcols].reshape(B, S, HEADS, HD)
            keep = min(wd, kb.shape[1] + S)
            new_bufs.append((jnp.concatenate([kb, kn], axis=1)[:, -keep:], jnp.concatenate([vb, vn], axis=1)[:, -keep:]))

    qm3 = qm.reshape(B, S, ATT_COLS)
    if prompt:
        om = _mem_attn(qm3, mem_k, 512).reshape(nt, ATT_COLS)
    else:
        om = _mem_attn_decode(qm3, mem_k, mem_v)

    h, n2, qh = _merge(xt, og.reshape(nt, d), zg, od, om, gt, p["gla_onorm_g"], W["wbg"], W["wbd"], W["wbm"],
                       W["wo"], p["ln2_g"], W["wq"])
    a, b, gw = _peer_topk(qh, W["keys"])
    y = _peer_mix(n2, _peer_weights(a, b, gw, W["u"].shape[0]), h, W["u"], W["v"])
    return y.reshape(B, S, d), s_fin, new_bufs


def kernel(x_prompt, x_sample, mem_prompt, state_gla, cache_dil_k0, cache_dil_v0, cache_dil_k1, cache_dil_v1, cache_dil_k2, cache_dil_v2, cache_mem_k, cache_mem_v, ln1_g, w_in, gla_wg2, gla_bg, gla_onorm_g, dil_qn_g, dil_kn_g, mem_norm_g, w_mem_kv, mem_qn_g, mem_kn_g, w_br_gla, w_br_dil, w_br_mem, w_out, ln2_g, peer_wq, peer_keys, peer_u, peer_v):
    B, S, d = x_prompt.shape
    Bd, T, _ = x_sample.shape
    p = dict(ln1_g=ln1_g, gla_bg=gla_bg, gla_onorm_g=gla_onorm_g, dil_qn_g=dil_qn_g, dil_kn_g=dil_kn_g,
             mem_qn_g=mem_qn_g, ln2_g=ln2_g)
    W = _prep_weights(w_in, gla_wg2, peer_wq, peer_keys, peer_u, peer_v, w_br_gla, w_br_dil, w_br_mem, w_out)

    M = mem_prompt.shape[1]
    mkv = _proj("mem_kv", mem_prompt.reshape(B * M, d), mem_norm_g, w_mem_kv.astype(BF16),
                extra=(mem_kn_g.reshape(1, HD),))
    mem_k_p = mkv[:, :ATT_COLS].reshape(B, M, HEADS, HD)
    mem_v_p = mkv[:, ATT_COLS:].reshape(B, M, HEADS, HD)

    s0 = jnp.zeros((B, GLA_HEADS, GLA_DK, GLA_DV), F32)
    mkv3 = mkv.reshape(B, M, 2 * ATT_COLS)
    y_prompt, gla_state_p, bufs_p = _layer(x_prompt, jnp.arange(S, dtype=jnp.int32), mkv3, mkv3, s0, None, p, W)

    caches = ((cache_dil_k0, cache_dil_v0), (cache_dil_k1, cache_dil_v1), (cache_dil_k2, cache_dil_v2))
    pos_s = PAST_LEN + jnp.arange(T, dtype=jnp.int32)
    y_sample, gla_state_s, bufs_s = _layer(x_sample, pos_s, cache_mem_k, cache_mem_v, state_gla, caches, p, W)

    (dk0_p, dv0_p), (dk1_p, dv1_p), (dk2_p, dv2_p) = bufs_p
    (dk0_s, dv0_s), (dk1_s, dv1_s), (dk2_s, dv2_s) = bufs_s
    return (y_prompt, y_sample,
            gla_state_p, dk0_p, dv0_p, dk1_p, dv1_p, dk2_p, dv2_p, mem_k_p, mem_v_p,
            gla_state_s, dk0_s, dv0_s, dk1_s, dv1_s, dk2_s, dv2_s)
```

```python
import functools

import jax
import jax.numpy as jnp
from jax import lax
from jax.experimental import pallas as pl
from jax.experimental.pallas import tpu as pltpu

F32 = jnp.float32
BF16 = jnp.bfloat16

D_MODEL = 1024
PAST_LEN = 8192
GLA_HEADS = 4
GLA_DK = 128
GLA_DV = 256
GLA_RANK = 16
GLA_TAU = 16.0
DIL_GROUPS = ((128, 1), (512, 4), (2048, 16))
N_DIL = 3
HEADS = 4
HD = 128
SPAN = 128
MEM_LEN = 256
ROPE_THETA = 10000.0
PEER_HEADS = 8
PEER_NKEYS = 128
PEER_TOPK = 16
EPS = 1e-6
NEG = -1e30

LANES = 128
SUBLANES = 8
GLA_COLS = 2 * GLA_HEADS * GLA_DK + 2 * GLA_HEADS * GLA_DV
DIL_COLS = N_DIL * 3 * HEADS * HD
ATT_COLS = HEADS * HD
VMEM_LIMIT = 56 * 1024 * 1024

PEER_T = 1024
PEER_EB = 1024


def _cparams(*sem):
    return pltpu.CompilerParams(dimension_semantics=sem, vmem_limit_bytes=VMEM_LIMIT)


def _rms(x, g):
    return x * lax.rsqrt(jnp.mean(x * x, axis=-1, keepdims=True) + EPS) * g


def _dot(a, b):
    return jnp.dot(a, b, preferred_element_type=F32)


def _dot_nt(a, b):
    return lax.dot_general(a, b, (((1,), (1,)), ((), ())), preferred_element_type=F32)


def _split3(x):
    hi = x.astype(BF16)
    r1 = x - hi.astype(F32)
    mid = r1.astype(BF16)
    lo = (r1 - mid.astype(F32)).astype(BF16)
    return hi, mid, lo


def _proj_body(kind, *refs):
    if kind == "dil":
        x_ref, g_ref, w_ref, hg_ref, cos_ref, sin_ref, o_ref, n_scr = refs
    elif kind in ("mem_q", "mem_kv"):
        x_ref, g_ref, w_ref, hg_ref, o_ref, n_scr = refs
    else:
        x_ref, g_ref, w_ref, o_ref, n_scr = refs
    j = pl.program_id(1)

    @pl.when(j == 0)
    def _():
        n_scr[...] = _rms(x_ref[...], g_ref[...]).astype(BF16)

    z = _dot(n_scr[...], w_ref[...])
    if kind == "gla":
        col = lax.broadcasted_iota(jnp.int32, (1, z.shape[1]), 1) + j * z.shape[1]
        o_ref[...] = z * jnp.where(col < GLA_HEADS * GLA_DK, GLA_DK ** -0.5, 1.0).astype(F32)
    elif kind == "sigmoid":
        o_ref[...] = jax.nn.sigmoid(z)
    elif kind == "mem_q":
        hg = hg_ref[...]
        o_ref[...] = jnp.concatenate([_rms(z[:, h * HD:(h + 1) * HD], hg) for h in range(HEADS)], axis=-1)
    elif kind == "mem_kv":
        @pl.when(j == 0)
        def _():
            hg = hg_ref[...]
            o_ref[...] = jnp.concatenate([_rms(z[:, h * HD:(h + 1) * HD], hg) for h in range(HEADS)], axis=-1)

        @pl.when(j != 0)
        def _():
            o_ref[...] = z
    elif kind == "dil":
        c = j % 3

        @pl.when(c == 2)
        def _():
            o_ref[...] = z

        @pl.when(c != 2)
        def _():
            hg = jnp.where(c == 0, hg_ref[0:1, :], hg_ref[1:2, :])
            cs = cos_ref[...]
            sn = sin_ref[...]
            parts = []
            for h in range(HEADS):
                zh = _rms(z[:, h * HD:(h + 1) * HD], hg)
                parts.append(zh * cs + pltpu.roll(zh, HD // 2, 1) * sn)
            o_ref[...] = jnp.concatenate(parts, axis=-1)
    else:
        raise ValueError(kind)


def _proj(kind, x, g, w, extra=(), tm=1024, tn=1024):
    nt, d = x.shape
    ncol = w.shape[1]
    tm = min(tm, nt)
    if kind in ("dil", "mem_q", "mem_kv"):
        tn = ATT_COLS
    assert nt % tm == 0 and ncol % tn == 0
    in_specs = [
        pl.BlockSpec((tm, d), lambda i, j: (i, 0)),
        pl.BlockSpec((1, d), lambda i, j: (0, 0)),
        pl.BlockSpec((d, tn), lambda i, j: (0, j)),
    ]
    if kind == "dil":
        in_specs += [
            pl.BlockSpec((2, HD), lambda i, j: (0, 0)),
            pl.BlockSpec((tm, HD), lambda i, j: (i, 0)),
            pl.BlockSpec((tm, HD), lambda i, j: (i, 0)),
        ]
    elif kind in ("mem_q", "mem_kv"):
        in_specs += [pl.BlockSpec((1, HD), lambda i, j: (0, 0))]
    return pl.pallas_call(
        functools.partial(_proj_body, kind),
        grid=(nt // tm, ncol // tn),
        in_specs=in_specs,
        out_specs=pl.BlockSpec((tm, tn), lambda i, j: (i, j)),
        out_shape=jax.ShapeDtypeStruct((nt, ncol), F32),
        scratch_shapes=[pltpu.VMEM((tm, d), BF16)],
        compiler_params=_cparams("parallel", "arbitrary"),
        name="proj_" + kind,
    )(x, g.reshape(1, d), w, *extra)


def _gate_body(x_ref, g_ref, wa_ref, wg2_ref, bg_ref, o_ref):
    n = _rms(x_ref[...], g_ref[...]).astype(BF16)
    a = _dot(n, wa_ref[...])
    pre = _dot(a.astype(BF16), wg2_ref[...]) + bg_ref[...]
    o_ref[...] = jax.nn.log_sigmoid(pre) / GLA_TAU


def _gla_gate(x, g, wa, wg2, bg, tm=512):
    nt, d = x.shape
    tm = min(tm, nt)
    ncol = wg2.shape[1]
    return pl.pallas_call(
        _gate_body,
        grid=(nt // tm,),
        in_specs=[
            pl.BlockSpec((tm, d), lambda i: (i, 0)),
            pl.BlockSpec((1, d), lambda i: (0, 0)),
            pl.BlockSpec((d, LANES), lambda i: (0, 0)),
            pl.BlockSpec((LANES, ncol), lambda i: (0, 0)),
            pl.BlockSpec((1, ncol), lambda i: (0, 0)),
        ],
        out_specs=pl.BlockSpec((tm, ncol), lambda i: (i, 0)),
        out_shape=jax.ShapeDtypeStruct((nt, ncol), F32),
        compiler_params=_cparams("parallel"),
        name="gla_gate",
    )(x, g.reshape(1, d), wa, wg2, bg.reshape(1, ncol))


GLA_CHUNK = 128
GLA_SUB = 16


def _gla_intra(q, k, b, qs_scr, bs_scr, h, n_sub):
    P, SB = GLA_CHUNK, GLA_SUB
    rowp = lax.broadcasted_iota(jnp.int32, (P, 1), 0)
    qparts, kparts = [], []
    for j in range(n_sub - 1):
        lo, hi = SB * j, SB * (j + 1)
        be = b[hi - 1:hi, :]
        qparts.append(jnp.where(rowp >= hi, q * jnp.exp(b - be), 0.0).astype(BF16))
        kblk = (k[lo:hi] * jnp.exp(be - b[lo:hi])).astype(BF16)
        pieces = [kblk]
        if lo:
            pieces.insert(0, jnp.zeros((lo, k.shape[1]), BF16))
        pieces.append(jnp.zeros((P - hi, k.shape[1]), BF16))
        kparts.append(jnp.concatenate(pieces, axis=0))
    if qparts:
        a_off = _dot_nt(jnp.concatenate(qparts, axis=1), jnp.concatenate(kparts, axis=1))
    else:
        a_off = jnp.zeros((P, P), F32)
    lane = lax.broadcasted_iota(jnp.int32, (SB, P), 1)
    srow = lax.broadcasted_iota(jnp.int32, (SB, P), 0)
    tiles = []
    for j in range(P // SB):
        if j >= n_sub:
            tiles.append(jnp.zeros((SB, P), F32))
            continue
        lo, hi = SB * j, SB * (j + 1)
        kj, bj = k[lo:hi], b[lo:hi]
        at = jnp.zeros((SB, P), F32)
        for t in range(lo, hi):
            bt = bs_scr[h, pl.ds(t, 1), :]
            qt = qs_scr[h, pl.ds(t, 1), :]
            colv = jnp.sum(qt * kj * jnp.exp(jnp.minimum(bt - bj, 0.0)), axis=-1, keepdims=True)
            at = jnp.where(lane == t, colv, at)
        tiles.append(jnp.where(srow + lo <= lane, at, 0.0))
    return a_off + jnp.transpose(jnp.concatenate(tiles, axis=0))


def _gla_body(tc, q_ref, k_ref, v_ref, g_ref, s0_ref, o_ref, sf_ref, s_scr, qp, kp, vp, gp, qs_scr, bs_scr):
    c = pl.program_id(1)
    P = GLA_CHUNK
    H, dk, dv = GLA_HEADS, GLA_DK, GLA_DV
    n_sub = -(-tc // GLA_SUB)

    @pl.when(c == 0)
    def _():
        s_scr[...] = s0_ref[...]

    if tc == P:
        q, k, v, g = q_ref[...], k_ref[...], v_ref[...], g_ref[...]
    else:
        for pad, ref in ((qp, q_ref), (kp, k_ref), (vp, v_ref), (gp, g_ref)):
            pad[...] = jnp.zeros(pad.shape, F32)
            pad[0:tc, :] = ref[...]
        q, k, v, g = qp[...], kp[...], vp[...], gp[...]

    row = lax.broadcasted_iota(jnp.int32, (P, P), 0)
    col = lax.broadcasted_iota(jnp.int32, (P, P), 1)
    trib = (row >= col).astype(BF16)
    g1, g2, g3 = _split3(g)
    b = _dot(trib, g1) + _dot(trib, g2) + _dot(trib, g3)
    b_end = b[P - 1:P, :]
    qe = (q * jnp.exp(b)).astype(BF16)
    kd = k * jnp.exp(b_end - b)
    vb = v.astype(BF16)
    for h in range(H):
        ks = slice(h * dk, (h + 1) * dk)
        qs_scr[h] = q[:, ks]
        bs_scr[h] = b[:, ks]
    for h in range(H):
        ks = slice(h * dk, (h + 1) * dk)
        vs = slice(h * dv, (h + 1) * dv)
        a = _gla_intra(q[:, ks], k[:, ks], b[:, ks], qs_scr, bs_scr, h, n_sub)
        s = s_scr[h]
        o = _dot(qe[:, ks], s.astype(BF16)) + _dot(a.astype(BF16), vb[:, vs])
        o_ref[:, vs] = o[0:tc, :]
        decay = jnp.exp(jnp.sum(jnp.transpose(g[:, ks]), axis=1, keepdims=True))
        s_new = decay * s + _dot(jnp.transpose(kd[:, ks]).astype(BF16), vb[:, vs])
        s_scr[h] = s_new

        @pl.when(c == pl.num_programs(1) - 1)
        def _():
            sf_ref[h] = s_new


def _gla(zg, lg, s0, tc):
    B, S, _ = zg.shape
    H, dk, dv = GLA_HEADS, GLA_DK, GLA_DV
    assert S % tc == 0
    P = GLA_CHUNK
    qk = H * dk
    st = pl.BlockSpec((None, H, dk, dv), lambda b, c: (b, 0, 0, 0))
    return pl.pallas_call(
        functools.partial(_gla_body, tc),
        grid=(B, S // tc),
        in_specs=[
            pl.BlockSpec((None, tc, qk), lambda b, c: (b, c, 0)),
            pl.BlockSpec((None, tc, qk), lambda b, c: (b, c, 1)),
            pl.BlockSpec((None, tc, H * dv), lambda b, c: (b, c, 2 * qk // (H * dv))),
            pl.BlockSpec((None, tc, qk), lambda b, c: (b, c, 0)),
            st,
        ],
        out_specs=[pl.BlockSpec((None, tc, H * dv), lambda b, c: (b, c, 0)), st],
        out_shape=[
            jax.ShapeDtypeStruct((B, S, H * dv), F32),
            jax.ShapeDtypeStruct((B, H, dk, dv), F32),
        ],
        scratch_shapes=[
            pltpu.VMEM((H, dk, dv), F32),
            pltpu.VMEM((P, qk), F32), pltpu.VMEM((P, qk), F32), pltpu.VMEM((P, H * dv), F32), pltpu.VMEM((P, qk), F32),
            pltpu.VMEM((H, P, dk), F32), pltpu.VMEM((H, P, dk), F32),
        ],
        compiler_params=_cparams("parallel", "arbitrary"),
        name="gla",
    )(zg, zg, zg, lg, s0)


def _proj_dil_body(r, x_ref, g_ref, w_ref, hg_ref, cos_ref, sin_ref, o_ref, n_scr, z_scr):
    j = pl.program_id(1)

    @pl.when(j == 0)
    def _():
        n_scr[...] = _rms(x_ref[...], g_ref[...]).astype(BF16)

    z = _dot(n_scr[...], w_ref[...])

    @pl.when(j == 2)
    def _():
        for h in range(HEADS):
            z_scr[h] = z[:, h * HD:(h + 1) * HD]

    @pl.when(j != 2)
    def _():
        hg = jnp.where(j == 0, hg_ref[0:1, :], hg_ref[1:2, :])
        cs = cos_ref[...]
        sn = sin_ref[...]
        for h in range(HEADS):
            zh = _rms(z[:, h * HD:(h + 1) * HD], hg)
            z_scr[h] = zh * cs + pltpu.roll(zh, HD // 2, 1) * sn

    rows = z_scr.shape[1] // r
    for rho in range(r):
        for h in range(HEADS):
            zh = z_scr[h, pl.ds(rho, rows, stride=r), :] if r > 1 else z_scr[h]
            o_ref[rho, :, h * HD:(h + 1) * HD] = zh


def _proj_dil_group(x, g, w, hg, cosf, sinf, B, S, r, tm=1024):
    nt, d = x.shape
    assert S % tm == 0 and tm % (r * SUBLANES) == 0
    spb = S // tm
    return pl.pallas_call(
        functools.partial(_proj_dil_body, r),
        grid=(nt // tm, 3),
        in_specs=[
            pl.BlockSpec((tm, d), lambda i, j: (i, 0)),
            pl.BlockSpec((1, d), lambda i, j: (0, 0)),
            pl.BlockSpec((d, ATT_COLS), lambda i, j: (0, j)),
            pl.BlockSpec((2, HD), lambda i, j: (0, 0)),
            pl.BlockSpec((tm, HD), lambda i, j: (i, 0)),
            pl.BlockSpec((tm, HD), lambda i, j: (i, 0)),
        ],
        out_specs=pl.BlockSpec((None, r, tm // r, ATT_COLS), lambda i, j: (i // spb, 0, i % spb, j)),
        out_shape=jax.ShapeDtypeStruct((B, r, S // r, 3 * ATT_COLS), F32),
        scratch_shapes=[pltpu.VMEM((tm, d), BF16), pltpu.VMEM((HEADS, tm, HD), F32)],
        compiler_params=_cparams("parallel", "arbitrary"),
        name="proj_dil_r%d" % r,
    )(x, g.reshape(1, d), w, hg, cosf, sinf)


def _dil_body(r, q_ref, kc_ref, kp_ref, vc_ref, vp_ref, o_ref, l_ref):
    i = pl.program_id(1)
    rho = pl.program_id(2)
    scale = HD ** -0.5
    mq = lax.broadcasted_iota(jnp.int32, (SPAN, SPAN), 0)
    mk = lax.broadcasted_iota(jnp.int32, (SPAN, SPAN), 1)
    mask_c = mk <= mq
    mask_p = jnp.logical_and(mk >= mq, i > 0)
    sls = [slice(h * HD, (h + 1) * HD) for h in range(HEADS)]
    qs = [q_ref[:, sl].astype(BF16) for sl in sls]
    scs = [jnp.where(mask_c, _dot_nt(q, kc_ref[:, sl].astype(BF16)) * scale, NEG) for q, sl in zip(qs, sls)]
    sps = [jnp.where(mask_p, _dot_nt(q, kp_ref[:, sl].astype(BF16)) * scale, NEG) for q, sl in zip(qs, sls)]
    ms = [jnp.maximum(jnp.max(sc, axis=-1, keepdims=True), jnp.max(sp, axis=-1, keepdims=True))
          for sc, sp in zip(scs, sps)]
    ecs = [jnp.exp(sc - m) for sc, m in zip(scs, ms)]
    eps = [jnp.exp(sp - m) for sp, m in zip(sps, ms)]
    dens = [jnp.sum(ec, axis=-1, keepdims=True) + jnp.sum(ep, axis=-1, keepdims=True) for ec, ep in zip(ecs, eps)]
    for h, sl in enumerate(sls):
        o = (_dot((ecs[h] / dens[h]).astype(BF16), vc_ref[:, sl].astype(BF16))
             + _dot((eps[h] / dens[h]).astype(BF16), vp_ref[:, sl].astype(BF16)))
        lse = jnp.broadcast_to(ms[h] + jnp.log(dens[h]), (SPAN, HD))
        if r == 1:
            o_ref[h] = o
            l_ref[h] = lse
        else:
            o_ref[h, pl.ds(rho, SPAN, stride=r), :] = o
            l_ref[h, pl.ds(rho, SPAN, stride=r), :] = lse


def _dil_prompt(zr, S):
    B, r, L, _ = zr.shape
    assert L % SPAN == 0
    nb = L // SPAN

    def spec(c, prev):
        if prev:
            return pl.BlockSpec((None, None, SPAN, ATT_COLS), lambda b, i, p: (b, p, jnp.maximum(i - 1, 0), c))
        return pl.BlockSpec((None, None, SPAN, ATT_COLS), lambda b, i, p: (b, p, i, c))

    out_spec = pl.BlockSpec((None, HEADS, SPAN * r, HD), lambda b, i, p: (b, 0, i, 0))
    return pl.pallas_call(
        functools.partial(_dil_body, r),
        grid=(B, nb, r),
        in_specs=[spec(0, False), spec(1, False), spec(1, True), spec(2, False), spec(2, True)],
        out_specs=[out_spec, out_spec],
        out_shape=[jax.ShapeDtypeStruct((B, HEADS, S, HD), F32)] * 2,
        compiler_params=_cparams("parallel", "parallel", "arbitrary"),
        name="dil_prompt_r%d" % r,
    )(zr, zr, zr, zr, zr)


def _dil_mix_body(o0, l0, o1, l1, o2, l2, od_ref):
    for h in range(HEADS):
        a0, a1, a2 = l0[h], l1[h], l2[h]
        m = jnp.maximum(jnp.maximum(a0, a1), a2)
        e0, e1, e2 = jnp.exp(a0 - m), jnp.exp(a1 - m), jnp.exp(a2 - m)
        den = e0 + e1 + e2
        od_ref[:, h * HD:(h + 1) * HD] = (e0 / den) * o0[h] + (e1 / den) * o1[h] + (e2 / den) * o2[h]


def _dil_mix(parts, tm=1024):
    B, _, S, _ = parts[0].shape
    spb = S // tm
    spec = pl.BlockSpec((None, HEADS, tm, HD), lambda i: (i // spb, 0, i % spb, 0))
    return pl.pallas_call(
        _dil_mix_body,
        grid=(B * spb,),
        in_specs=[spec] * 6,
        out_specs=pl.BlockSpec((tm, ATT_COLS), lambda i: (i, 0)),
        out_shape=jax.ShapeDtypeStruct((B * S, ATT_COLS), F32),
        compiler_params=_cparams("parallel"),
        name="dil_mix",
    )(*parts)


def _masked_attn(q, k, v, kn, vn, mask_b, mask_n, scale):
    sb = jnp.where(mask_b, _dot_nt(q, k) * scale, NEG)
    sn = jnp.where(mask_n, _dot_nt(q, kn) * scale, NEG)
    m = jnp.maximum(jnp.max(sb, axis=-1, keepdims=True), jnp.max(sn, axis=-1, keepdims=True))
    eb = jnp.exp(sb - m)
    en = jnp.exp(sn - m)
    den = jnp.sum(eb, axis=-1, keepdims=True) + jnp.sum(en, axis=-1, keepdims=True)
    o = _dot((eb / den).astype(BF16), v) + _dot((en / den).astype(BF16), vn)
    return o, m + jnp.log(den)


def _dil_dec_body(T, q_ref, kn_ref, vn_ref, k0, v0, k1, v1, k2, v2, od_ref):
    scale = HD ** -0.5
    R = T * HEADS
    caches = ((k0, v0), (k1, v1), (k2, v2))
    rn = lax.broadcasted_iota(jnp.int32, (R, R), 0)
    cn = lax.broadcasted_iota(jnp.int32, (R, R), 1)
    outs, lses = [], []
    for gi, (_, r) in enumerate(DIL_GROUPS):
        kc, vc = caches[gi]
        n = kc.shape[0] if len(kc.shape) == 2 else kc.shape[0] * kc.shape[1]
        kb = kc[...].reshape(n, HD).astype(BF16)
        vb = vc[...].reshape(n, HD).astype(BF16)
        rb = lax.broadcasted_iota(jnp.int32, (R, n), 0)
        cb = lax.broadcasted_iota(jnp.int32, (R, n), 1)
        if r == 1:
            mask_b = jnp.logical_and(cb % HEADS == rb % HEADS, cb // HEADS >= rb // HEADS)
            mask_n = jnp.logical_and(cn % HEADS == rn % HEADS, cn // HEADS <= rn // HEADS)
        else:
            mask_b = cb % R == rb
            mask_n = cn == rn
        o, lse = _masked_attn(q_ref[gi].astype(BF16), kb, vb, kn_ref[gi].astype(BF16), vn_ref[gi].astype(BF16),
                              mask_b, mask_n, scale)
        outs.append(o)
        lses.append(lse)
    l0, l1, l2 = lses
    m = jnp.maximum(jnp.maximum(l0, l1), l2)
    e0, e1, e2 = jnp.exp(l0 - m), jnp.exp(l1 - m), jnp.exp(l2 - m)
    den = e0 + e1 + e2
    od_ref[...] = (e0 / den) * outs[0] + (e1 / den) * outs[1] + (e2 / den) * outs[2]


def _dil_decode(zd, caches):
    Bd, T, _ = zd.shape
    R = T * HEADS
    z6 = zd.reshape(Bd, T, N_DIL, 3, HEADS, HD)
    qkv = [jnp.transpose(z6[:, :, :, c], (0, 2, 1, 3, 4)).reshape(Bd, N_DIL, R, HD) for c in range(3)]
    new = pl.BlockSpec((None, N_DIL, R, HD), lambda b: (b, 0, 0, 0))
    args, specs = list(qkv), [new, new, new]
    for (wd, r), (kc, vc) in zip(DIL_GROUPS, caches):
        assert kc.shape[1] == wd and (r == 1 or (T <= r and R % SUBLANES == 0))
        for a in (kc, vc):
            if r <= T:
                args.append(a.reshape(Bd, wd * HEADS, HD))
                specs.append(pl.BlockSpec((None, wd * HEADS, HD), lambda b: (b, 0, 0)))
            else:
                args.append(a.reshape(Bd, wd // r, r * HEADS, HD))
                specs.append(pl.BlockSpec((None, wd // r, R, HD), lambda b: (b, 0, 0, 0)))
    od = pl.pallas_call(
        functools.partial(_dil_dec_body, T),
        grid=(Bd,),
        in_specs=specs,
        out_specs=pl.BlockSpec((None, R, HD), lambda b: (b, 0, 0)),
        out_shape=jax.ShapeDtypeStruct((Bd, R, HD), F32),
        compiler_params=_cparams("parallel"),
        name="dil_decode",
    )(*args)
    return od.reshape(Bd * T, ATT_COLS)


def _mem_body(q_ref, k_ref, v_ref, o_ref):
    scale = HD ** -0.5
    for h in range(HEADS):
        sl = slice(h * HD, (h + 1) * HD)
        s = _dot_nt(q_ref[:, sl].astype(BF16), k_ref[:, sl].astype(BF16)) * scale
        e = jnp.exp(s - jnp.max(s, axis=-1, keepdims=True))
        p = e / jnp.sum(e, axis=-1, keepdims=True)
        o_ref[:, sl] = _dot(p.astype(BF16), v_ref[:, sl].astype(BF16))


def _mem_attn(qm, mkv, tq):
    B, S, _ = qm.shape
    M = mkv.shape[1]
    return pl.pallas_call(
        _mem_body,
        grid=(B, S // tq),
        in_specs=[
            pl.BlockSpec((None, tq, ATT_COLS), lambda b, i: (b, i, 0)),
            pl.BlockSpec((None, M, ATT_COLS), lambda b, i: (b, 0, 0)),
            pl.BlockSpec((None, M, ATT_COLS), lambda b, i: (b, 0, 1)),
        ],
        out_specs=pl.BlockSpec((None, tq, ATT_COLS), lambda b, i: (b, i, 0)),
        out_shape=jax.ShapeDtypeStruct((B, S, ATT_COLS), F32),
        compiler_params=_cparams("parallel", "arbitrary"),
        name="mem_attn",
    )(qm, mkv, mkv)


def _mem_dec_body(q_ref, k_ref, v_ref, o_ref):
    R, n = q_ref.shape[0], k_ref.shape[0]
    rb = lax.broadcasted_iota(jnp.int32, (R, n), 0)
    cb = lax.broadcasted_iota(jnp.int32, (R, n), 1)
    s = _dot_nt(q_ref[...].astype(BF16), k_ref[...].astype(BF16)) * (HD ** -0.5)
    s = jnp.where(cb % HEADS == rb % HEADS, s, -jnp.inf)
    e = jnp.exp(s - jnp.max(s, axis=-1, keepdims=True))
    p = e / jnp.sum(e, axis=-1, keepdims=True)
    o_ref[...] = _dot(p.astype(BF16), v_ref[...].astype(BF16))


def _mem_attn_decode(qm, mk, mv):
    Bd, T, _ = qm.shape
    M = mk.shape[1]
    R = T * HEADS
    kv = pl.BlockSpec((None, M * HEADS, HD), lambda b: (b, 0, 0))
    o = pl.pallas_call(
        _mem_dec_body,
        grid=(Bd,),
        in_specs=[pl.BlockSpec((None, R, HD), lambda b: (b, 0, 0)), kv, kv],
        out_specs=pl.BlockSpec((None, R, HD), lambda b: (b, 0, 0)),
        out_shape=jax.ShapeDtypeStruct((Bd, R, HD), F32),
        compiler_params=_cparams("parallel"),
        name="mem_attn_decode",
    )(qm.reshape(Bd, R, HD), mk.reshape(Bd, M * HEADS, HD), mv.reshape(Bd, M * HEADS, HD))
    return o.reshape(Bd * T, ATT_COLS)


def _merge_body(x_ref, og_ref, r_ref, od_ref, om_ref, gt_ref, ong_ref, wbg_ref, wbd_ref, wbm_ref, wo_ref,
                ln2_ref, wq_ref, h_ref, n2_ref, qh_ref):
    og = og_ref[...]
    r = r_ref[...]
    ong = ong_ref[...]
    parts = []
    for h in range(GLA_HEADS):
        sl = slice(h * GLA_DV, (h + 1) * GLA_DV)
        parts.append(_rms(og[:, sl], ong) * jax.nn.silu(r[:, sl]))
    br_gla = _dot(jnp.concatenate(parts, axis=-1).astype(BF16), wbg_ref[...])
    br_dil = _dot(od_ref[...].astype(BF16), wbd_ref[...])
    br_mem = _dot(om_ref[...].astype(BF16), wbm_ref[...])
    d = D_MODEL
    merged = gt_ref[:, 0:d] * br_gla + gt_ref[:, d:2 * d] * br_dil + gt_ref[:, 2 * d:3 * d] * br_mem
    hres = x_ref[...] + _dot(merged.astype(BF16), wo_ref[...])
    h_ref[...] = hres
    n2 = _rms(hres, ln2_ref[...]).astype(BF16)
    n2_ref[...] = n2
    qh_ref[...] = _dot(n2, wq_ref[...])


def _merge(x, og, zg, od, om, gt, ong, wbg, wbd, wbm, wo, ln2, wq, tm=256):
    nt, d = x.shape
    tm = min(tm, nt)
    nq = wq.shape[1]
    rblk = (2 * GLA_HEADS * GLA_DK + GLA_HEADS * GLA_DV) // d

    def tok(cols, blk=0):
        return pl.BlockSpec((tm, cols), lambda i: (i, blk))

    def const(shape):
        return pl.BlockSpec(shape, lambda i: (0, 0), pipeline_mode=pl.Buffered(1))

    return pl.pallas_call(
        _merge_body,
        grid=(nt // tm,),
        in_specs=[
            tok(d), tok(d), tok(d, rblk), tok(ATT_COLS), tok(ATT_COLS), tok(3 * d),
            const((1, GLA_DV)), const(wbg.shape), const(wbd.shape), const(wbm.shape), const(wo.shape),
            const((1, d)), const(wq.shape),
        ],
        out_specs=[tok(d), tok(d), tok(nq)],
        out_shape=[
            jax.ShapeDtypeStruct((nt, d), F32),
            jax.ShapeDtypeStruct((nt, d), BF16),
            jax.ShapeDtypeStruct((nt, nq), F32),
        ],
        compiler_params=_cparams("parallel"),
        name="merge",
    )(x, og, zg, od, om, gt, ong.reshape(1, GLA_DV), wbg, wbd, wbm, wo, ln2.reshape(1, d), wq)


PEER_CAND_GROUPS = 10


def _peer_cand_layout(tp):
    r = lax.broadcasted_iota(jnp.int32, (PEER_CAND_GROUPS * SUBLANES, tp), 0)
    grp = r // SUBLANES
    p = r % SUBLANES
    K = PEER_TOPK
    k1 = jnp.where(grp < 2, 0, jnp.where(grp < 9, grp - 1, SUBLANES + p))
    k2 = jnp.where(grp < 2, r, jnp.where(grp < 9, p, 0))
    valid = (k1 + 1) * (k2 + 1) <= K
    return (k1 * K + k2).astype(F32), valid


def _peer_cands(x1, x2):
    top, bot = x2[0:SUBLANES], x2[SUBLANES:2 * SUBLANES]
    g1 = [jnp.broadcast_to(x1[0:1], top.shape)] * 2 + [jnp.broadcast_to(x1[k:k + 1], top.shape) for k in range(1, 8)]
    g1.append(x1[SUBLANES:2 * SUBLANES])
    g2 = [top, bot] + [top] * 7 + [jnp.broadcast_to(x2[0:1], top.shape)]
    return jnp.concatenate(g1, axis=0), jnp.concatenate(g2, axis=0)


def _peer_topk_body(qh_ref, keys_ref, a_ref, b_ref, gw_ref, s_scr, sv_scr, si_scr, cand_scr, eid_scr, sc_scr, ev_scr):
    tp = qh_ref.shape[0]
    K = PEER_TOPK
    NK = PEER_NKEYS
    NH = PEER_HEADS
    rowf = lax.broadcasted_iota(jnp.int32, (NK, tp), 0).astype(F32)
    pos, valid = _peer_cand_layout(tp)
    for i in range(2 * NH):
        s_scr[i] = _dot_nt(keys_ref[i], qh_ref[:, i * LANES:(i + 1) * LANES].astype(BF16))

    def step1(k, carry):
        for i in range(2 * NH):
            s = s_scr[i]
            mx = jnp.max(s, axis=0, keepdims=True)
            idx = jnp.min(jnp.where(s == mx, rowf, float(NK)), axis=0, keepdims=True)
            s_scr[i] = jnp.where(rowf == idx, -jnp.inf, s)
            sv_scr[i, pl.ds(k, 1), :] = mx
            si_scr[i, pl.ds(k, 1), :] = idx
        return carry

    lax.fori_loop(0, K, step1, 0)

    for h in range(NH):
        c1, c2 = _peer_cands(sv_scr[2 * h], sv_scr[2 * h + 1])
        e1, e2 = _peer_cands(si_scr[2 * h], si_scr[2 * h + 1])
        cand_scr[h] = jnp.where(valid, c1 + c2, -jnp.inf)
        eid_scr[h] = e1 * float(NK) + e2

    def step2(k, carry):
        for h in range(NH):
            cand = cand_scr[h]
            mx = jnp.max(cand, axis=0, keepdims=True)
            first = jnp.min(jnp.where(cand == mx, pos, float(K * K)), axis=0, keepdims=True)
            hit = pos == first
            ev_scr[h, pl.ds(k, 1), :] = jnp.max(jnp.where(hit, eid_scr[h], -1.0), axis=0, keepdims=True)
            sc_scr[h, pl.ds(k, 1), :] = mx
            cand_scr[h] = jnp.where(hit, -jnp.inf, cand)
        return carry

    lax.fori_loop(0, K, step2, 0)

    a_rows, b_rows, g_rows = [], [], []
    for h in range(NH):
        sc, ev = sc_scr[h], ev_scr[h]
        e = jnp.exp(sc - jnp.max(sc, axis=0, keepdims=True))
        g_rows.append(e / jnp.sum(e, axis=0, keepdims=True))
        a = jnp.floor(ev * (1.0 / NK))
        a_rows.append(a)
        b_rows.append(ev - a * float(NK))
    a_ref[...] = jnp.transpose(jnp.concatenate(a_rows, axis=0))
    b_ref[...] = jnp.transpose(jnp.concatenate(b_rows, axis=0))
    gw_ref[...] = jnp.transpose(jnp.concatenate(g_rows, axis=0))


def _peer_topk(qh, keys, tp=128):
    nt, nq = qh.shape
    tp = min(tp, nt)
    spec = pl.BlockSpec((tp, LANES), lambda i: (i, 0))
    shp = jax.ShapeDtypeStruct((nt, LANES), F32)
    K, NH = PEER_TOPK, PEER_HEADS
    ncand = PEER_CAND_GROUPS * SUBLANES
    return pl.pallas_call(
        _peer_topk_body,
        grid=(nt // tp,),
        in_specs=[
            pl.BlockSpec((tp, nq), lambda i: (i, 0)),
            pl.BlockSpec(keys.shape, lambda i: (0, 0, 0)),
        ],
        out_specs=[spec, spec, spec],
        out_shape=[shp, shp, shp],
        scratch_shapes=[
            pltpu.VMEM((2 * NH, PEER_NKEYS, tp), F32), pltpu.VMEM((2 * NH, K, tp), F32), pltpu.VMEM((2 * NH, K, tp), F32),
            pltpu.VMEM((NH, ncand, tp), F32), pltpu.VMEM((NH, ncand, tp), F32),
            pltpu.VMEM((NH, K, tp), F32), pltpu.VMEM((NH, K, tp), F32),
        ],
        compiler_params=_cparams("parallel"),
        name="peer_topk",
    )(qh, keys)


def _peer_w_body(a_ref, b_ref, gw_ref, w_ref, scr):
    tw = a_ref.shape[0]
    NK = PEER_NKEYS
    stride = scr.shape[0] // NK
    sub = lax.broadcasted_iota(jnp.int32, (NK, LANES), 0).astype(F32)

    zero = jnp.zeros((NK, LANES), BF16)

    def tok2(p, carry):
        oas, gbs = [], []
        for t in (2 * p, 2 * p + 1):
            arow = a_ref[pl.ds(t, 1), :]
            brow = b_ref[pl.ds(t, 1), :]
            grow = gw_ref[pl.ds(t, 1), :]
            oas.append((arow == sub).astype(BF16))
            gbs.append(jnp.where(brow == sub, grow, 0.0).astype(BF16))
        lhs = jnp.concatenate(oas, axis=1)
        rhs = jnp.concatenate([jnp.concatenate([gbs[0], zero], axis=1),
                               jnp.concatenate([zero, gbs[1]], axis=1)], axis=0)
        w2 = _dot_nt(lhs, rhs)
        scr[pl.ds(2 * p, NK, stride=stride), :] = w2[:, 0:LANES]
        scr[pl.ds(2 * p + 1, NK, stride=stride), :] = w2[:, LANES:2 * LANES]
        return carry

    lax.fori_loop(0, tw // 2, tok2, 0, unroll=16)
    for i1 in range(NK):
        w_ref[:, i1 * NK:(i1 + 1) * NK] = scr[i1 * stride:i1 * stride + tw, :].astype(BF16)


def _peer_weights(a, b, gw, n_experts, tw=128):
    nt = a.shape[0]
    tw = min(tw, nt)
    stride = tw + SUBLANES
    tok = pl.BlockSpec((tw, LANES), lambda i: (i, 0))
    return pl.pallas_call(
        _peer_w_body,
        grid=(nt // tw,),
        in_specs=[tok, tok, tok],
        out_specs=pl.BlockSpec((tw, n_experts), lambda i: (i, 0)),
        out_shape=jax.ShapeDtypeStruct((nt, n_experts), BF16),
        scratch_shapes=[pltpu.VMEM((PEER_NKEYS * stride, LANES), F32)],
        compiler_params=_cparams("parallel"),
        name="peer_weights",
    )(a, b, gw)


def _peer_mix_body(n2_ref, w_ref, h_ref, u_ref, v_ref, y_ref, acc):
    j = pl.program_id(1)

    @pl.when(j == 0)
    def _():
        acc[...] = jnp.zeros(acc.shape, F32)

    hpre = _dot_nt(n2_ref[...], u_ref[...])
    gelu = 0.5 * hpre * (1.0 + lax.erf(hpre * (2.0 ** -0.5)))
    act = (gelu * w_ref[...].astype(F32)).astype(BF16)
    acc[...] += _dot(act, v_ref[...])

    @pl.when(j == pl.num_programs(1) - 1)
    def _():
        y_ref[...] = h_ref[...] + acc[...]


def _peer_mix(n2, w, h, u, v):
    nt, d = n2.shape
    T = min(PEER_T, nt)
    assert nt % T == 0 and u.shape[0] % PEER_EB == 0
    tok = pl.BlockSpec((T, d), lambda i, j: (i, 0))
    tab = pl.BlockSpec((PEER_EB, d), lambda i, j: (j, 0))
    return pl.pallas_call(
        _peer_mix_body,
        grid=(nt // T, u.shape[0] // PEER_EB),
        in_specs=[tok, pl.BlockSpec((T, PEER_EB), lambda i, j: (i, j)), tok, tab, tab],
        out_specs=tok,
        out_shape=jax.ShapeDtypeStruct((nt, d), F32),
        scratch_shapes=[pltpu.VMEM((T, d), F32)],
        compiler_params=_cparams("parallel", "arbitrary"),
        name="peer_mix",
    )(n2, w, h, u, v)


def _rope_tables(pos):
    half = HD // 2
    inv = ROPE_THETA ** (-jnp.arange(half, dtype=F32) / half)
    ang = pos.astype(F32)[:, None] * inv[None, :]
    cos, sin = jnp.cos(ang), jnp.sin(ang)
    return jnp.concatenate([cos, cos], axis=-1), jnp.concatenate([-sin, sin], axis=-1)


def _prep_weights(w_in, gla_wg2, peer_wq, peer_keys, peer_u, peer_v, w_br_gla, w_br_dil, w_br_mem, w_out):
    c0 = GLA_COLS
    c1 = c0 + GLA_RANK
    c2 = c1 + DIL_COLS
    c3 = c2 + ATT_COLS
    wa = jnp.pad(w_in[:, c0:c1], ((0, 0), (0, LANES - GLA_RANK))).astype(BF16)
    wg2 = jnp.pad(gla_wg2, ((0, LANES - GLA_RANK), (0, 0))).astype(BF16)
    return dict(
        w_gla=w_in[:, :c0].astype(BF16), wa=wa, wg2=wg2,
        w_dil=w_in[:, c1:c2].astype(BF16), w_qm=w_in[:, c2:c3].astype(BF16), w_gt=w_in[:, c3:].astype(BF16),
        wbg=w_br_gla.astype(BF16), wbd=w_br_dil.astype(BF16), wbm=w_br_mem.astype(BF16), wo=w_out.astype(BF16),
        wq=peer_wq.astype(BF16),
        keys=peer_keys.reshape(PEER_HEADS * 2, PEER_NKEYS, LANES).astype(BF16),
        u=peer_u.astype(BF16), v=peer_v.astype(BF16),
    )


def _layer(x, pos, mem_k, mem_v, s0, caches, p, W):
    B, S, d = x.shape
    nt = B * S
    xt = x.reshape(nt, d)
    ln1 = p["ln1_g"]
    zg = _proj("gla", xt, ln1, W["w_gla"])
    lg = _gla_gate(xt, ln1, W["wa"], W["wg2"], p["gla_bg"])
    cosf, sinf = _rope_tables(pos)
    cosf = jnp.broadcast_to(cosf[None], (B, S, HD)).reshape(nt, HD)
    sinf = jnp.broadcast_to(sinf[None], (B, S, HD)).reshape(nt, HD)
    hg = jnp.stack([p["dil_qn_g"], p["dil_kn_g"]])
    qm = _proj("mem_q", xt, ln1, W["w_qm"], extra=(p["mem_qn_g"].reshape(1, HD),))
    gt = _proj("sigmoid", xt, ln1, W["w_gt"])

    prompt = caches is None
    og, s_fin = _gla(zg.reshape(B, S, GLA_COLS), lg.reshape(B, S, GLA_HEADS * GLA_DK), s0,
                     GLA_CHUNK if prompt else S)

    new_bufs = []
    gcols = 3 * ATT_COLS
    if prompt:
        parts = []
        for gi, (wd, r) in enumerate(DIL_GROUPS):
            zr = _proj_dil_group(xt, ln1, W["w_dil"][:, gi * gcols:(gi + 1) * gcols], hg, cosf, sinf, B, S, r)
            parts.extend(_dil_prompt(zr, S))
            keep = min(wd, S)
            kv = zr[:, :, (S - keep) // r:, ATT_COLS:]
            kv = jnp.swapaxes(kv, 1, 2).reshape(B, keep, 2, HEADS, HD)
            new_bufs.append((kv[:, :, 0], kv[:, :, 1]))
        od = _dil_mix(parts)
    else:
        zd3 = _proj("dil", xt, ln1, W["w_dil"], extra=(hg, cosf, sinf)).reshape(B, S, DIL_COLS)
        od = _dil_decode(zd3, caches).reshape(nt, ATT_COLS)
        for gi, (wd, r) in enumerate(DIL_GROUPS):
            kb, vb = caches[gi]
            kn = zd3[..., gi * gcols + ATT_COLS:gi * gcols + 2 * ATT_COLS].reshape(B, S, HEADS, HD)
            vn = zd3[..., gi * gcols + 2 * ATT_COLS:(gi + 1) * gcols].reshape(B, S, HEADS, HD)
            keep = min(wd, kb.shape[1] + S)
            new_bufs.append((jnp.concatenate([kb, kn], axis=1)[:, -keep:], jnp.concatenate([vb, vn], axis=1)[:, -keep:]))

    qm3 = qm.reshape(B, S, ATT_COLS)
    if prompt:
        om = _mem_attn(qm3, mem_k, 512).reshape(nt, ATT_COLS)
    else:
        om = _mem_attn_decode(qm3, mem_k, mem_v)

    h, n2, qh = _merge(xt, og.reshape(nt, d), zg, od, om, gt, p["gla_onorm_g"], W["wbg"], W["wbd"], W["wbm"],
                       W["wo"], p["ln2_g"], W["wq"])
    a, b, gw = _peer_topk(qh, W["keys"])
    y = _peer_mix(n2, _peer_weights(a, b, gw, W["u"].shape[0]), h, W["u"], W["v"])
    return y.reshape(B, S, d), s_fin, new_bufs


def kernel(x_prompt, x_sample, mem_prompt, state_gla, cache_dil_k0, cache_dil_v0, cache_dil_k1, cache_dil_v1, cache_dil_k2, cache_dil_v2, cache_mem_k, cache_mem_v, ln1_g, w_in, gla_wg2, gla_bg, gla_onorm_g, dil_qn_g, dil_kn_g, mem_norm_g, w_mem_kv, mem_qn_g, mem_kn_g, w_br_gla, w_br_dil, w_br_mem, w_out, ln2_g, peer_wq, peer_keys, peer_u, peer_v):
    B, S, d = x_prompt.shape
    Bd, T, _ = x_sample.shape
    p = dict(ln1_g=ln1_g, gla_bg=gla_bg, gla_onorm_g=gla_onorm_g, dil_qn_g=dil_qn_g, dil_kn_g=dil_kn_g,
             mem_qn_g=mem_qn_g, ln2_g=ln2_g)
    W = _prep_weights(w_in, gla_wg2, peer_wq, peer_keys, peer_u, peer_v, w_br_gla, w_br_dil, w_br_mem, w_out)

    M = mem_prompt.shape[1]
    mkv = _proj("mem_kv", mem_prompt.reshape(B * M, d), mem_norm_g, w_mem_kv.astype(BF16),
                extra=(mem_kn_g.reshape(1, HD),))
    mem_k_p = mkv[:, :ATT_COLS].reshape(B, M, HEADS, HD)
    mem_v_p = mkv[:, ATT_COLS:].reshape(B, M, HEADS, HD)

    s0 = jnp.zeros((B, GLA_HEADS, GLA_DK, GLA_DV), F32)
    mkv3 = mkv.reshape(B, M, 2 * ATT_COLS)
    y_prompt, gla_state_p, bufs_p = _layer(x_prompt, jnp.arange(S, dtype=jnp.int32), mkv3, mkv3, s0, None, p, W)

    caches = ((cache_dil_k0, cache_dil_v0), (cache_dil_k1, cache_dil_v1), (cache_dil_k2, cache_dil_v2))
    pos_s = PAST_LEN + jnp.arange(T, dtype=jnp.int32)
    y_sample, gla_state_s, bufs_s = _layer(x_sample, pos_s, cache_mem_k, cache_mem_v, state_gla, caches, p, W)

    (dk0_p, dv0_p), (dk1_p, dv1_p), (dk2_p, dv2_p) = bufs_p
    (dk0_s, dv0_s), (dk1_s, dv1_s), (dk2_s, dv2_s) = bufs_s
    return (y_prompt, y_sample,
            gla_state_p, dk0_p, dv0_p, dk1_p, dv1_p, dk2_p, dv2_p, mem_k_p, mem_v_p,
            gla_state_s, dk0_s, dv0_s, dk1_s, dv1_s, dk2_s, dv2_s)
```

```python
import functools

import jax
import jax.numpy as jnp
from jax import lax
from jax.experimental import pallas as pl
from jax.experimental.pallas import tpu as pltpu

F32 = jnp.float32
BF16 = jnp.bfloat16

D_MODEL = 1024
PAST_LEN = 8192
GLA_HEADS = 4
GLA_DK = 128
GLA_DV = 256
GLA_RANK = 16
GLA_TAU = 16.0
DIL_GROUPS = ((128, 1), (512, 4), (2048, 16))
N_DIL = 3
HEADS = 4
HD = 128
SPAN = 128
MEM_LEN = 256
ROPE_THETA = 10000.0
PEER_HEADS = 8
PEER_NKEYS = 128
PEER_TOPK = 16
EPS = 1e-6
NEG = -1e30

LANES = 128
SUBLANES = 8
GLA_COLS = 2 * GLA_HEADS * GLA_DK + 2 * GLA_HEADS * GLA_DV
DIL_COLS = N_DIL * 3 * HEADS * HD
ATT_COLS = HEADS * HD
VMEM_LIMIT = 56 * 1024 * 1024

PEER_T = 1024
PEER_EB = 1024


def _cparams(*sem):
    return pltpu.CompilerParams(dimension_semantics=sem, vmem_limit_bytes=VMEM_LIMIT)


def _rms(x, g):
    return x * lax.rsqrt(jnp.mean(x * x, axis=-1, keepdims=True) + EPS) * g


def _dot(a, b):
    return jnp.dot(a, b, preferred_element_type=F32)


def _dot_nt(a, b):
    return lax.dot_general(a, b, (((1,), (1,)), ((), ())), preferred_element_type=F32)


def _split3(x):
    hi = x.astype(BF16)
    r1 = x - hi.astype(F32)
    mid = r1.astype(BF16)
    lo = (r1 - mid.astype(F32)).astype(BF16)
    return hi, mid, lo


def _proj_body(kind, *refs):
    if kind == "dil":
        x_ref, g_ref, w_ref, hg_ref, cos_ref, sin_ref, o_ref, n_scr = refs
    elif kind in ("mem_q", "mem_kv"):
        x_ref, g_ref, w_ref, hg_ref, o_ref, n_scr = refs
    else:
        x_ref, g_ref, w_ref, o_ref, n_scr = refs
    j = pl.program_id(1)

    @pl.when(j == 0)
    def _():
        n_scr[...] = _rms(x_ref[...], g_ref[...]).astype(BF16)

    z = _dot(n_scr[...], w_ref[...])
    if kind == "gla":
        col = lax.broadcasted_iota(jnp.int32, (1, z.shape[1]), 1) + j * z.shape[1]
        o_ref[...] = z * jnp.where(col < GLA_HEADS * GLA_DK, GLA_DK ** -0.5, 1.0).astype(F32)
    elif kind == "sigmoid":
        o_ref[...] = jax.nn.sigmoid(z)
    elif kind == "mem_q":
        hg = hg_ref[...]
        o_ref[...] = jnp.concatenate([_rms(z[:, h * HD:(h + 1) * HD], hg) for h in range(HEADS)], axis=-1)
    elif kind == "mem_kv":
        @pl.when(j == 0)
        def _():
            hg = hg_ref[...]
            o_ref[...] = jnp.concatenate([_rms(z[:, h * HD:(h + 1) * HD], hg) for h in range(HEADS)], axis=-1)

        @pl.when(j != 0)
        def _():
            o_ref[...] = z
    elif kind == "dil":
        c = j % 3

        @pl.when(c == 2)
        def _():
            o_ref[...] = z

        @pl.when(c != 2)
        def _():
            hg = jnp.where(c == 0, hg_ref[0:1, :], hg_ref[1:2, :])
            cs = cos_ref[...]
            sn = sin_ref[...]
            parts = []
            for h in range(HEADS):
                zh = _rms(z[:, h * HD:(h + 1) * HD], hg)
                parts.append(zh * cs + pltpu.roll(zh, HD // 2, 1) * sn)
            o_ref[...] = jnp.concatenate(parts, axis=-1)
    else:
        raise ValueError(kind)


def _proj(kind, x, g, w, extra=(), tm=1024, tn=1024):
    nt, d = x.shape
    ncol = w.shape[1]
    tm = min(tm, nt)
    if kind in ("dil", "mem_q", "mem_kv"):
        tn = ATT_COLS
    assert nt % tm == 0 and ncol % tn == 0
    in_specs = [
        pl.BlockSpec((tm, d), lambda i, j: (i, 0)),
        pl.BlockSpec((1, d), lambda i, j: (0, 0)),
        pl.BlockSpec((d, tn), lambda i, j: (0, j)),
    ]
    if kind == "dil":
        in_specs += [
            pl.BlockSpec((2, HD), lambda i, j: (0, 0)),
            pl.BlockSpec((tm, HD), lambda i, j: (i, 0)),
            pl.BlockSpec((tm, HD), lambda i, j: (i, 0)),
        ]
    elif kind in ("mem_q", "mem_kv"):
        in_specs += [pl.BlockSpec((1, HD), lambda i, j: (0, 0))]
    return pl.pallas_call(
        functools.partial(_proj_body, kind),
        grid=(nt // tm, ncol // tn),
        in_specs=in_specs,
        out_specs=pl.BlockSpec((tm, tn), lambda i, j: (i, j)),
        out_shape=jax.ShapeDtypeStruct((nt, ncol), F32),
        scratch_shapes=[pltpu.VMEM((tm, d), BF16)],
        compiler_params=_cparams("parallel", "arbitrary"),
        name="proj_" + kind,
    )(x, g.reshape(1, d), w, *extra)


def _gate_body(x_ref, g_ref, wa_ref, wg2_ref, bg_ref, o_ref):
    n = _rms(x_ref[...], g_ref[...]).astype(BF16)
    a = _dot(n, wa_ref[...])
    pre = _dot(a.astype(BF16), wg2_ref[...]) + bg_ref[...]
    o_ref[...] = jax.nn.log_sigmoid(pre) / GLA_TAU


def _gla_gate(x, g, wa, wg2, bg, tm=512):
    nt, d = x.shape
    tm = min(tm, nt)
    ncol = wg2.shape[1]
    return pl.pallas_call(
        _gate_body,
        grid=(nt // tm,),
        in_specs=[
            pl.BlockSpec((tm, d), lambda i: (i, 0)),
            pl.BlockSpec((1, d), lambda i: (0, 0)),
            pl.BlockSpec((d, LANES), lambda i: (0, 0)),
            pl.BlockSpec((LANES, ncol), lambda i: (0, 0)),
            pl.BlockSpec((1, ncol), lambda i: (0, 0)),
        ],
        out_specs=pl.BlockSpec((tm, ncol), lambda i: (i, 0)),
        out_shape=jax.ShapeDtypeStruct((nt, ncol), F32),
        compiler_params=_cparams("parallel"),
        name="gla_gate",
    )(x, g.reshape(1, d), wa, wg2, bg.reshape(1, ncol))


GLA_CHUNK = 128
GLA_SUB = 16


def _gla_intra(q, k, b, qs_scr, bs_scr, h, n_sub):
    P, SB = GLA_CHUNK, GLA_SUB
    rowp = lax.broadcasted_iota(jnp.int32, (P, 1), 0)
    qparts, kparts = [], []
    for j in range(n_sub - 1):
        lo, hi = SB * j, SB * (j + 1)
        be = b[hi - 1:hi, :]
        qparts.append(jnp.where(rowp >= hi, q * jnp.exp(b - be), 0.0).astype(BF16))
        kblk = (k[lo:hi] * jnp.exp(be - b[lo:hi])).astype(BF16)
        pieces = [kblk]
        if lo:
            pieces.insert(0, jnp.zeros((lo, k.shape[1]), BF16))
        pieces.append(jnp.zeros((P - hi, k.shape[1]), BF16))
        kparts.append(jnp.concatenate(pieces, axis=0))
    if qparts:
        a_off = _dot_nt(jnp.concatenate(qparts, axis=1), jnp.concatenate(kparts, axis=1))
    else:
        a_off = jnp.zeros((P, P), F32)
    lane = lax.broadcasted_iota(jnp.int32, (SB, P), 1)
    srow = lax.broadcasted_iota(jnp.int32, (SB, P), 0)
    tiles = []
    for j in range(P // SB):
        if j >= n_sub:
            tiles.append(jnp.zeros((SB, P), F32))
            continue
        lo, hi = SB * j, SB * (j + 1)
        kj, bj = k[lo:hi], b[lo:hi]
        at = jnp.zeros((SB, P), F32)
        for t in range(lo, hi):
            bt = bs_scr[h, pl.ds(t, 1), :]
            qt = qs_scr[h, pl.ds(t, 1), :]
            colv = jnp.sum(qt * kj * jnp.exp(jnp.minimum(bt - bj, 0.0)), axis=-1, keepdims=True)
            at = jnp.where(lane == t, colv, at)
        tiles.append(jnp.where(srow + lo <= lane, at, 0.0))
    return a_off + jnp.transpose(jnp.concatenate(tiles, axis=0))


def _gla_body(tc, q_ref, k_ref, v_ref, g_ref, s0_ref, o_ref, sf_ref, s_scr, qp, kp, vp, gp, qs_scr, bs_scr):
    c = pl.program_id(1)
    P = GLA_CHUNK
    H, dk, dv = GLA_HEADS, GLA_DK, GLA_DV
    n_sub = -(-tc // GLA_SUB)

    @pl.when(c == 0)
    def _():
        s_scr[...] = s0_ref[...]

    if tc == P:
        q, k, v, g = q_ref[...], k_ref[...], v_ref[...], g_ref[...]
    else:
        for pad, ref in ((qp, q_ref), (kp, k_ref), (vp, v_ref), (gp, g_ref)):
            pad[...] = jnp.zeros(pad.shape, F32)
            pad[0:tc, :] = ref[...]
        q, k, v, g = qp[...], kp[...], vp[...], gp[...]

    row = lax.broadcasted_iota(jnp.int32, (P, P), 0)
    col = lax.broadcasted_iota(jnp.int32, (P, P), 1)
    trib = (row >= col).astype(BF16)
    g1, g2, g3 = _split3(g)
    b = _dot(trib, g1) + _dot(trib, g2) + _dot(trib, g3)
    b_end = b[P - 1:P, :]
    qe = (q * jnp.exp(b)).astype(BF16)
    kd = k * jnp.exp(b_end - b)
    vb = v.astype(BF16)
    for h in range(H):
        ks = slice(h * dk, (h + 1) * dk)
        qs_scr[h] = q[:, ks]
        bs_scr[h] = b[:, ks]
    for h in range(H):
        ks = slice(h * dk, (h + 1) * dk)
        vs = slice(h * dv, (h + 1) * dv)
        a = _gla_intra(q[:, ks], k[:, ks], b[:, ks], qs_scr, bs_scr, h, n_sub)
        s = s_scr[h]
        o = _dot(qe[:, ks], s.astype(BF16)) + _dot(a.astype(BF16), vb[:, vs])
        o_ref[:, vs] = o[0:tc, :]
        decay = jnp.exp(jnp.sum(jnp.transpose(g[:, ks]), axis=1, keepdims=True))
        s_new = decay * s + _dot(jnp.transpose(kd[:, ks]).astype(BF16), vb[:, vs])
        s_scr[h] = s_new

        @pl.when(c == pl.num_programs(1) - 1)
        def _():
            sf_ref[h] = s_new


def _gla(zg, lg, s0, tc):
    B, S, _ = zg.shape
    H, dk, dv = GLA_HEADS, GLA_DK, GLA_DV
    assert S % tc == 0
    P = GLA_CHUNK
    qk = H * dk
    st = pl.BlockSpec((None, H, dk, dv), lambda b, c: (b, 0, 0, 0))
    return pl.pallas_call(
        functools.partial(_gla_body, tc),
        grid=(B, S // tc),
        in_specs=[
            pl.BlockSpec((None, tc, qk), lambda b, c: (b, c, 0)),
            pl.BlockSpec((None, tc, qk), lambda b, c: (b, c, 1)),
            pl.BlockSpec((None, tc, H * dv), lambda b, c: (b, c, 2 * qk // (H * dv))),
            pl.BlockSpec((None, tc, qk), lambda b, c: (b, c, 0)),
            st,
        ],
        out_specs=[pl.BlockSpec((None, tc, H * dv), lambda b, c: (b, c, 0)), st],
        out_shape=[
            jax.ShapeDtypeStruct((B, S, H * dv), F32),
            jax.ShapeDtypeStruct((B, H, dk, dv), F32),
        ],
        scratch_shapes=[
            pltpu.VMEM((H, dk, dv), F32),
            pltpu.VMEM((P, qk), F32), pltpu.VMEM((P, qk), F32), pltpu.VMEM((P, H * dv), F32), pltpu.VMEM((P, qk), F32),
            pltpu.VMEM((H, P, dk), F32), pltpu.VMEM((H, P, dk), F32),
        ],
        compiler_params=_cparams("parallel", "arbitrary"),
        name="gla",
    )(zg, zg, zg, lg, s0)


def _proj_dil_body(r, x_ref, g_ref, w_ref, hg_ref, cos_ref, sin_ref, o_ref, n_scr, z_scr):
    j = pl.program_id(1)

    @pl.when(j == 0)
    def _():
        n_scr[...] = _rms(x_ref[...], g_ref[...]).astype(BF16)

    z = _dot(n_scr[...], w_ref[...])

    @pl.when(j == 2)
    def _():
        for h in range(HEADS):
            z_scr[h] = z[:, h * HD:(h + 1) * HD]

    @pl.when(j != 2)
    def _():
        hg = jnp.where(j == 0, hg_ref[0:1, :], hg_ref[1:2, :])
        cs = cos_ref[...]
        sn = sin_ref[...]
        zs = [z[:, h * HD:(h + 1) * HD] for h in range(HEADS)]
        ms = [jnp.mean(zh * zh, axis=-1, keepdims=True) for zh in zs]
        ns = [zh * lax.rsqrt(m + EPS) * hg for zh, m in zip(zs, ms)]
        rs = [pltpu.roll(nh, HD // 2, 1) for nh in ns]
        for h in range(HEADS):
            z_scr[h] = ns[h] * cs + rs[h] * sn

    rows = z_scr.shape[1] // r
    for rho in range(r):
        for h in range(HEADS):
            zh = z_scr[h, pl.ds(rho, rows, stride=r), :] if r > 1 else z_scr[h]
            o_ref[rho, :, h * HD:(h + 1) * HD] = zh


def _proj_dil_group(x, g, w, hg, cosf, sinf, B, S, r, tm=1024):
    nt, d = x.shape
    assert S % tm == 0 and tm % (r * SUBLANES) == 0
    spb = S // tm
    return pl.pallas_call(
        functools.partial(_proj_dil_body, r),
        grid=(nt // tm, 3),
        in_specs=[
            pl.BlockSpec((tm, d), lambda i, j: (i, 0)),
            pl.BlockSpec((1, d), lambda i, j: (0, 0)),
            pl.BlockSpec((d, ATT_COLS), lambda i, j: (0, j)),
            pl.BlockSpec((2, HD), lambda i, j: (0, 0)),
            pl.BlockSpec((tm, HD), lambda i, j: (i, 0)),
            pl.BlockSpec((tm, HD), lambda i, j: (i, 0)),
        ],
        out_specs=pl.BlockSpec((None, r, tm // r, ATT_COLS), lambda i, j: (i // spb, 0, i % spb, j)),
        out_shape=jax.ShapeDtypeStruct((B, r, S // r, 3 * ATT_COLS), F32),
        scratch_shapes=[pltpu.VMEM((tm, d), BF16), pltpu.VMEM((HEADS, tm, HD), F32)],
        compiler_params=_cparams("parallel", "arbitrary"),
        name="proj_dil_r%d" % r,
    )(x, g.reshape(1, d), w, hg, cosf, sinf)


def _dil_body(r, q_ref, kc_ref, kp_ref, vc_ref, vp_ref, o_ref, l_ref):
    i = pl.program_id(1)
    rho = pl.program_id(2)
    scale = HD ** -0.5
    mq = lax.broadcasted_iota(jnp.int32, (SPAN, SPAN), 0)
    mk = lax.broadcasted_iota(jnp.int32, (SPAN, SPAN), 1)
    mask_c = mk <= mq
    mask_p = jnp.logical_and(mk >= mq, i > 0)
    sls = [slice(h * HD, (h + 1) * HD) for h in range(HEADS)]
    qs = [q_ref[:, sl].astype(BF16) for sl in sls]
    scs = [jnp.where(mask_c, _dot_nt(q, kc_ref[:, sl].astype(BF16)) * scale, NEG) for q, sl in zip(qs, sls)]
    sps = [jnp.where(mask_p, _dot_nt(q, kp_ref[:, sl].astype(BF16)) * scale, NEG) for q, sl in zip(qs, sls)]
    ms = [jnp.maximum(jnp.max(sc, axis=-1, keepdims=True), jnp.max(sp, axis=-1, keepdims=True))
          for sc, sp in zip(scs, sps)]
    ecs = [jnp.exp(sc - m) for sc, m in zip(scs, ms)]
    eps = [jnp.exp(sp - m) for sp, m in zip(sps, ms)]
    dens = [jnp.sum(ec, axis=-1, keepdims=True) + jnp.sum(ep, axis=-1, keepdims=True) for ec, ep in zip(ecs, eps)]
    for h, sl in enumerate(sls):
        o = (_dot((ecs[h] / dens[h]).astype(BF16), vc_ref[:, sl].astype(BF16))
             + _dot((eps[h] / dens[h]).astype(BF16), vp_ref[:, sl].astype(BF16)))
        lse = jnp.broadcast_to(ms[h] + jnp.log(dens[h]), (SPAN, HD))
        if r == 1:
            o_ref[h] = o
            l_ref[h] = lse
        else:
            o_ref[h, pl.ds(rho, SPAN, stride=r), :] = o
            l_ref[h, pl.ds(rho, SPAN, stride=r), :] = lse


def _dil_prompt(zr, S):
    B, r, L, _ = zr.shape
    assert L % SPAN == 0
    nb = L // SPAN

    def spec(c, prev):
        if prev:
            return pl.BlockSpec((None, None, SPAN, ATT_COLS), lambda b, i, p: (b, p, jnp.maximum(i - 1, 0), c))
        return pl.BlockSpec((None, None, SPAN, ATT_COLS), lambda b, i, p: (b, p, i, c))

    out_spec = pl.BlockSpec((None, HEADS, SPAN * r, HD), lambda b, i, p: (b, 0, i, 0))
    return pl.pallas_call(
        functools.partial(_dil_body, r),
        grid=(B, nb, r),
        in_specs=[spec(0, False), spec(1, False), spec(1, True), spec(2, False), spec(2, True)],
        out_specs=[out_spec, out_spec],
        out_shape=[jax.ShapeDtypeStruct((B, HEADS, S, HD), F32)] * 2,
        compiler_params=_cparams("parallel", "parallel", "arbitrary"),
        name="dil_prompt_r%d" % r,
    )(zr, zr, zr, zr, zr)


def _dil_mix_body(o0, l0, o1, l1, o2, l2, od_ref):
    for h in range(HEADS):
        a0, a1, a2 = l0[h], l1[h], l2[h]
        m = jnp.maximum(jnp.maximum(a0, a1), a2)
        e0, e1, e2 = jnp.exp(a0 - m), jnp.exp(a1 - m), jnp.exp(a2 - m)
        den = e0 + e1 + e2
        od_ref[:, h * HD:(h + 1) * HD] = (e0 / den) * o0[h] + (e1 / den) * o1[h] + (e2 / den) * o2[h]


def _dil_mix(parts, tm=1024):
    B, _, S, _ = parts[0].shape
    spb = S // tm
    spec = pl.BlockSpec((None, HEADS, tm, HD), lambda i: (i // spb, 0, i % spb, 0))
    return pl.pallas_call(
        _dil_mix_body,
        grid=(B * spb,),
        in_specs=[spec] * 6,
        out_specs=pl.BlockSpec((tm, ATT_COLS), lambda i: (i, 0)),
        out_shape=jax.ShapeDtypeStruct((B * S, ATT_COLS), F32),
        compiler_params=_cparams("parallel"),
        name="dil_mix",
    )(*parts)


def _masked_attn(q, k, v, kn, vn, mask_b, mask_n, scale):
    sb = jnp.where(mask_b, _dot_nt(q, k) * scale, NEG)
    sn = jnp.where(mask_n, _dot_nt(q, kn) * scale, NEG)
    m = jnp.maximum(jnp.max(sb, axis=-1, keepdims=True), jnp.max(sn, axis=-1, keepdims=True))
    eb = jnp.exp(sb - m)
    en = jnp.exp(sn - m)
    den = jnp.sum(eb, axis=-1, keepdims=True) + jnp.sum(en, axis=-1, keepdims=True)
    o = _dot((eb / den).astype(BF16), v) + _dot((en / den).astype(BF16), vn)
    return o, m + jnp.log(den)


def _dil_dec_body(T, q_ref, kn_ref, vn_ref, k0, v0, k1, v1, k2, v2, od_ref, nk0, nv0, nk1, nv1, nk2, nv2):
    scale = HD ** -0.5
    R = T * HEADS
    caches = ((k0, v0, nk0, nv0), (k1, v1, nk1, nv1), (k2, v2, nk2, nv2))
    rn = lax.broadcasted_iota(jnp.int32, (R, R), 0)
    cn = lax.broadcasted_iota(jnp.int32, (R, R), 1)
    outs, lses = [], []
    for gi, (_, r) in enumerate(DIL_GROUPS):
        kc, vc, nkc, nvc = caches[gi]
        rows = kc.shape[0]
        for src, new, dst in ((kc, kn_ref, nkc), (vc, vn_ref, nvc)):
            dst[pl.ds(0, rows - R), :] = src[pl.ds(R, rows - R), :]
            dst[pl.ds(rows - R, R), :] = new[gi]
        if r <= T:
            n = rows
            kb = kc[...].astype(BF16)
            vb = vc[...].astype(BF16)
        else:
            n = rows // (r * HEADS) * R
            kb = kc[...].reshape(rows // (r * HEADS), r * HEADS, HD)[:, 0:R, :].reshape(n, HD).astype(BF16)
            vb = vc[...].reshape(rows // (r * HEADS), r * HEADS, HD)[:, 0:R, :].reshape(n, HD).astype(BF16)
        rb = lax.broadcasted_iota(jnp.int32, (R, n), 0)
        cb = lax.broadcasted_iota(jnp.int32, (R, n), 1)
        if r == 1:
            mask_b = jnp.logical_and(cb % HEADS == rb % HEADS, cb // HEADS >= rb // HEADS)
            mask_n = jnp.logical_and(cn % HEADS == rn % HEADS, cn // HEADS <= rn // HEADS)
        else:
            mask_b = cb % R == rb
            mask_n = cn == rn
        o, lse = _masked_attn(q_ref[gi].astype(BF16), kb, vb, kn_ref[gi].astype(BF16), vn_ref[gi].astype(BF16),
                              mask_b, mask_n, scale)
        outs.append(o)
        lses.append(lse)
    l0, l1, l2 = lses
    m = jnp.maximum(jnp.maximum(l0, l1), l2)
    e0, e1, e2 = jnp.exp(l0 - m), jnp.exp(l1 - m), jnp.exp(l2 - m)
    den = e0 + e1 + e2
    od_ref[...] = (e0 / den) * outs[0] + (e1 / den) * outs[1] + (e2 / den) * outs[2]


def _dil_decode(zd, caches):
    Bd, T, _ = zd.shape
    R = T * HEADS
    z6 = zd.reshape(Bd, T, N_DIL, 3, HEADS, HD)
    qkv = [jnp.transpose(z6[:, :, :, c], (0, 2, 1, 3, 4)).reshape(Bd, N_DIL, R, HD) for c in range(3)]
    new = pl.BlockSpec((None, N_DIL, R, HD), lambda b: (b, 0, 0, 0))
    args, specs = list(qkv), [new, new, new]
    out_specs = [pl.BlockSpec((None, R, HD), lambda b: (b, 0, 0))]
    out_shape = [jax.ShapeDtypeStruct((Bd, R, HD), F32)]
    for (wd, r), (kc, vc) in zip(DIL_GROUPS, caches):
        assert kc.shape[1] == wd and wd > T and R % SUBLANES == 0 and (r <= T or wd % r == 0)
        for a in (kc, vc):
            spec = pl.BlockSpec((None, wd * HEADS, HD), lambda b: (b, 0, 0))
            args.append(a.reshape(Bd, wd * HEADS, HD))
            specs.append(spec)
            out_specs.append(spec)
            out_shape.append(jax.ShapeDtypeStruct((Bd, wd * HEADS, HD), F32))
    od, *rolled = pl.pallas_call(
        functools.partial(_dil_dec_body, T),
        grid=(Bd,),
        in_specs=specs,
        out_specs=out_specs,
        out_shape=out_shape,
        compiler_params=_cparams("parallel"),
        name="dil_decode",
    )(*args)
    bufs = [(rolled[2 * gi].reshape(Bd, wd, HEADS, HD), rolled[2 * gi + 1].reshape(Bd, wd, HEADS, HD))
            for gi, (wd, _) in enumerate(DIL_GROUPS)]
    return od.reshape(Bd * T, ATT_COLS), bufs


def _mem_body(q_ref, k_ref, v_ref, o_ref):
    scale = HD ** -0.5
    for h in range(HEADS):
        sl = slice(h * HD, (h + 1) * HD)
        s = _dot_nt(q_ref[:, sl].astype(BF16), k_ref[:, sl].astype(BF16)) * scale
        e = jnp.exp(s - jnp.max(s, axis=-1, keepdims=True))
        p = e / jnp.sum(e, axis=-1, keepdims=True)
        o_ref[:, sl] = _dot(p.astype(BF16), v_ref[:, sl].astype(BF16))


def _mem_attn(qm, mkv, tq):
    B, S, _ = qm.shape
    M = mkv.shape[1]
    return pl.pallas_call(
        _mem_body,
        grid=(B, S // tq),
        in_specs=[
            pl.BlockSpec((None, tq, ATT_COLS), lambda b, i: (b, i, 0)),
            pl.BlockSpec((None, M, ATT_COLS), lambda b, i: (b, 0, 0)),
            pl.BlockSpec((None, M, ATT_COLS), lambda b, i: (b, 0, 1)),
        ],
        out_specs=pl.BlockSpec((None, tq, ATT_COLS), lambda b, i: (b, i, 0)),
        out_shape=jax.ShapeDtypeStruct((B, S, ATT_COLS), F32),
        compiler_params=_cparams("parallel", "arbitrary"),
        name="mem_attn",
    )(qm, mkv, mkv)


def _mem_dec_body(q_ref, k_ref, v_ref, o_ref):
    R, n = q_ref.shape[0], k_ref.shape[0]
    rb = lax.broadcasted_iota(jnp.int32, (R, n), 0)
    cb = lax.broadcasted_iota(jnp.int32, (R, n), 1)
    s = _dot_nt(q_ref[...].astype(BF16), k_ref[...].astype(BF16)) * (HD ** -0.5)
    s = jnp.where(cb % HEADS == rb % HEADS, s, -jnp.inf)
    e = jnp.exp(s - jnp.max(s, axis=-1, keepdims=True))
    p = e / jnp.sum(e, axis=-1, keepdims=True)
    o_ref[...] = _dot(p.astype(BF16), v_ref[...].astype(BF16))


def _mem_attn_decode(qm, mk, mv):
    Bd, T, _ = qm.shape
    M = mk.shape[1]
    R = T * HEADS
    kv = pl.BlockSpec((None, M * HEADS, HD), lambda b: (b, 0, 0))
    o = pl.pallas_call(
        _mem_dec_body,
        grid=(Bd,),
        in_specs=[pl.BlockSpec((None, R, HD), lambda b: (b, 0, 0)), kv, kv],
        out_specs=pl.BlockSpec((None, R, HD), lambda b: (b, 0, 0)),
        out_shape=jax.ShapeDtypeStruct((Bd, R, HD), F32),
        compiler_params=_cparams("parallel"),
        name="mem_attn_decode",
    )(qm.reshape(Bd, R, HD), mk.reshape(Bd, M * HEADS, HD), mv.reshape(Bd, M * HEADS, HD))
    return o.reshape(Bd * T, ATT_COLS)


def _merge_body(x_ref, og_ref, r_ref, od_ref, om_ref, gt_ref, ong_ref, wbg_ref, wbd_ref, wbm_ref, wo_ref,
                ln2_ref, wq_ref, h_ref, n2_ref, qh_ref):
    og = og_ref[...]
    r = r_ref[...]
    ong = ong_ref[...]
    parts = []
    for h in range(GLA_HEADS):
        sl = slice(h * GLA_DV, (h + 1) * GLA_DV)
        parts.append(_rms(og[:, sl], ong) * jax.nn.silu(r[:, sl]))
    br_gla = _dot(jnp.concatenate(parts, axis=-1).astype(BF16), wbg_ref[...])
    br_dil = _dot(od_ref[...].astype(BF16), wbd_ref[...])
    br_mem = _dot(om_ref[...].astype(BF16), wbm_ref[...])
    d = D_MODEL
    merged = gt_ref[:, 0:d] * br_gla + gt_ref[:, d:2 * d] * br_dil + gt_ref[:, 2 * d:3 * d] * br_mem
    hres = x_ref[...] + _dot(merged.astype(BF16), wo_ref[...])
    h_ref[...] = hres
    n2 = _rms(hres, ln2_ref[...]).astype(BF16)
    n2_ref[...] = n2
    qh_ref[...] = _dot(n2, wq_ref[...])


def _merge(x, og, zg, od, om, gt, ong, wbg, wbd, wbm, wo, ln2, wq, tm=256):
    nt, d = x.shape
    tm = min(tm, nt)
    nq = wq.shape[1]
    rblk = (2 * GLA_HEADS * GLA_DK + GLA_HEADS * GLA_DV) // d

    def tok(cols, blk=0):
        return pl.BlockSpec((tm, cols), lambda i: (i, blk))

    def const(shape):
        return pl.BlockSpec(shape, lambda i: (0, 0), pipeline_mode=pl.Buffered(1))

    return pl.pallas_call(
        _merge_body,
        grid=(nt // tm,),
        in_specs=[
            tok(d), tok(d), tok(d, rblk), tok(ATT_COLS), tok(ATT_COLS), tok(3 * d),
            const((1, GLA_DV)), const(wbg.shape), const(wbd.shape), const(wbm.shape), const(wo.shape),
            const((1, d)), const(wq.shape),
        ],
        out_specs=[tok(d), tok(d), tok(nq)],
        out_shape=[
            jax.ShapeDtypeStruct((nt, d), F32),
            jax.ShapeDtypeStruct((nt, d), BF16),
            jax.ShapeDtypeStruct((nt, nq), F32),
        ],
        compiler_params=_cparams("parallel"),
        name="merge",
    )(x, og, zg, od, om, gt, ong.reshape(1, GLA_DV), wbg, wbd, wbm, wo, ln2.reshape(1, d), wq)


PEER_CAND_GROUPS = 10


def _peer_cand_layout(tp):
    r = lax.broadcasted_iota(jnp.int32, (PEER_CAND_GROUPS * SUBLANES, tp), 0)
    grp = r // SUBLANES
    p = r % SUBLANES
    K = PEER_TOPK
    k1 = jnp.where(grp < 2, 0, jnp.where(grp < 9, grp - 1, SUBLANES + p))
    k2 = jnp.where(grp < 2, r, jnp.where(grp < 9, p, 0))
    valid = (k1 + 1) * (k2 + 1) <= K
    return (k1 * K + k2).astype(F32), valid


def _peer_cands(x1, x2):
    top, bot = x2[0:SUBLANES], x2[SUBLANES:2 * SUBLANES]
    g1 = [jnp.broadcast_to(x1[0:1], top.shape)] * 2 + [jnp.broadcast_to(x1[k:k + 1], top.shape) for k in range(1, 8)]
    g1.append(x1[SUBLANES:2 * SUBLANES])
    g2 = [top, bot] + [top] * 7 + [jnp.broadcast_to(x2[0:1], top.shape)]
    return jnp.concatenate(g1, axis=0), jnp.concatenate(g2, axis=0)


def _peer_topk_body(qh_ref, keys_ref, a_ref, b_ref, gw_ref, s_scr, sv_scr, si_scr, cand_scr, eid_scr, sc_scr, ev_scr):
    tp = qh_ref.shape[0]
    K = PEER_TOPK
    NK = PEER_NKEYS
    NH = PEER_HEADS
    rowf = lax.broadcasted_iota(jnp.int32, (NK, tp), 0).astype(F32)
    pos, valid = _peer_cand_layout(tp)
    for i in range(2 * NH):
        s_scr[i] = _dot_nt(keys_ref[i], qh_ref[:, i * LANES:(i + 1) * LANES].astype(BF16))

    def step1(k, carry):
        for i in range(2 * NH):
            s = s_scr[i]
            mx = jnp.max(s, axis=0, keepdims=True)
            idx = jnp.min(jnp.where(s == mx, rowf, float(NK)), axis=0, keepdims=True)
            s_scr[i] = jnp.where(rowf == idx, -jnp.inf, s)
            sv_scr[i, pl.ds(k, 1), :] = mx
            si_scr[i, pl.ds(k, 1), :] = idx
        return carry

    lax.fori_loop(0, K, step1, 0)

    for h in range(NH):
        c1, c2 = _peer_cands(sv_scr[2 * h], sv_scr[2 * h + 1])
        e1, e2 = _peer_cands(si_scr[2 * h], si_scr[2 * h + 1])
        cand_scr[h] = jnp.where(valid, c1 + c2, -jnp.inf)
        eid_scr[h] = e1 * float(NK) + e2

    def step2(k, carry):
        for h in range(NH):
            cand = cand_scr[h]
            mx = jnp.max(cand, axis=0, keepdims=True)
            first = jnp.min(jnp.where(cand == mx, pos, float(K * K)), axis=0, keepdims=True)
            hit = pos == first
            ev_scr[h, pl.ds(k, 1), :] = jnp.max(jnp.where(hit, eid_scr[h], -1.0), axis=0, keepdims=True)
            sc_scr[h, pl.ds(k, 1), :] = mx
            cand_scr[h] = jnp.where(hit, -jnp.inf, cand)
        return carry

    lax.fori_loop(0, K, step2, 0)

    a_rows, b_rows, g_rows = [], [], []
    for h in range(NH):
        sc, ev = sc_scr[h], ev_scr[h]
        e = jnp.exp(sc - jnp.max(sc, axis=0, keepdims=True))
        g_rows.append(e / jnp.sum(e, axis=0, keepdims=True))
        a = jnp.floor(ev * (1.0 / NK))
        a_rows.append(a)
        b_rows.append(ev - a * float(NK))
    a_ref[...] = jnp.transpose(jnp.concatenate(a_rows, axis=0))
    b_ref[...] = jnp.transpose(jnp.concatenate(b_rows, axis=0))
    gw_ref[...] = jnp.transpose(jnp.concatenate(g_rows, axis=0))


def _peer_topk(qh, keys, tp=128):
    nt, nq = qh.shape
    tp = min(tp, nt)
    spec = pl.BlockSpec((tp, LANES), lambda i: (i, 0))
    shp = jax.ShapeDtypeStruct((nt, LANES), F32)
    K, NH = PEER_TOPK, PEER_HEADS
    ncand = PEER_CAND_GROUPS * SUBLANES
    return pl.pallas_call(
        _peer_topk_body,
        grid=(nt // tp,),
        in_specs=[
            pl.BlockSpec((tp, nq), lambda i: (i, 0)),
            pl.BlockSpec(keys.shape, lambda i: (0, 0, 0)),
        ],
        out_specs=[spec, spec, spec],
        out_shape=[shp, shp, shp],
        scratch_shapes=[
            pltpu.VMEM((2 * NH, PEER_NKEYS, tp), F32), pltpu.VMEM((2 * NH, K, tp), F32), pltpu.VMEM((2 * NH, K, tp), F32),
            pltpu.VMEM((NH, ncand, tp), F32), pltpu.VMEM((NH, ncand, tp), F32),
            pltpu.VMEM((NH, K, tp), F32), pltpu.VMEM((NH, K, tp), F32),
        ],
        compiler_params=_cparams("parallel"),
        name="peer_topk",
    )(qh, keys)


def _peer_w_body(a_ref, b_ref, gw_ref, w_ref, scr):
    tw = a_ref.shape[0]
    NK = PEER_NKEYS
    stride = scr.shape[0] // NK
    sub = lax.broadcasted_iota(jnp.int32, (NK, LANES), 0).astype(F32)

    zero = jnp.zeros((NK, LANES), BF16)

    def tok2(p, carry):
        oas, gbs = [], []
        for t in (2 * p, 2 * p + 1):
            arow = a_ref[pl.ds(t, 1), :]
            brow = b_ref[pl.ds(t, 1), :]
            grow = gw_ref[pl.ds(t, 1), :]
            oas.append((arow == sub).astype(BF16))
            gbs.append(jnp.where(brow == sub, grow, 0.0).astype(BF16))
        lhs = jnp.concatenate(oas, axis=1)
        rhs = jnp.concatenate([jnp.concatenate([gbs[0], zero], axis=1),
                               jnp.concatenate([zero, gbs[1]], axis=1)], axis=0)
        w2 = _dot_nt(lhs, rhs)
        scr[pl.ds(2 * p, NK, stride=stride), :] = w2[:, 0:LANES]
        scr[pl.ds(2 * p + 1, NK, stride=stride), :] = w2[:, LANES:2 * LANES]
        return carry

    lax.fori_loop(0, tw // 2, tok2, 0, unroll=32)
    for i1 in range(NK):
        w_ref[:, i1 * NK:(i1 + 1) * NK] = scr[i1 * stride:i1 * stride + tw, :].astype(BF16)


def _peer_weights(a, b, gw, n_experts, tw=128):
    nt = a.shape[0]
    tw = min(tw, nt)
    stride = tw + SUBLANES
    tok = pl.BlockSpec((tw, LANES), lambda i: (i, 0))
    return pl.pallas_call(
        _peer_w_body,
        grid=(nt // tw,),
        in_specs=[tok, tok, tok],
        out_specs=pl.BlockSpec((tw, n_experts), lambda i: (i, 0)),
        out_shape=jax.ShapeDtypeStruct((nt, n_experts), BF16),
        scratch_shapes=[pltpu.VMEM((PEER_NKEYS * stride, LANES), F32)],
        compiler_params=_cparams("parallel"),
        name="peer_weights",
    )(a, b, gw)


def _peer_mix_body(n2_ref, w_ref, h_ref, u_ref, v_ref, y_ref, acc):
    j = pl.program_id(1)

    @pl.when(j == 0)
    def _():
        acc[...] = jnp.zeros(acc.shape, F32)

    hpre = _dot_nt(n2_ref[...], u_ref[...])
    gelu = 0.5 * hpre * (1.0 + lax.erf(hpre * (2.0 ** -0.5)))
    act = (gelu * w_ref[...].astype(F32)).astype(BF16)
    acc[...] += _dot(act, v_ref[...])

    @pl.when(j == pl.num_programs(1) - 1)
    def _():
        y_ref[...] = h_ref[...] + acc[...]


def _peer_mix(n2, w, h, u, v):
    nt, d = n2.shape
    T = min(PEER_T, nt)
    assert nt % T == 0 and u.shape[0] % PEER_EB == 0
    tok = pl.BlockSpec((T, d), lambda i, j: (i, 0))
    tab = pl.BlockSpec((PEER_EB, d), lambda i, j: (j, 0))
    return pl.pallas_call(
        _peer_mix_body,
        grid=(nt // T, u.shape[0] // PEER_EB),
        in_specs=[tok, pl.BlockSpec((T, PEER_EB), lambda i, j: (i, j)), tok, tab, tab],
        out_specs=tok,
        out_shape=jax.ShapeDtypeStruct((nt, d), F32),
        scratch_shapes=[pltpu.VMEM((T, d), F32)],
        compiler_params=_cparams("parallel", "arbitrary"),
        name="peer_mix",
    )(n2, w, h, u, v)


def _rope_tables(pos):
    half = HD // 2
    inv = ROPE_THETA ** (-jnp.arange(half, dtype=F32) / half)
    ang = pos.astype(F32)[:, None] * inv[None, :]
    cos, sin = jnp.cos(ang), jnp.sin(ang)
    return jnp.concatenate([cos, cos], axis=-1), jnp.concatenate([-sin, sin], axis=-1)


def _prep_weights(w_in, gla_wg2, peer_wq, peer_keys, peer_u, peer_v, w_br_gla, w_br_dil, w_br_mem, w_out):
    c0 = GLA_COLS
    c1 = c0 + GLA_RANK
    c2 = c1 + DIL_COLS
    c3 = c2 + ATT_COLS
    wa = jnp.pad(w_in[:, c0:c1], ((0, 0), (0, LANES - GLA_RANK))).astype(BF16)
    wg2 = jnp.pad(gla_wg2, ((0, LANES - GLA_RANK), (0, 0))).astype(BF16)
    return dict(
        w_gla=w_in[:, :c0].astype(BF16), wa=wa, wg2=wg2,
        w_dil=w_in[:, c1:c2].astype(BF16), w_qm=w_in[:, c2:c3].astype(BF16), w_gt=w_in[:, c3:].astype(BF16),
        wbg=w_br_gla.astype(BF16), wbd=w_br_dil.astype(BF16), wbm=w_br_mem.astype(BF16), wo=w_out.astype(BF16),
        wq=peer_wq.astype(BF16),
        keys=peer_keys.reshape(PEER_HEADS * 2, PEER_NKEYS, LANES).astype(BF16),
        u=peer_u.astype(BF16), v=peer_v.astype(BF16),
    )


def _layer(x, pos, mem_k, mem_v, s0, caches, p, W):
    B, S, d = x.shape
    nt = B * S
    xt = x.reshape(nt, d)
    ln1 = p["ln1_g"]
    zg = _proj("gla", xt, ln1, W["w_gla"])
    lg = _gla_gate(xt, ln1, W["wa"], W["wg2"], p["gla_bg"])
    cosf, sinf = _rope_tables(pos)
    cosf = jnp.broadcast_to(cosf[None], (B, S, HD)).reshape(nt, HD)
    sinf = jnp.broadcast_to(sinf[None], (B, S, HD)).reshape(nt, HD)
    hg = jnp.stack([p["dil_qn_g"], p["dil_kn_g"]])
    qm = _proj("mem_q", xt, ln1, W["w_qm"], extra=(p["mem_qn_g"].reshape(1, HD),))
    gt = _proj("sigmoid", xt, ln1, W["w_gt"])

    prompt = caches is None
    og, s_fin = _gla(zg.reshape(B, S, GLA_COLS), lg.reshape(B, S, GLA_HEADS * GLA_DK), s0,
                     GLA_CHUNK if prompt else S)

    new_bufs = []
    gcols = 3 * ATT_COLS
    if prompt:
        parts = []
        for gi, (wd, r) in enumerate(DIL_GROUPS):
            zr = _proj_dil_group(xt, ln1, W["w_dil"][:, gi * gcols:(gi + 1) * gcols], hg, cosf, sinf, B, S, r)
            parts.extend(_dil_prompt(zr, S))
            keep = min(wd, S)
            kv = zr[:, :, (S - keep) // r:, ATT_COLS:]
            kv = jnp.swapaxes(kv, 1, 2).reshape(B, keep, 2, HEADS, HD)
            new_bufs.append((kv[:, :, 0], kv[:, :, 1]))
        od = _dil_mix(parts)
    else:
        zd3 = _proj("dil", xt, ln1, W["w_dil"], extra=(hg, cosf, sinf)).reshape(B, S, DIL_COLS)
        od, new_bufs = _dil_decode(zd3, caches)

    qm3 = qm.reshape(B, S, ATT_COLS)
    if prompt:
        om = _mem_attn(qm3, mem_k, 512).reshape(nt, ATT_COLS)
    else:
        om = _mem_attn_decode(qm3, mem_k, mem_v)

    h, n2, qh = _merge(xt, og.reshape(nt, d), zg, od, om, gt, p["gla_onorm_g"], W["wbg"], W["wbd"], W["wbm"],
                       W["wo"], p["ln2_g"], W["wq"])
    a, b, gw = _peer_topk(qh, W["keys"])
    y = _peer_mix(n2, _peer_weights(a, b, gw, W["u"].shape[0]), h, W["u"], W["v"])
    return y.reshape(B, S, d), s_fin, new_bufs


def kernel(x_prompt, x_sample, mem_prompt, state_gla, cache_dil_k0, cache_dil_v0, cache_dil_k1, cache_dil_v1, cache_dil_k2, cache_dil_v2, cache_mem_k, cache_mem_v, ln1_g, w_in, gla_wg2, gla_bg, gla_onorm_g, dil_qn_g, dil_kn_g, mem_norm_g, w_mem_kv, mem_qn_g, mem_kn_g, w_br_gla, w_br_dil, w_br_mem, w_out, ln2_g, peer_wq, peer_keys, peer_u, peer_v):
    B, S, d = x_prompt.shape
    Bd, T, _ = x_sample.shape
    p = dict(ln1_g=ln1_g, gla_bg=gla_bg, gla_onorm_g=gla_onorm_g, dil_qn_g=dil_qn_g, dil_kn_g=dil_kn_g,
             mem_qn_g=mem_qn_g, ln2_g=ln2_g)
    W = _prep_weights(w_in, gla_wg2, peer_wq, peer_keys, peer_u, peer_v, w_br_gla, w_br_dil, w_br_mem, w_out)

    M = mem_prompt.shape[1]
    mkv = _proj("mem_kv", mem_prompt.reshape(B * M, d), mem_norm_g, w_mem_kv.astype(BF16),
                extra=(mem_kn_g.reshape(1, HD),))
    mem_k_p = mkv[:, :ATT_COLS].reshape(B, M, HEADS, HD)
    mem_v_p = mkv[:, ATT_COLS:].reshape(B, M, HEADS, HD)

    s0 = jnp.zeros((B, GLA_HEADS, GLA_DK, GLA_DV), F32)
    mkv3 = mkv.reshape(B, M, 2 * ATT_COLS)
    y_prompt, gla_state_p, bufs_p = _layer(x_prompt, jnp.arange(S, dtype=jnp.int32), mkv3, mkv3, s0, None, p, W)

    caches = ((cache_dil_k0, cache_dil_v0), (cache_dil_k1, cache_dil_v1), (cache_dil_k2, cache_dil_v2))
    pos_s = PAST_LEN + jnp.arange(T, dtype=jnp.int32)
    y_sample, gla_state_s, bufs_s = _layer(x_sample, pos_s, cache_mem_k, cache_mem_v, state_gla, caches, p, W)

    (dk0_p, dv0_p), (dk1_p, dv1_p), (dk2_p, dv2_p) = bufs_p
    (dk0_s, dv0_s), (dk1_s, dv1_s), (dk2_s, dv2_s) = bufs_s
    return (y_prompt, y_sample,
            gla_state_p, dk0_p, dv0_p, dk1_p, dv1_p, dk2_p, dv2_p, mem_k_p, mem_v_p,
            gla_state_s, dk0_s, dv0_s, dk1_s, dv1_s, dk2_s, dv2_s)
```

```python
import functools

import jax
import jax.numpy as jnp
from jax import lax
from jax.experimental import pallas as pl
from jax.experimental.pallas import tpu as pltpu

F32 = jnp.float32
BF16 = jnp.bfloat16

D_MODEL = 1024
PAST_LEN = 8192
GLA_HEADS = 4
GLA_DK = 128
GLA_DV = 256
GLA_RANK = 16
GLA_TAU = 16.0
DIL_GROUPS = ((128, 1), (512, 4), (2048, 16))
N_DIL = 3
HEADS = 4
HD = 128
SPAN = 128
MEM_LEN = 256
ROPE_THETA = 10000.0
PEER_HEADS = 8
PEER_NKEYS = 128
PEER_TOPK = 16
EPS = 1e-6
NEG = -1e30

LANES = 128
SUBLANES = 8
GLA_COLS = 2 * GLA_HEADS * GLA_DK + 2 * GLA_HEADS * GLA_DV
DIL_COLS = N_DIL * 3 * HEADS * HD
ATT_COLS = HEADS * HD
VMEM_LIMIT = 56 * 1024 * 1024

PEER_T = 1024
PEER_EB = 1024


def _cparams(*sem):
    return pltpu.CompilerParams(dimension_semantics=sem, vmem_limit_bytes=VMEM_LIMIT)


def _rms(x, g):
    return x * lax.rsqrt(jnp.mean(x * x, axis=-1, keepdims=True) + EPS) * g


def _dot(a, b):
    return jnp.dot(a, b, preferred_element_type=F32)


def _dot_nt(a, b):
    return lax.dot_general(a, b, (((1,), (1,)), ((), ())), preferred_element_type=F32)


def _split3(x):
    hi = x.astype(BF16)
    r1 = x - hi.astype(F32)
    mid = r1.astype(BF16)
    lo = (r1 - mid.astype(F32)).astype(BF16)
    return hi, mid, lo


def _proj_body(kind, *refs):
    if kind == "dil":
        x_ref, g_ref, w_ref, hg_ref, cos_ref, sin_ref, o_ref, n_scr = refs
    elif kind in ("mem_q", "mem_kv"):
        x_ref, g_ref, w_ref, hg_ref, o_ref, n_scr = refs
    else:
        x_ref, g_ref, w_ref, o_ref, n_scr = refs
    j = pl.program_id(1)

    @pl.when(j == 0)
    def _():
        n_scr[...] = _rms(x_ref[...], g_ref[...]).astype(BF16)

    z = _dot(n_scr[...], w_ref[...])
    if kind == "gla":
        col = lax.broadcasted_iota(jnp.int32, (1, z.shape[1]), 1) + j * z.shape[1]
        o_ref[...] = z * jnp.where(col < GLA_HEADS * GLA_DK, GLA_DK ** -0.5, 1.0).astype(F32)
    elif kind == "sigmoid":
        o_ref[...] = jax.nn.sigmoid(z)
    elif kind == "mem_q":
        hg = hg_ref[...]
        o_ref[...] = jnp.concatenate([_rms(z[:, h * HD:(h + 1) * HD], hg) for h in range(HEADS)], axis=-1)
    elif kind == "mem_kv":
        @pl.when(j == 0)
        def _():
            hg = hg_ref[...]
            o_ref[...] = jnp.concatenate([_rms(z[:, h * HD:(h + 1) * HD], hg) for h in range(HEADS)], axis=-1)

        @pl.when(j != 0)
        def _():
            o_ref[...] = z
    elif kind == "dil":
        c = j % 3

        @pl.when(c == 2)
        def _():
            o_ref[...] = z

        @pl.when(c != 2)
        def _():
            hg = jnp.where(c == 0, hg_ref[0:1, :], hg_ref[1:2, :])
            cs = cos_ref[...]
            sn = sin_ref[...]
            parts = []
            for h in range(HEADS):
                zh = _rms(z[:, h * HD:(h + 1) * HD], hg)
                parts.append(zh * cs + pltpu.roll(zh, HD // 2, 1) * sn)
            o_ref[...] = jnp.concatenate(parts, axis=-1)
    else:
        raise ValueError(kind)


def _proj(kind, x, g, w, extra=(), tm=1024, tn=1024):
    nt, d = x.shape
    ncol = w.shape[1]
    tm = min(tm, nt)
    if kind in ("dil", "mem_q", "mem_kv"):
        tn = ATT_COLS
    assert nt % tm == 0 and ncol % tn == 0
    in_specs = [
        pl.BlockSpec((tm, d), lambda i, j: (i, 0)),
        pl.BlockSpec((1, d), lambda i, j: (0, 0)),
        pl.BlockSpec((d, tn), lambda i, j: (0, j)),
    ]
    if kind == "dil":
        in_specs += [
            pl.BlockSpec((2, HD), lambda i, j: (0, 0)),
            pl.BlockSpec((tm, HD), lambda i, j: (i, 0)),
            pl.BlockSpec((tm, HD), lambda i, j: (i, 0)),
        ]
    elif kind in ("mem_q", "mem_kv"):
        in_specs += [pl.BlockSpec((1, HD), lambda i, j: (0, 0))]
    return pl.pallas_call(
        functools.partial(_proj_body, kind),
        grid=(nt // tm, ncol // tn),
        in_specs=in_specs,
        out_specs=pl.BlockSpec((tm, tn), lambda i, j: (i, j)),
        out_shape=jax.ShapeDtypeStruct((nt, ncol), F32),
        scratch_shapes=[pltpu.VMEM((tm, d), BF16)],
        compiler_params=_cparams("parallel", "arbitrary"),
        name="proj_" + kind,
    )(x, g.reshape(1, d), w, *extra)


def _gate_body(x_ref, g_ref, wa_ref, wg2_ref, bg_ref, o_ref):
    n = _rms(x_ref[...], g_ref[...]).astype(BF16)
    a = _dot(n, wa_ref[...])
    pre = _dot(a.astype(BF16), wg2_ref[...]) + bg_ref[...]
    o_ref[...] = jax.nn.log_sigmoid(pre) / GLA_TAU


def _gla_gate(x, g, wa, wg2, bg, tm=512):
    nt, d = x.shape
    tm = min(tm, nt)
    ncol = wg2.shape[1]
    return pl.pallas_call(
        _gate_body,
        grid=(nt // tm,),
        in_specs=[
            pl.BlockSpec((tm, d), lambda i: (i, 0)),
            pl.BlockSpec((1, d), lambda i: (0, 0)),
            pl.BlockSpec((d, LANES), lambda i: (0, 0)),
            pl.BlockSpec((LANES, ncol), lambda i: (0, 0)),
            pl.BlockSpec((1, ncol), lambda i: (0, 0)),
        ],
        out_specs=pl.BlockSpec((tm, ncol), lambda i: (i, 0)),
        out_shape=jax.ShapeDtypeStruct((nt, ncol), F32),
        compiler_params=_cparams("parallel"),
        name="gla_gate",
    )(x, g.reshape(1, d), wa, wg2, bg.reshape(1, ncol))


GLA_CHUNK = 128
GLA_SUB = 16


def _gla_intra(q, k, b, qs_scr, bs_scr, h, n_sub):
    P, SB = GLA_CHUNK, GLA_SUB
    rowp = lax.broadcasted_iota(jnp.int32, (P, 1), 0)
    qparts, kparts = [], []
    for j in range(n_sub - 1):
        lo, hi = SB * j, SB * (j + 1)
        be = b[hi - 1:hi, :]
        qparts.append(jnp.where(rowp >= hi, q * jnp.exp(b - be), 0.0).astype(BF16))
        kblk = (k[lo:hi] * jnp.exp(be - b[lo:hi])).astype(BF16)
        pieces = [kblk]
        if lo:
            pieces.insert(0, jnp.zeros((lo, k.shape[1]), BF16))
        pieces.append(jnp.zeros((P - hi, k.shape[1]), BF16))
        kparts.append(jnp.concatenate(pieces, axis=0))
    if qparts:
        a_off = _dot_nt(jnp.concatenate(qparts, axis=1), jnp.concatenate(kparts, axis=1))
    else:
        a_off = jnp.zeros((P, P), F32)
    HS = SB // 2
    lane = lax.broadcasted_iota(jnp.int32, (HS, P), 1)
    srow = lax.broadcasted_iota(jnp.int32, (HS, P), 0)
    tiles = []
    for j in range(P // SB):
        if j >= n_sub:
            tiles.append(jnp.zeros((SB, P), F32))
            continue
        lo, mid, hi = SB * j, SB * j + HS, SB * (j + 1)
        halves = []
        for s0 in (lo, mid):
            kj, bj = k[s0:s0 + HS], b[s0:s0 + HS]
            at = jnp.zeros((HS, P), F32)
            for t in range(s0, hi):
                bt = bs_scr[h, pl.ds(t, 1), :]
                qt = qs_scr[h, pl.ds(t, 1), :]
                colv = jnp.sum(qt * kj * jnp.exp(jnp.minimum(bt - bj, 0.0)), axis=-1, keepdims=True)
                at = jnp.where(lane == t, colv, at)
            halves.append(jnp.where(srow + s0 <= lane, at, 0.0))
        tiles.extend(halves)
    return a_off + jnp.transpose(jnp.concatenate(tiles, axis=0))


def _gla_body(tc, q_ref, k_ref, v_ref, g_ref, s0_ref, o_ref, sf_ref, s_scr, qp, kp, vp, gp, qs_scr, bs_scr):
    c = pl.program_id(1)
    P = GLA_CHUNK
    H, dk, dv = GLA_HEADS, GLA_DK, GLA_DV
    n_sub = -(-tc // GLA_SUB)

    @pl.when(c == 0)
    def _():
        s_scr[...] = s0_ref[...]

    if tc == P:
        q, k, v, g = q_ref[...], k_ref[...], v_ref[...], g_ref[...]
    else:
        for pad, ref in ((qp, q_ref), (kp, k_ref), (vp, v_ref), (gp, g_ref)):
            pad[...] = jnp.zeros(pad.shape, F32)
            pad[0:tc, :] = ref[...]
        q, k, v, g = qp[...], kp[...], vp[...], gp[...]

    row = lax.broadcasted_iota(jnp.int32, (P, P), 0)
    col = lax.broadcasted_iota(jnp.int32, (P, P), 1)
    trib = (row >= col).astype(BF16)
    g1, g2, g3 = _split3(g)
    b = _dot(trib, g1) + _dot(trib, g2) + _dot(trib, g3)
    b_end = b[P - 1:P, :]
    qe = (q * jnp.exp(b)).astype(BF16)
    kd = k * jnp.exp(b_end - b)
    vb = v.astype(BF16)
    for h in range(H):
        ks = slice(h * dk, (h + 1) * dk)
        qs_scr[h] = q[:, ks]
        bs_scr[h] = b[:, ks]
    for h in range(H):
        ks = slice(h * dk, (h + 1) * dk)
        vs = slice(h * dv, (h + 1) * dv)
        a = _gla_intra(q[:, ks], k[:, ks], b[:, ks], qs_scr, bs_scr, h, n_sub)
        s = s_scr[h]
        o = _dot(qe[:, ks], s.astype(BF16)) + _dot(a.astype(BF16), vb[:, vs])
        o_ref[:, vs] = o[0:tc, :]
        decay = jnp.exp(jnp.sum(jnp.transpose(g[:, ks]), axis=1, keepdims=True))
        s_new = decay * s + _dot(jnp.transpose(kd[:, ks]).astype(BF16), vb[:, vs])
        s_scr[h] = s_new

        @pl.when(c == pl.num_programs(1) - 1)
        def _():
            sf_ref[h] = s_new


def _gla(zg, lg, s0, tc):
    B, S, _ = zg.shape
    H, dk, dv = GLA_HEADS, GLA_DK, GLA_DV
    assert S % tc == 0
    P = GLA_CHUNK
    qk = H * dk
    st = pl.BlockSpec((None, H, dk, dv), lambda b, c: (b, 0, 0, 0))
    return pl.pallas_call(
        functools.partial(_gla_body, tc),
        grid=(B, S // tc),
        in_specs=[
            pl.BlockSpec((None, tc, qk), lambda b, c: (b, c, 0)),
            pl.BlockSpec((None, tc, qk), lambda b, c: (b, c, 1)),
            pl.BlockSpec((None, tc, H * dv), lambda b, c: (b, c, 2 * qk // (H * dv))),
            pl.BlockSpec((None, tc, qk), lambda b, c: (b, c, 0)),
            st,
        ],
        out_specs=[pl.BlockSpec((None, tc, H * dv), lambda b, c: (b, c, 0)), st],
        out_shape=[
            jax.ShapeDtypeStruct((B, S, H * dv), F32),
            jax.ShapeDtypeStruct((B, H, dk, dv), F32),
        ],
        scratch_shapes=[
            pltpu.VMEM((H, dk, dv), F32),
            pltpu.VMEM((P, qk), F32), pltpu.VMEM((P, qk), F32), pltpu.VMEM((P, H * dv), F32), pltpu.VMEM((P, qk), F32),
            pltpu.VMEM((H, P, dk), F32), pltpu.VMEM((H, P, dk), F32),
        ],
        compiler_params=_cparams("parallel", "arbitrary"),
        name="gla",
    )(zg, zg, zg, lg, s0)


def _proj_dil_body(r, x_ref, g_ref, w_ref, hg_ref, cos_ref, sin_ref, o_ref, n_scr, z_scr):
    j = pl.program_id(1)

    @pl.when(j == 0)
    def _():
        n_scr[...] = _rms(x_ref[...], g_ref[...]).astype(BF16)

    z = _dot(n_scr[...], w_ref[...])

    @pl.when(j == 2)
    def _():
        for h in range(HEADS):
            z_scr[h] = z[:, h * HD:(h + 1) * HD]

    @pl.when(j != 2)
    def _():
        hg = jnp.where(j == 0, hg_ref[0:1, :], hg_ref[1:2, :])
        cs = cos_ref[...]
        sn = sin_ref[...]
        zs = [z[:, h * HD:(h + 1) * HD] for h in range(HEADS)]
        ms = [jnp.mean(zh * zh, axis=-1, keepdims=True) for zh in zs]
        ns = [zh * lax.rsqrt(m + EPS) * hg for zh, m in zip(zs, ms)]
        rs = [pltpu.roll(nh, HD // 2, 1) for nh in ns]
        for h in range(HEADS):
            z_scr[h] = ns[h] * cs + rs[h] * sn

    rows = z_scr.shape[1] // r
    for rho in range(r):
        for h in range(HEADS):
            zh = z_scr[h, pl.ds(rho, rows, stride=r), :] if r > 1 else z_scr[h]
            o_ref[rho, :, h * HD:(h + 1) * HD] = zh


def _proj_dil_group(x, g, w, hg, cosf, sinf, B, S, r, tm=1024):
    nt, d = x.shape
    assert S % tm == 0 and tm % (r * SUBLANES) == 0
    spb = S // tm
    return pl.pallas_call(
        functools.partial(_proj_dil_body, r),
        grid=(nt // tm, 3),
        in_specs=[
            pl.BlockSpec((tm, d), lambda i, j: (i, 0)),
            pl.BlockSpec((1, d), lambda i, j: (0, 0)),
            pl.BlockSpec((d, ATT_COLS), lambda i, j: (0, j)),
            pl.BlockSpec((2, HD), lambda i, j: (0, 0)),
            pl.BlockSpec((tm, HD), lambda i, j: (i, 0)),
            pl.BlockSpec((tm, HD), lambda i, j: (i, 0)),
        ],
        out_specs=pl.BlockSpec((None, r, tm // r, ATT_COLS), lambda i, j: (i // spb, 0, i % spb, j)),
        out_shape=jax.ShapeDtypeStruct((B, r, S // r, 3 * ATT_COLS), F32),
        scratch_shapes=[pltpu.VMEM((tm, d), BF16), pltpu.VMEM((HEADS, tm, HD), F32)],
        compiler_params=_cparams("parallel", "arbitrary"),
        name="proj_dil_r%d" % r,
    )(x, g.reshape(1, d), w, hg, cosf, sinf)


DIL_QBLOCKS = 2


def _dil_body(r, q_ref, kc_ref, kp_ref, vc_ref, vp_ref, o_ref, l_ref):
    i = pl.program_id(1)
    rho = pl.program_id(2)
    scale = HD ** -0.5
    QB = q_ref.shape[0]
    mq = lax.broadcasted_iota(jnp.int32, (QB, QB), 0)
    mk = lax.broadcasted_iota(jnp.int32, (QB, QB), 1)
    mask_c = jnp.logical_and(mk <= mq, mq - mk <= SPAN)
    pq = lax.broadcasted_iota(jnp.int32, (QB, SPAN), 0)
    pk = lax.broadcasted_iota(jnp.int32, (QB, SPAN), 1)
    mask_p = jnp.logical_and(pk >= pq, i > 0)
    sls = [slice(h * HD, (h + 1) * HD) for h in range(HEADS)]
    qs = [q_ref[:, sl].astype(BF16) for sl in sls]
    scs = [jnp.where(mask_c, _dot_nt(q, kc_ref[:, sl].astype(BF16)) * scale, NEG) for q, sl in zip(qs, sls)]
    sps = [jnp.where(mask_p, _dot_nt(q, kp_ref[:, sl].astype(BF16)) * scale, NEG) for q, sl in zip(qs, sls)]
    ms = [jnp.maximum(jnp.max(sc, axis=-1, keepdims=True), jnp.max(sp, axis=-1, keepdims=True))
          for sc, sp in zip(scs, sps)]
    ecs = [jnp.exp(sc - m) for sc, m in zip(scs, ms)]
    eps = [jnp.exp(sp - m) for sp, m in zip(sps, ms)]
    dens = [jnp.sum(ec, axis=-1, keepdims=True) + jnp.sum(ep, axis=-1, keepdims=True) for ec, ep in zip(ecs, eps)]
    for h, sl in enumerate(sls):
        o = (_dot((ecs[h] / dens[h]).astype(BF16), vc_ref[:, sl].astype(BF16))
             + _dot((eps[h] / dens[h]).astype(BF16), vp_ref[:, sl].astype(BF16)))
        lse = jnp.broadcast_to(ms[h] + jnp.log(dens[h]), (QB, HD))
        if r == 1:
            o_ref[h] = o
            l_ref[h] = lse
        else:
            o_ref[h, pl.ds(rho, QB, stride=r), :] = o
            l_ref[h, pl.ds(rho, QB, stride=r), :] = lse


def _dil_prompt(zr, S):
    B, r, L, _ = zr.shape
    nq = DIL_QBLOCKS
    QB = nq * SPAN
    assert L % QB == 0

    def cur(c):
        return pl.BlockSpec((None, None, QB, ATT_COLS), lambda b, i, p: (b, p, i, c))

    def prev(c):
        return pl.BlockSpec((None, None, SPAN, ATT_COLS), lambda b, i, p: (b, p, jnp.maximum(nq * i - 1, 0), c))

    out_spec = pl.BlockSpec((None, HEADS, QB * r, HD), lambda b, i, p: (b, 0, i, 0))
    return pl.pallas_call(
        functools.partial(_dil_body, r),
        grid=(B, L // QB, r),
        in_specs=[cur(0), cur(1), prev(1), cur(2), prev(2)],
        out_specs=[out_spec, out_spec],
        out_shape=[jax.ShapeDtypeStruct((B, HEADS, S, HD), F32)] * 2,
        compiler_params=_cparams("parallel", "parallel", "arbitrary"),
        name="dil_prompt_r%d" % r,
    )(zr, zr, zr, zr, zr)


def _dil_mix_body(o0, l0, o1, l1, o2, l2, od_ref):
    for h in range(HEADS):
        a0, a1, a2 = l0[h], l1[h], l2[h]
        m = jnp.maximum(jnp.maximum(a0, a1), a2)
        e0, e1, e2 = jnp.exp(a0 - m), jnp.exp(a1 - m), jnp.exp(a2 - m)
        den = e0 + e1 + e2
        od_ref[:, h * HD:(h + 1) * HD] = (e0 / den) * o0[h] + (e1 / den) * o1[h] + (e2 / den) * o2[h]


def _dil_mix(parts, tm=1024):
    B, _, S, _ = parts[0].shape
    spb = S // tm
    spec = pl.BlockSpec((None, HEADS, tm, HD), lambda i: (i // spb, 0, i % spb, 0))
    return pl.pallas_call(
        _dil_mix_body,
        grid=(B * spb,),
        in_specs=[spec] * 6,
        out_specs=pl.BlockSpec((tm, ATT_COLS), lambda i: (i, 0)),
        out_shape=jax.ShapeDtypeStruct((B * S, ATT_COLS), F32),
        compiler_params=_cparams("parallel"),
        name="dil_mix",
    )(*parts)


def _masked_attn(q, k, v, kn, vn, mask_b, mask_n, scale):
    sb = jnp.where(mask_b, _dot_nt(q, k) * scale, NEG)
    sn = jnp.where(mask_n, _dot_nt(q, kn) * scale, NEG)
    m = jnp.maximum(jnp.max(sb, axis=-1, keepdims=True), jnp.max(sn, axis=-1, keepdims=True))
    eb = jnp.exp(sb - m)
    en = jnp.exp(sn - m)
    den = jnp.sum(eb, axis=-1, keepdims=True) + jnp.sum(en, axis=-1, keepdims=True)
    o = _dot((eb / den).astype(BF16), v) + _dot((en / den).astype(BF16), vn)
    return o, m + jnp.log(den)


def _dil_dec_body(T, q_ref, kn_ref, vn_ref, k0, v0, k1, v1, k2, v2, od_ref, nk0, nv0, nk1, nv1, nk2, nv2):
    scale = HD ** -0.5
    R = T * HEADS
    caches = ((k0, v0, nk0, nv0), (k1, v1, nk1, nv1), (k2, v2, nk2, nv2))
    rn = lax.broadcasted_iota(jnp.int32, (R, R), 0)
    cn = lax.broadcasted_iota(jnp.int32, (R, R), 1)
    outs, lses = [], []
    for gi, (_, r) in enumerate(DIL_GROUPS):
        kc, vc, nkc, nvc = caches[gi]
        rows = kc.shape[0]
        for src, new, dst in ((kc, kn_ref, nkc), (vc, vn_ref, nvc)):
            dst[pl.ds(0, rows - R), :] = src[pl.ds(R, rows - R), :]
            dst[pl.ds(rows - R, R), :] = new[gi]
        if r <= T:
            n = rows
            kb = kc[...].astype(BF16)
            vb = vc[...].astype(BF16)
        else:
            n = rows // (r * HEADS) * R
            kb = kc[...].reshape(rows // (r * HEADS), r * HEADS, HD)[:, 0:R, :].reshape(n, HD).astype(BF16)
            vb = vc[...].reshape(rows // (r * HEADS), r * HEADS, HD)[:, 0:R, :].reshape(n, HD).astype(BF16)
        rb = lax.broadcasted_iota(jnp.int32, (R, n), 0)
        cb = lax.broadcasted_iota(jnp.int32, (R, n), 1)
        if r == 1:
            mask_b = jnp.logical_and(cb % HEADS == rb % HEADS, cb // HEADS >= rb // HEADS)
            mask_n = jnp.logical_and(cn % HEADS == rn % HEADS, cn // HEADS <= rn // HEADS)
        else:
            mask_b = cb % R == rb
            mask_n = cn == rn
        o, lse = _masked_attn(q_ref[gi].astype(BF16), kb, vb, kn_ref[gi].astype(BF16), vn_ref[gi].astype(BF16),
                              mask_b, mask_n, scale)
        outs.append(o)
        lses.append(lse)
    l0, l1, l2 = lses
    m = jnp.maximum(jnp.maximum(l0, l1), l2)
    e0, e1, e2 = jnp.exp(l0 - m), jnp.exp(l1 - m), jnp.exp(l2 - m)
    den = e0 + e1 + e2
    od_ref[...] = (e0 / den) * outs[0] + (e1 / den) * outs[1] + (e2 / den) * outs[2]


def _dil_decode(zd, caches):
    Bd, T, _ = zd.shape
    R = T * HEADS
    z6 = zd.reshape(Bd, T, N_DIL, 3, HEADS, HD)
    qkv = [jnp.transpose(z6[:, :, :, c], (0, 2, 1, 3, 4)).reshape(Bd, N_DIL, R, HD) for c in range(3)]
    new = pl.BlockSpec((None, N_DIL, R, HD), lambda b: (b, 0, 0, 0))
    args, specs = list(qkv), [new, new, new]
    out_specs = [pl.BlockSpec((None, R, HD), lambda b: (b, 0, 0))]
    out_shape = [jax.ShapeDtypeStruct((Bd, R, HD), F32)]
    for (wd, r), (kc, vc) in zip(DIL_GROUPS, caches):
        assert kc.shape[1] == wd and wd > T and R % SUBLANES == 0 and (r <= T or wd % r == 0)
        for a in (kc, vc):
            spec = pl.BlockSpec((None, wd * HEADS, HD), lambda b: (b, 0, 0))
            args.append(a.reshape(Bd, wd * HEADS, HD))
            specs.append(spec)
            out_specs.append(spec)
            out_shape.append(jax.ShapeDtypeStruct((Bd, wd * HEADS, HD), F32))
    od, *rolled = pl.pallas_call(
        functools.partial(_dil_dec_body, T),
        grid=(Bd,),
        in_specs=specs,
        out_specs=out_specs,
        out_shape=out_shape,
        compiler_params=_cparams("parallel"),
        name="dil_decode",
    )(*args)
    bufs = [(rolled[2 * gi].reshape(Bd, wd, HEADS, HD), rolled[2 * gi + 1].reshape(Bd, wd, HEADS, HD))
            for gi, (wd, _) in enumerate(DIL_GROUPS)]
    return od.reshape(Bd * T, ATT_COLS), bufs


def _mem_body(q_ref, k_ref, v_ref, o_ref):
    scale = HD ** -0.5
    for h in range(HEADS):
        sl = slice(h * HD, (h + 1) * HD)
        s = _dot_nt(q_ref[:, sl].astype(BF16), k_ref[:, sl].astype(BF16)) * scale
        e = jnp.exp(s - jnp.max(s, axis=-1, keepdims=True))
        p = e / jnp.sum(e, axis=-1, keepdims=True)
        o_ref[:, sl] = _dot(p.astype(BF16), v_ref[:, sl].astype(BF16))


def _mem_attn(qm, mkv, tq):
    B, S, _ = qm.shape
    M = mkv.shape[1]
    return pl.pallas_call(
        _mem_body,
        grid=(B, S // tq),
        in_specs=[
            pl.BlockSpec((None, tq, ATT_COLS), lambda b, i: (b, i, 0)),
            pl.BlockSpec((None, M, ATT_COLS), lambda b, i: (b, 0, 0)),
            pl.BlockSpec((None, M, ATT_COLS), lambda b, i: (b, 0, 1)),
        ],
        out_specs=pl.BlockSpec((None, tq, ATT_COLS), lambda b, i: (b, i, 0)),
        out_shape=jax.ShapeDtypeStruct((B, S, ATT_COLS), F32),
        compiler_params=_cparams("parallel", "arbitrary"),
        name="mem_attn",
    )(qm, mkv, mkv)


def _mem_dec_body(q_ref, k_ref, v_ref, o_ref):
    R, n = q_ref.shape[0], k_ref.shape[0]
    rb = lax.broadcasted_iota(jnp.int32, (R, n), 0)
    cb = lax.broadcasted_iota(jnp.int32, (R, n), 1)
    s = _dot_nt(q_ref[...].astype(BF16), k_ref[...].astype(BF16)) * (HD ** -0.5)
    s = jnp.where(cb % HEADS == rb % HEADS, s, -jnp.inf)
    e = jnp.exp(s - jnp.max(s, axis=-1, keepdims=True))
    p = e / jnp.sum(e, axis=-1, keepdims=True)
    o_ref[...] = _dot(p.astype(BF16), v_ref[...].astype(BF16))


def _mem_attn_decode(qm, mk, mv):
    Bd, T, _ = qm.shape
    M = mk.shape[1]
    R = T * HEADS
    kv = pl.BlockSpec((None, M * HEADS, HD), lambda b: (b, 0, 0))
    o = pl.pallas_call(
        _mem_dec_body,
        grid=(Bd,),
        in_specs=[pl.BlockSpec((None, R, HD), lambda b: (b, 0, 0)), kv, kv],
        out_specs=pl.BlockSpec((None, R, HD), lambda b: (b, 0, 0)),
        out_shape=jax.ShapeDtypeStruct((Bd, R, HD), F32),
        compiler_params=_cparams("parallel"),
        name="mem_attn_decode",
    )(qm.reshape(Bd, R, HD), mk.reshape(Bd, M * HEADS, HD), mv.reshape(Bd, M * HEADS, HD))
    return o.reshape(Bd * T, ATT_COLS)


def _merge_body(x_ref, og_ref, r_ref, od_ref, om_ref, gt_ref, ong_ref, wbg_ref, wbd_ref, wbm_ref, wo_ref,
                ln2_ref, wq_ref, h_ref, n2_ref, qh_ref):
    og = og_ref[...]
    r = r_ref[...]
    ong = ong_ref[...]
    parts = []
    for h in range(GLA_HEADS):
        sl = slice(h * GLA_DV, (h + 1) * GLA_DV)
        parts.append(_rms(og[:, sl], ong) * jax.nn.silu(r[:, sl]))
    br_gla = _dot(jnp.concatenate(parts, axis=-1).astype(BF16), wbg_ref[...])
    br_dil = _dot(od_ref[...].astype(BF16), wbd_ref[...])
    br_mem = _dot(om_ref[...].astype(BF16), wbm_ref[...])
    d = D_MODEL
    merged = gt_ref[:, 0:d] * br_gla + gt_ref[:, d:2 * d] * br_dil + gt_ref[:, 2 * d:3 * d] * br_mem
    hres = x_ref[...] + _dot(merged.astype(BF16), wo_ref[...])
    h_ref[...] = hres
    n2 = _rms(hres, ln2_ref[...]).astype(BF16)
    n2_ref[...] = n2
    qh_ref[...] = _dot(n2, wq_ref[...])


def _merge(x, og, zg, od, om, gt, ong, wbg, wbd, wbm, wo, ln2, wq, tm=256):
    nt, d = x.shape
    tm = min(tm, nt)
    nq = wq.shape[1]
    rblk = (2 * GLA_HEADS * GLA_DK + GLA_HEADS * GLA_DV) // d

    def tok(cols, blk=0):
        return pl.BlockSpec((tm, cols), lambda i: (i, blk))

    def const(shape):
        return pl.BlockSpec(shape, lambda i: (0, 0), pipeline_mode=pl.Buffered(1))

    return pl.pallas_call(
        _merge_body,
        grid=(nt // tm,),
        in_specs=[
            tok(d), tok(d), tok(d, rblk), tok(ATT_COLS), tok(ATT_COLS), tok(3 * d),
            const((1, GLA_DV)), const(wbg.shape), const(wbd.shape), const(wbm.shape), const(wo.shape),
            const((1, d)), const(wq.shape),
        ],
        out_specs=[tok(d), tok(d), tok(nq)],
        out_shape=[
            jax.ShapeDtypeStruct((nt, d), F32),
            jax.ShapeDtypeStruct((nt, d), BF16),
            jax.ShapeDtypeStruct((nt, nq), F32),
        ],
        compiler_params=_cparams("parallel"),
        name="merge",
    )(x, og, zg, od, om, gt, ong.reshape(1, GLA_DV), wbg, wbd, wbm, wo, ln2.reshape(1, d), wq)


PEER_CAND_GROUPS = 10


def _peer_cand_layout(tp):
    r = lax.broadcasted_iota(jnp.int32, (PEER_CAND_GROUPS * SUBLANES, tp), 0)
    grp = r // SUBLANES
    p = r % SUBLANES
    K = PEER_TOPK
    k1 = jnp.where(grp < 2, 0, jnp.where(grp < 9, grp - 1, SUBLANES + p))
    k2 = jnp.where(grp < 2, r, jnp.where(grp < 9, p, 0))
    valid = (k1 + 1) * (k2 + 1) <= K
    return (k1 * K + k2).astype(F32), valid


def _peer_cands(x1, x2):
    top, bot = x2[0:SUBLANES], x2[SUBLANES:2 * SUBLANES]
    g1 = [jnp.broadcast_to(x1[0:1], top.shape)] * 2 + [jnp.broadcast_to(x1[k:k + 1], top.shape) for k in range(1, 8)]
    g1.append(x1[SUBLANES:2 * SUBLANES])
    g2 = [top, bot] + [top] * 7 + [jnp.broadcast_to(x2[0:1], top.shape)]
    return jnp.concatenate(g1, axis=0), jnp.concatenate(g2, axis=0)


def _peer_topk_body(qh_ref, keys_ref, a_ref, b_ref, gw_ref, s_scr, sv_scr, si_scr, cand_scr, eid_scr, sc_scr, ev_scr):
    tp = qh_ref.shape[0]
    K = PEER_TOPK
    NK = PEER_NKEYS
    NH = PEER_HEADS
    rowf = lax.broadcasted_iota(jnp.int32, (NK, tp), 0).astype(F32)
    pos, valid = _peer_cand_layout(tp)
    for i in range(2 * NH):
        s_scr[i] = _dot_nt(keys_ref[i], qh_ref[:, i * LANES:(i + 1) * LANES].astype(BF16))

    def step1(k, carry):
        for i in range(2 * NH):
            s = s_scr[i]
            mx = jnp.max(s, axis=0, keepdims=True)
            idx = jnp.min(jnp.where(s == mx, rowf, float(NK)), axis=0, keepdims=True)
            s_scr[i] = jnp.where(rowf == idx, -jnp.inf, s)
            sv_scr[i, pl.ds(k, 1), :] = mx
            si_scr[i, pl.ds(k, 1), :] = idx
        return carry

    lax.fori_loop(0, K, step1, 0)

    for h in range(NH):
        c1, c2 = _peer_cands(sv_scr[2 * h], sv_scr[2 * h + 1])
        e1, e2 = _peer_cands(si_scr[2 * h], si_scr[2 * h + 1])
        cand_scr[h] = jnp.where(valid, c1 + c2, -jnp.inf)
        eid_scr[h] = e1 * float(NK) + e2

    def step2(k, carry):
        for h in range(NH):
            cand = cand_scr[h]
            mx = jnp.max(cand, axis=0, keepdims=True)
            first = jnp.min(jnp.where(cand == mx, pos, float(K * K)), axis=0, keepdims=True)
            hit = pos == first
            ev_scr[h, pl.ds(k, 1), :] = jnp.max(jnp.where(hit, eid_scr[h], -1.0), axis=0, keepdims=True)
            sc_scr[h, pl.ds(k, 1), :] = mx
            cand_scr[h] = jnp.where(hit, -jnp.inf, cand)
        return carry

    lax.fori_loop(0, K, step2, 0)

    a_rows, b_rows, g_rows = [], [], []
    for h in range(NH):
        sc, ev = sc_scr[h], ev_scr[h]
        e = jnp.exp(sc - jnp.max(sc, axis=0, keepdims=True))
        g_rows.append(e / jnp.sum(e, axis=0, keepdims=True))
        a = jnp.floor(ev * (1.0 / NK))
        a_rows.append(a)
        b_rows.append(ev - a * float(NK))
    a_ref[...] = jnp.transpose(jnp.concatenate(a_rows, axis=0))
    b_ref[...] = jnp.transpose(jnp.concatenate(b_rows, axis=0))
    gw_ref[...] = jnp.transpose(jnp.concatenate(g_rows, axis=0))


def _peer_topk(qh, keys, tp=128):
    nt, nq = qh.shape
    tp = min(tp, nt)
    spec = pl.BlockSpec((tp, LANES), lambda i: (i, 0))
    shp = jax.ShapeDtypeStruct((nt, LANES), F32)
    K, NH = PEER_TOPK, PEER_HEADS
    ncand = PEER_CAND_GROUPS * SUBLANES
    return pl.pallas_call(
        _peer_topk_body,
        grid=(nt // tp,),
        in_specs=[
            pl.BlockSpec((tp, nq), lambda i: (i, 0)),
            pl.BlockSpec(keys.shape, lambda i: (0, 0, 0)),
        ],
        out_specs=[spec, spec, spec],
        out_shape=[shp, shp, shp],
        scratch_shapes=[
            pltpu.VMEM((2 * NH, PEER_NKEYS, tp), F32), pltpu.VMEM((2 * NH, K, tp), F32), pltpu.VMEM((2 * NH, K, tp), F32),
            pltpu.VMEM((NH, ncand, tp), F32), pltpu.VMEM((NH, ncand, tp), F32),
            pltpu.VMEM((NH, K, tp), F32), pltpu.VMEM((NH, K, tp), F32),
        ],
        compiler_params=_cparams("parallel"),
        name="peer_topk",
    )(qh, keys)


def _peer_w_body(a_ref, b_ref, gw_ref, w_ref, scr):
    tw = a_ref.shape[0]
    NK = PEER_NKEYS
    stride = scr.shape[0] // NK
    sub = lax.broadcasted_iota(jnp.int32, (NK, LANES), 0).astype(F32)

    zero = jnp.zeros((NK, LANES), BF16)

    def tok2(p, carry):
        oas, gbs = [], []
        for t in (2 * p, 2 * p + 1):
            arow = a_ref[pl.ds(t, 1), :]
            brow = b_ref[pl.ds(t, 1), :]
            grow = gw_ref[pl.ds(t, 1), :]
            oas.append((arow == sub).astype(BF16))
            gbs.append(jnp.where(brow == sub, grow, 0.0).astype(BF16))
        lhs = jnp.concatenate(oas, axis=1)
        rhs = jnp.concatenate([jnp.concatenate([gbs[0], zero], axis=1),
                               jnp.concatenate([zero, gbs[1]], axis=1)], axis=0)
        w2 = _dot_nt(lhs, rhs)
        scr[pl.ds(2 * p, NK, stride=stride), :] = w2[:, 0:LANES]
        scr[pl.ds(2 * p + 1, NK, stride=stride), :] = w2[:, LANES:2 * LANES]
        return carry

    lax.fori_loop(0, tw // 2, tok2, 0, unroll=32)
    for i1 in range(NK):
        w_ref[:, i1 * NK:(i1 + 1) * NK] = scr[i1 * stride:i1 * stride + tw, :].astype(BF16)


def _peer_weights(a, b, gw, n_experts, tw=128):
    nt = a.shape[0]
    tw = min(tw, nt)
    stride = tw + SUBLANES
    tok = pl.BlockSpec((tw, LANES), lambda i: (i, 0))
    return pl.pallas_call(
        _peer_w_body,
        grid=(nt // tw,),
        in_specs=[tok, tok, tok],
        out_specs=pl.BlockSpec((tw, n_experts), lambda i: (i, 0)),
        out_shape=jax.ShapeDtypeStruct((nt, n_experts), BF16),
        scratch_shapes=[pltpu.VMEM((PEER_NKEYS * stride, LANES), F32)],
        compiler_params=_cparams("parallel"),
        name="peer_weights",
    )(a, b, gw)


def _peer_mix_body(n2_ref, w_ref, h_ref, u_ref, v_ref, y_ref, acc):
    j = pl.program_id(1)

    @pl.when(j == 0)
    def _():
        acc[...] = jnp.zeros(acc.shape, F32)

    hpre = _dot_nt(n2_ref[...], u_ref[...])
    gelu = 0.5 * hpre * (1.0 + lax.erf(hpre * (2.0 ** -0.5)))
    act = (gelu * w_ref[...].astype(F32)).astype(BF16)
    acc[...] += _dot(act, v_ref[...])

    @pl.when(j == pl.num_programs(1) - 1)
    def _():
        y_ref[...] = h_ref[...] + acc[...]


def _peer_mix(n2, w, h, u, v):
    nt, d = n2.shape
    T = min(PEER_T, nt)
    assert nt % T == 0 and u.shape[0] % PEER_EB == 0
    tok = pl.BlockSpec((T, d), lambda i, j: (i, 0))
    tab = pl.BlockSpec((PEER_EB, d), lambda i, j: (j, 0))
    return pl.pallas_call(
        _peer_mix_body,
        grid=(nt // T, u.shape[0] // PEER_EB),
        in_specs=[tok, pl.BlockSpec((T, PEER_EB), lambda i, j: (i, j)), tok, tab, tab],
        out_specs=tok,
        out_shape=jax.ShapeDtypeStruct((nt, d), F32),
        scratch_shapes=[pltpu.VMEM((T, d), F32)],
        compiler_params=_cparams("parallel", "arbitrary"),
        name="peer_mix",
    )(n2, w, h, u, v)


def _rope_tables(pos):
    half = HD // 2
    inv = ROPE_THETA ** (-jnp.arange(half, dtype=F32) / half)
    ang = pos.astype(F32)[:, None] * inv[None, :]
    cos, sin = jnp.cos(ang), jnp.sin(ang)
    return jnp.concatenate([cos, cos], axis=-1), jnp.concatenate([-sin, sin], axis=-1)


def _prep_weights(w_in, gla_wg2, peer_wq, peer_keys, peer_u, peer_v, w_br_gla, w_br_dil, w_br_mem, w_out):
    c0 = GLA_COLS
    c1 = c0 + GLA_RANK
    c2 = c1 + DIL_COLS
    c3 = c2 + ATT_COLS
    wa = jnp.pad(w_in[:, c0:c1], ((0, 0), (0, LANES - GLA_RANK))).astype(BF16)
    wg2 = jnp.pad(gla_wg2, ((0, LANES - GLA_RANK), (0, 0))).astype(BF16)
    return dict(
        w_gla=w_in[:, :c0].astype(BF16), wa=wa, wg2=wg2,
        w_dil=w_in[:, c1:c2].astype(BF16), w_qm=w_in[:, c2:c3].astype(BF16), w_gt=w_in[:, c3:].astype(BF16),
        wbg=w_br_gla.astype(BF16), wbd=w_br_dil.astype(BF16), wbm=w_br_mem.astype(BF16), wo=w_out.astype(BF16),
        wq=peer_wq.astype(BF16),
        keys=peer_keys.reshape(PEER_HEADS * 2, PEER_NKEYS, LANES).astype(BF16),
        u=peer_u.astype(BF16), v=peer_v.astype(BF16),
    )


def _layer(x, pos, mem_k, mem_v, s0, caches, p, W):
    B, S, d = x.shape
    nt = B * S
    xt = x.reshape(nt, d)
    ln1 = p["ln1_g"]
    zg = _proj("gla", xt, ln1, W["w_gla"])
    lg = _gla_gate(xt, ln1, W["wa"], W["wg2"], p["gla_bg"])
    cosf, sinf = _rope_tables(pos)
    cosf = jnp.broadcast_to(cosf[None], (B, S, HD)).reshape(nt, HD)
    sinf = jnp.broadcast_to(sinf[None], (B, S, HD)).reshape(nt, HD)
    hg = jnp.stack([p["dil_qn_g"], p["dil_kn_g"]])
    qm = _proj("mem_q", xt, ln1, W["w_qm"], extra=(p["mem_qn_g"].reshape(1, HD),))
    gt = _proj("sigmoid", xt, ln1, W["w_gt"])

    prompt = caches is None
    og, s_fin = _gla(zg.reshape(B, S, GLA_COLS), lg.reshape(B, S, GLA_HEADS * GLA_DK), s0,
                     GLA_CHUNK if prompt else S)

    new_bufs = []
    gcols = 3 * ATT_COLS
    if prompt:
        parts = []
        for gi, (wd, r) in enumerate(DIL_GROUPS):
            zr = _proj_dil_group(xt, ln1, W["w_dil"][:, gi * gcols:(gi + 1) * gcols], hg, cosf, sinf, B, S, r)
            parts.extend(_dil_prompt(zr, S))
            keep = min(wd, S)
            kv = zr[:, :, (S - keep) // r:, ATT_COLS:]
            kv = jnp.swapaxes(kv, 1, 2).reshape(B, keep, 2, HEADS, HD)
            new_bufs.append((kv[:, :, 0], kv[:, :, 1]))
        od = _dil_mix(parts)
    else:
        zd3 = _proj("dil", xt, ln1, W["w_dil"], extra=(hg, cosf, sinf)).reshape(B, S, DIL_COLS)
        od, new_bufs = _dil_decode(zd3, caches)

    qm3 = qm.reshape(B, S, ATT_COLS)
    if prompt:
        om = _mem_attn(qm3, mem_k, 512).reshape(nt, ATT_COLS)
    else:
        om = _mem_attn_decode(qm3, mem_k, mem_v)

    h, n2, qh = _merge(xt, og.reshape(nt, d), zg, od, om, gt, p["gla_onorm_g"], W["wbg"], W["wbd"], W["wbm"],
                       W["wo"], p["ln2_g"], W["wq"])
    a, b, gw = _peer_topk(qh, W["keys"])
    y = _peer_mix(n2, _peer_weights(a, b, gw, W["u"].shape[0]), h, W["u"], W["v"])
    return y.reshape(B, S, d), s_fin, new_bufs


def kernel(x_prompt, x_sample, mem_prompt, state_gla, cache_dil_k0, cache_dil_v0, cache_dil_k1, cache_dil_v1, cache_dil_k2, cache_dil_v2, cache_mem_k, cache_mem_v, ln1_g, w_in, gla_wg2, gla_bg, gla_onorm_g, dil_qn_g, dil_kn_g, mem_norm_g, w_mem_kv, mem_qn_g, mem_kn_g, w_br_gla, w_br_dil, w_br_mem, w_out, ln2_g, peer_wq, peer_keys, peer_u, peer_v):
    B, S, d = x_prompt.shape
    Bd, T, _ = x_sample.shape
    p = dict(ln1_g=ln1_g, gla_bg=gla_bg, gla_onorm_g=gla_onorm_g, dil_qn_g=dil_qn_g, dil_kn_g=dil_kn_g,
             mem_qn_g=mem_qn_g, ln2_g=ln2_g)
    W = _prep_weights(w_in, gla_wg2, peer_wq, peer_keys, peer_u, peer_v, w_br_gla, w_br_dil, w_br_mem, w_out)

    M = mem_prompt.shape[1]
    mkv = _proj("mem_kv", mem_prompt.reshape(B * M, d), mem_norm_g, w_mem_kv.astype(BF16),
                extra=(mem_kn_g.reshape(1, HD),))
    mem_k_p = mkv[:, :ATT_COLS].reshape(B, M, HEADS, HD)
    mem_v_p = mkv[:, ATT_COLS:].reshape(B, M, HEADS, HD)

    s0 = jnp.zeros((B, GLA_HEADS, GLA_DK, GLA_DV), F32)
    mkv3 = mkv.reshape(B, M, 2 * ATT_COLS)
    y_prompt, gla_state_p, bufs_p = _layer(x_prompt, jnp.arange(S, dtype=jnp.int32), mkv3, mkv3, s0, None, p, W)

    caches = ((cache_dil_k0, cache_dil_v0), (cache_dil_k1, cache_dil_v1), (cache_dil_k2, cache_dil_v2))
    pos_s = PAST_LEN + jnp.arange(T, dtype=jnp.int32)
    y_sample, gla_state_s, bufs_s = _layer(x_sample, pos_s, cache_mem_k, cache_mem_v, state_gla, caches, p, W)

    (dk0_p, dv0_p), (dk1_p, dv1_p), (dk2_p, dv2_p) = bufs_p
    (dk0_s, dv0_s), (dk1_s, dv1_s), (dk2_s, dv2_s) = bufs_s
    return (y_prompt, y_sample,
            gla_state_p, dk0_p, dv0_p, dk1_p, dv1_p, dk2_p, dv2_p, mem_k_p, mem_v_p,
            gla_state_s, dk0_s, dv0_s, dk1_s, dv1_s, dk2_s, dv2_s)
```

```python
import functools

import jax
import jax.numpy as jnp
from jax import lax
from jax.experimental import pallas as pl
from jax.experimental.pallas import tpu as pltpu

F32 = jnp.float32
BF16 = jnp.bfloat16

D_MODEL = 1024
PAST_LEN = 8192
GLA_HEADS = 4
GLA_DK = 128
GLA_DV = 256
GLA_RANK = 16
GLA_TAU = 16.0
DIL_GROUPS = ((128, 1), (512, 4), (2048, 16))
N_DIL = 3
HEADS = 4
HD = 128
SPAN = 128
MEM_LEN = 256
ROPE_THETA = 10000.0
PEER_HEADS = 8
PEER_NKEYS = 128
PEER_TOPK = 16
EPS = 1e-6
NEG = -1e30

LANES = 128
SUBLANES = 8
GLA_COLS = 2 * GLA_HEADS * GLA_DK + 2 * GLA_HEADS * GLA_DV
DIL_COLS = N_DIL * 3 * HEADS * HD
ATT_COLS = HEADS * HD
VMEM_LIMIT = 56 * 1024 * 1024

PEER_T = 1024
PEER_EB = 1024


def _cparams(*sem):
    return pltpu.CompilerParams(dimension_semantics=sem, vmem_limit_bytes=VMEM_LIMIT)


def _rms(x, g):
    return x * lax.rsqrt(jnp.mean(x * x, axis=-1, keepdims=True) + EPS) * g


def _dot(a, b):
    return jnp.dot(a, b, preferred_element_type=F32)


def _dot_nt(a, b):
    return lax.dot_general(a, b, (((1,), (1,)), ((), ())), preferred_element_type=F32)


def _split3(x):
    hi = x.astype(BF16)
    r1 = x - hi.astype(F32)
    mid = r1.astype(BF16)
    lo = (r1 - mid.astype(F32)).astype(BF16)
    return hi, mid, lo


def _norm_body(x_ref, g_ref, n_ref):
    n_ref[...] = _rms(x_ref[...], g_ref[...]).astype(BF16)


def _norm(x, g, tm=1024):
    nt, d = x.shape
    tm = min(tm, nt)
    return pl.pallas_call(
        _norm_body,
        grid=(nt // tm,),
        in_specs=[pl.BlockSpec((tm, d), lambda i: (i, 0)), pl.BlockSpec((1, d), lambda i: (0, 0))],
        out_specs=pl.BlockSpec((tm, d), lambda i: (i, 0)),
        out_shape=jax.ShapeDtypeStruct((nt, d), BF16),
        compiler_params=_cparams("parallel"),
        name="norm",
    )(x, g.reshape(1, d))


def _proj_body(kind, *refs):
    if kind == "dil":
        n_ref, w_ref, hg_ref, cos_ref, sin_ref, o_ref = refs
    elif kind in ("mem_q", "mem_kv"):
        n_ref, w_ref, hg_ref, o_ref = refs
    else:
        n_ref, w_ref, o_ref = refs
    j = pl.program_id(1)
    z = _dot(n_ref[...], w_ref[...])
    if kind == "gla":
        col = lax.broadcasted_iota(jnp.int32, (1, z.shape[1]), 1) + j * z.shape[1]
        o_ref[...] = z * jnp.where(col < GLA_HEADS * GLA_DK, GLA_DK ** -0.5, 1.0).astype(F32)
    elif kind == "sigmoid":
        o_ref[...] = jax.nn.sigmoid(z)
    elif kind == "mem_q":
        hg = hg_ref[...]
        o_ref[...] = jnp.concatenate([_rms(z[:, h * HD:(h + 1) * HD], hg) for h in range(HEADS)], axis=-1)
    elif kind == "mem_kv":
        @pl.when(j == 0)
        def _():
            hg = hg_ref[...]
            o_ref[...] = jnp.concatenate([_rms(z[:, h * HD:(h + 1) * HD], hg) for h in range(HEADS)], axis=-1)

        @pl.when(j != 0)
        def _():
            o_ref[...] = z
    elif kind == "dil":
        c = j % 3

        @pl.when(c == 2)
        def _():
            o_ref[...] = z

        @pl.when(c != 2)
        def _():
            hg = jnp.where(c == 0, hg_ref[0:1, :], hg_ref[1:2, :])
            cs = cos_ref[...]
            sn = sin_ref[...]
            parts = []
            for h in range(HEADS):
                zh = _rms(z[:, h * HD:(h + 1) * HD], hg)
                parts.append(zh * cs + pltpu.roll(zh, HD // 2, 1) * sn)
            o_ref[...] = jnp.concatenate(parts, axis=-1)
    else:
        raise ValueError(kind)


def _proj(kind, n, w, extra=(), tm=1024, tn=1024):
    nt, d = n.shape
    ncol = w.shape[1]
    tm = min(tm, nt)
    if kind in ("dil", "mem_q", "mem_kv"):
        tn = ATT_COLS
    assert nt % tm == 0 and ncol % tn == 0
    in_specs = [
        pl.BlockSpec((tm, d), lambda i, j: (i, 0)),
        pl.BlockSpec((d, tn), lambda i, j: (0, j)),
    ]
    if kind == "dil":
        in_specs += [
            pl.BlockSpec((2, HD), lambda i, j: (0, 0)),
            pl.BlockSpec((tm, HD), lambda i, j: (i, 0)),
            pl.BlockSpec((tm, HD), lambda i, j: (i, 0)),
        ]
    elif kind in ("mem_q", "mem_kv"):
        in_specs += [pl.BlockSpec((1, HD), lambda i, j: (0, 0))]
    return pl.pallas_call(
        functools.partial(_proj_body, kind),
        grid=(nt // tm, ncol // tn),
        in_specs=in_specs,
        out_specs=pl.BlockSpec((tm, tn), lambda i, j: (i, j)),
        out_shape=jax.ShapeDtypeStruct((nt, ncol), F32),
        compiler_params=_cparams("parallel", "arbitrary"),
        name="proj_" + kind,
    )(n, w, *extra)


def _gate_body(n_ref, wa_ref, wg2_ref, bg_ref, o_ref):
    a = _dot(n_ref[...], wa_ref[...])
    pre = _dot(a.astype(BF16), wg2_ref[...]) + bg_ref[...]
    o_ref[...] = jax.nn.log_sigmoid(pre) / GLA_TAU


def _gla_gate(n, wa, wg2, bg, tm=1024):
    nt, d = n.shape
    tm = min(tm, nt)
    ncol = wg2.shape[1]
    return pl.pallas_call(
        _gate_body,
        grid=(nt // tm,),
        in_specs=[
            pl.BlockSpec((tm, d), lambda i: (i, 0)),
            pl.BlockSpec((d, LANES), lambda i: (0, 0)),
            pl.BlockSpec((LANES, ncol), lambda i: (0, 0)),
            pl.BlockSpec((1, ncol), lambda i: (0, 0)),
        ],
        out_specs=pl.BlockSpec((tm, ncol), lambda i: (i, 0)),
        out_shape=jax.ShapeDtypeStruct((nt, ncol), F32),
        compiler_params=_cparams("parallel"),
        name="gla_gate",
    )(n, wa, wg2, bg.reshape(1, ncol))


GLA_CHUNK = 128
GLA_SUB = 16


def _gla_intra(q, k, b, qs_scr, bs_scr, h, n_sub):
    P, SB = GLA_CHUNK, GLA_SUB
    rowp = lax.broadcasted_iota(jnp.int32, (P, 1), 0)
    qparts, kparts = [], []
    for j in range(n_sub - 1):
        lo, hi = SB * j, SB * (j + 1)
        be = b[hi - 1:hi, :]
        qparts.append(jnp.where(rowp >= hi, q * jnp.exp(b - be), 0.0).astype(BF16))
        kblk = (k[lo:hi] * jnp.exp(be - b[lo:hi])).astype(BF16)
        pieces = [kblk]
        if lo:
            pieces.insert(0, jnp.zeros((lo, k.shape[1]), BF16))
        pieces.append(jnp.zeros((P - hi, k.shape[1]), BF16))
        kparts.append(jnp.concatenate(pieces, axis=0))
    if qparts:
        a_off = _dot_nt(jnp.concatenate(qparts, axis=1), jnp.concatenate(kparts, axis=1))
    else:
        a_off = jnp.zeros((P, P), F32)
    HS = SB // 2
    lane = lax.broadcasted_iota(jnp.int32, (HS, P), 1)
    srow = lax.broadcasted_iota(jnp.int32, (HS, P), 0)
    tiles = []
    for j in range(P // SB):
        if j >= n_sub:
            tiles.append(jnp.zeros((SB, P), F32))
            continue
        lo, mid, hi = SB * j, SB * j + HS, SB * (j + 1)
        halves = []
        for s0 in (lo, mid):
            kj, bj = k[s0:s0 + HS], b[s0:s0 + HS]
            at = jnp.zeros((HS, P), F32)
            for t in range(s0, hi):
                bt = bs_scr[h, pl.ds(t, 1), :]
                qt = qs_scr[h, pl.ds(t, 1), :]
                colv = jnp.sum(qt * kj * jnp.exp(jnp.minimum(bt - bj, 0.0)), axis=-1, keepdims=True)
                at = jnp.where(lane == t, colv, at)
            halves.append(jnp.where(srow + s0 <= lane, at, 0.0))
        tiles.extend(halves)
    return a_off + jnp.transpose(jnp.concatenate(tiles, axis=0))


def _gla_body(tc, q_ref, k_ref, v_ref, g_ref, s0_ref, o_ref, sf_ref, s_scr, qp, kp, vp, gp, qs_scr, bs_scr):
    c = pl.program_id(1)
    P = GLA_CHUNK
    H, dk, dv = GLA_HEADS, GLA_DK, GLA_DV
    n_sub = -(-tc // GLA_SUB)

    @pl.when(c == 0)
    def _():
        s_scr[...] = s0_ref[...]

    if tc == P:
        q, k, v, g = q_ref[...], k_ref[...], v_ref[...], g_ref[...]
    else:
        for pad, ref in ((qp, q_ref), (kp, k_ref), (vp, v_ref), (gp, g_ref)):
            pad[...] = jnp.zeros(pad.shape, F32)
            pad[0:tc, :] = ref[...]
        q, k, v, g = qp[...], kp[...], vp[...], gp[...]

    row = lax.broadcasted_iota(jnp.int32, (P, P), 0)
    col = lax.broadcasted_iota(jnp.int32, (P, P), 1)
    trib = (row >= col).astype(BF16)
    g1, g2, g3 = _split3(g)
    b = _dot(trib, g1) + _dot(trib, g2) + _dot(trib, g3)
    b_end = b[P - 1:P, :]
    qe = (q * jnp.exp(b)).astype(BF16)
    kd = k * jnp.exp(b_end - b)
    vb = v.astype(BF16)
    for h in range(H):
        ks = slice(h * dk, (h + 1) * dk)
        qs_scr[h] = q[:, ks]
        bs_scr[h] = b[:, ks]
    for h in range(H):
        ks = slice(h * dk, (h + 1) * dk)
        vs = slice(h * dv, (h + 1) * dv)
        a = _gla_intra(q[:, ks], k[:, ks], b[:, ks], qs_scr, bs_scr, h, n_sub)
        s = s_scr[h]
        o = _dot(qe[:, ks], s.astype(BF16)) + _dot(a.astype(BF16), vb[:, vs])
        o_ref[:, vs] = o[0:tc, :]
        decay = jnp.exp(jnp.sum(jnp.transpose(g[:, ks]), axis=1, keepdims=True))
        s_new = decay * s + _dot(jnp.transpose(kd[:, ks]).astype(BF16), vb[:, vs])
        s_scr[h] = s_new

        @pl.when(c == pl.num_programs(1) - 1)
        def _():
            sf_ref[h] = s_new


def _gla(zg, lg, s0, tc):
    B, S, _ = zg.shape
    H, dk, dv = GLA_HEADS, GLA_DK, GLA_DV
    assert S % tc == 0
    P = GLA_CHUNK
    qk = H * dk
    st = pl.BlockSpec((None, H, dk, dv), lambda b, c: (b, 0, 0, 0))
    return pl.pallas_call(
        functools.partial(_gla_body, tc),
        grid=(B, S // tc),
        in_specs=[
            pl.BlockSpec((None, tc, qk), lambda b, c: (b, c, 0)),
            pl.BlockSpec((None, tc, qk), lambda b, c: (b, c, 1)),
            pl.BlockSpec((None, tc, H * dv), lambda b, c: (b, c, 2 * qk // (H * dv))),
            pl.BlockSpec((None, tc, qk), lambda b, c: (b, c, 0)),
            st,
        ],
        out_specs=[pl.BlockSpec((None, tc, H * dv), lambda b, c: (b, c, 0)), st],
        out_shape=[
            jax.ShapeDtypeStruct((B, S, H * dv), F32),
            jax.ShapeDtypeStruct((B, H, dk, dv), F32),
        ],
        scratch_shapes=[
            pltpu.VMEM((H, dk, dv), F32),
            pltpu.VMEM((P, qk), F32), pltpu.VMEM((P, qk), F32), pltpu.VMEM((P, H * dv), F32), pltpu.VMEM((P, qk), F32),
            pltpu.VMEM((H, P, dk), F32), pltpu.VMEM((H, P, dk), F32),
        ],
        compiler_params=_cparams("parallel", "arbitrary"),
        name="gla",
    )(zg, zg, zg, lg, s0)


def _proj_dil_body(r, n_ref, w_ref, hg_ref, cos_ref, sin_ref, o_ref, z_scr):
    j = pl.program_id(1)
    z = _dot(n_ref[...], w_ref[...])

    @pl.when(j == 2)
    def _():
        for h in range(HEADS):
            z_scr[h] = z[:, h * HD:(h + 1) * HD]

    @pl.when(j != 2)
    def _():
        hg = jnp.where(j == 0, hg_ref[0:1, :], hg_ref[1:2, :])
        cs = cos_ref[...]
        sn = sin_ref[...]
        zs = [z[:, h * HD:(h + 1) * HD] for h in range(HEADS)]
        ms = [jnp.mean(zh * zh, axis=-1, keepdims=True) for zh in zs]
        ns = [zh * lax.rsqrt(m + EPS) * hg for zh, m in zip(zs, ms)]
        rs = [pltpu.roll(nh, HD // 2, 1) for nh in ns]
        for h in range(HEADS):
            z_scr[h] = ns[h] * cs + rs[h] * sn

    rows = z_scr.shape[1] // r
    for rho in range(r):
        for h in range(HEADS):
            zh = z_scr[h, pl.ds(rho, rows, stride=r), :] if r > 1 else z_scr[h]
            o_ref[rho, :, h * HD:(h + 1) * HD] = zh


def _proj_dil_group(n, w, hg, cosf, sinf, B, S, r, tm=1024):
    nt, d = n.shape
    assert S % tm == 0 and tm % (r * SUBLANES) == 0
    spb = S // tm
    return pl.pallas_call(
        functools.partial(_proj_dil_body, r),
        grid=(nt // tm, 3),
        in_specs=[
            pl.BlockSpec((tm, d), lambda i, j: (i, 0)),
            pl.BlockSpec((d, ATT_COLS), lambda i, j: (0, j)),
            pl.BlockSpec((2, HD), lambda i, j: (0, 0)),
            pl.BlockSpec((tm, HD), lambda i, j: (i, 0)),
            pl.BlockSpec((tm, HD), lambda i, j: (i, 0)),
        ],
        out_specs=pl.BlockSpec((None, r, tm // r, ATT_COLS), lambda i, j: (i // spb, 0, i % spb, j)),
        out_shape=jax.ShapeDtypeStruct((B, r, S // r, 3 * ATT_COLS), F32),
        scratch_shapes=[pltpu.VMEM((HEADS, tm, HD), F32)],
        compiler_params=_cparams("parallel", "arbitrary"),
        name="proj_dil_r%d" % r,
    )(n, w, hg, cosf, sinf)


DIL_QBLOCKS = 2


def _dil_body(r, q_ref, kc_ref, kp_ref, vc_ref, vp_ref, o_ref, l_ref):
    i = pl.program_id(1)
    rho = pl.program_id(2)
    scale = HD ** -0.5
    QB = q_ref.shape[0]
    mq = lax.broadcasted_iota(jnp.int32, (QB, QB), 0)
    mk = lax.broadcasted_iota(jnp.int32, (QB, QB), 1)
    mask_c = jnp.logical_and(mk <= mq, mq - mk <= SPAN)
    pq = lax.broadcasted_iota(jnp.int32, (QB, SPAN), 0)
    pk = lax.broadcasted_iota(jnp.int32, (QB, SPAN), 1)
    mask_p = jnp.logical_and(pk >= pq, i > 0)
    sls = [slice(h * HD, (h + 1) * HD) for h in range(HEADS)]
    qs = [q_ref[:, sl].astype(BF16) for sl in sls]
    scs = [jnp.where(mask_c, _dot_nt(q, kc_ref[:, sl].astype(BF16)) * scale, NEG) for q, sl in zip(qs, sls)]
    sps = [jnp.where(mask_p, _dot_nt(q, kp_ref[:, sl].astype(BF16)) * scale, NEG) for q, sl in zip(qs, sls)]
    ms = [jnp.maximum(jnp.max(sc, axis=-1, keepdims=True), jnp.max(sp, axis=-1, keepdims=True))
          for sc, sp in zip(scs, sps)]
    ecs = [jnp.exp(sc - m) for sc, m in zip(scs, ms)]
    eps = [jnp.exp(sp - m) for sp, m in zip(sps, ms)]
    dens = [jnp.sum(ec, axis=-1, keepdims=True) + jnp.sum(ep, axis=-1, keepdims=True) for ec, ep in zip(ecs, eps)]
    for h, sl in enumerate(sls):
        o = (_dot((ecs[h] / dens[h]).astype(BF16), vc_ref[:, sl].astype(BF16))
             + _dot((eps[h] / dens[h]).astype(BF16), vp_ref[:, sl].astype(BF16)))
        lse = jnp.broadcast_to(ms[h] + jnp.log(dens[h]), (QB, HD))
        if r == 1:
            o_ref[h] = o
            l_ref[h] = lse
        else:
            o_ref[h, pl.ds(rho, QB, stride=r), :] = o
            l_ref[h, pl.ds(rho, QB, stride=r), :] = lse


def _dil_prompt(zr, S):
    B, r, L, _ = zr.shape
    nq = DIL_QBLOCKS
    QB = nq * SPAN
    assert L % QB == 0

    def cur(c):
        return pl.BlockSpec((None, None, QB, ATT_COLS), lambda b, i, p: (b, p, i, c))

    def prev(c):
        return pl.BlockSpec((None, None, SPAN, ATT_COLS), lambda b, i, p: (b, p, jnp.maximum(nq * i - 1, 0), c))

    out_spec = pl.BlockSpec((None, HEADS, QB * r, HD), lambda b, i, p: (b, 0, i, 0))
    return pl.pallas_call(
        functools.partial(_dil_body, r),
        grid=(B, L // QB, r),
        in_specs=[cur(0), cur(1), prev(1), cur(2), prev(2)],
        out_specs=[out_spec, out_spec],
        out_shape=[jax.ShapeDtypeStruct((B, HEADS, S, HD), F32)] * 2,
        compiler_params=_cparams("parallel", "parallel", "arbitrary"),
        name="dil_prompt_r%d" % r,
    )(zr, zr, zr, zr, zr)


def _dil_mix_body(o0, l0, o1, l1, o2, l2, od_ref):
    for h in range(HEADS):
        a0, a1, a2 = l0[h], l1[h], l2[h]
        m = jnp.maximum(jnp.maximum(a0, a1), a2)
        e0, e1, e2 = jnp.exp(a0 - m), jnp.exp(a1 - m), jnp.exp(a2 - m)
        den = e0 + e1 + e2
        od_ref[:, h * HD:(h + 1) * HD] = (e0 / den) * o0[h] + (e1 / den) * o1[h] + (e2 / den) * o2[h]


def _dil_mix(parts, tm=1024):
    B, _, S, _ = parts[0].shape
    spb = S // tm
    spec = pl.BlockSpec((None, HEADS, tm, HD), lambda i: (i // spb, 0, i % spb, 0))
    return pl.pallas_call(
        _dil_mix_body,
        grid=(B * spb,),
        in_specs=[spec] * 6,
        out_specs=pl.BlockSpec((tm, ATT_COLS), lambda i: (i, 0)),
        out_shape=jax.ShapeDtypeStruct((B * S, ATT_COLS), F32),
        compiler_params=_cparams("parallel"),
        name="dil_mix",
    )(*parts)


def _masked_attn(q, k, v, kn, vn, mask_b, mask_n, scale):
    sb = jnp.where(mask_b, _dot_nt(q, k) * scale, NEG)
    sn = jnp.where(mask_n, _dot_nt(q, kn) * scale, NEG)
    m = jnp.maximum(jnp.max(sb, axis=-1, keepdims=True), jnp.max(sn, axis=-1, keepdims=True))
    eb = jnp.exp(sb - m)
    en = jnp.exp(sn - m)
    den = jnp.sum(eb, axis=-1, keepdims=True) + jnp.sum(en, axis=-1, keepdims=True)
    o = _dot((eb / den).astype(BF16), v) + _dot((en / den).astype(BF16), vn)
    return o, m + jnp.log(den)


def _dil_dec_body(T, q_ref, kn_ref, vn_ref, k0, v0, k1, v1, k2, v2, od_ref, nk0, nv0, nk1, nv1, nk2, nv2):
    scale = HD ** -0.5
    R = T * HEADS
    caches = ((k0, v0, nk0, nv0), (k1, v1, nk1, nv1), (k2, v2, nk2, nv2))
    rn = lax.broadcasted_iota(jnp.int32, (R, R), 0)
    cn = lax.broadcasted_iota(jnp.int32, (R, R), 1)
    outs, lses = [], []
    for gi, (_, r) in enumerate(DIL_GROUPS):
        kc, vc, nkc, nvc = caches[gi]
        rows = kc.shape[0]
        for src, new, dst in ((kc, kn_ref, nkc), (vc, vn_ref, nvc)):
            dst[pl.ds(0, rows - R), :] = src[pl.ds(R, rows - R), :]
            dst[pl.ds(rows - R, R), :] = new[gi]
        if r <= T:
            n = rows
            kb = kc[...].astype(BF16)
            vb = vc[...].astype(BF16)
        else:
            n = rows // (r * HEADS) * R
            kb = kc[...].reshape(rows // (r * HEADS), r * HEADS, HD)[:, 0:R, :].reshape(n, HD).astype(BF16)
            vb = vc[...].reshape(rows // (r * HEADS), r * HEADS, HD)[:, 0:R, :].reshape(n, HD).astype(BF16)
        rb = lax.broadcasted_iota(jnp.int32, (R, n), 0)
        cb = lax.broadcasted_iota(jnp.int32, (R, n), 1)
        if r == 1:
            mask_b = jnp.logical_and(cb % HEADS == rb % HEADS, cb // HEADS >= rb // HEADS)
            mask_n = jnp.logical_and(cn % HEADS == rn % HEADS, cn // HEADS <= rn // HEADS)
        else:
            mask_b = cb % R == rb
            mask_n = cn == rn
        o, lse = _masked_attn(q_ref[gi].astype(BF16), kb, vb, kn_ref[gi].astype(BF16), vn_ref[gi].astype(BF16),
                              mask_b, mask_n, scale)
        outs.append(o)
        lses.append(lse)
    l0, l1, l2 = lses
    m = jnp.maximum(jnp.maximum(l0, l1), l2)
    e0, e1, e2 = jnp.exp(l0 - m), jnp.exp(l1 - m), jnp.exp(l2 - m)
    den = e0 + e1 + e2
    od_ref[...] = (e0 / den) * outs[0] + (e1 / den) * outs[1] + (e2 / den) * outs[2]


def _dil_decode(zd, caches):
    Bd, T, _ = zd.shape
    R = T * HEADS
    z6 = zd.reshape(Bd, T, N_DIL, 3, HEADS, HD)
    qkv = [jnp.transpose(z6[:, :, :, c], (0, 2, 1, 3, 4)).reshape(Bd, N_DIL, R, HD) for c in range(3)]
    new = pl.BlockSpec((None, N_DIL, R, HD), lambda b: (b, 0, 0, 0))
    args, specs = list(qkv), [new, new, new]
    out_specs = [pl.BlockSpec((None, R, HD), lambda b: (b, 0, 0))]
    out_shape = [jax.ShapeDtypeStruct((Bd, R, HD), F32)]
    for (wd, r), (kc, vc) in zip(DIL_GROUPS, caches):
        assert kc.shape[1] == wd and wd > T and R % SUBLANES == 0 and (r <= T or wd % r == 0)
        for a in (kc, vc):
            spec = pl.BlockSpec((None, wd * HEADS, HD), lambda b: (b, 0, 0))
            args.append(a.reshape(Bd, wd * HEADS, HD))
            specs.append(spec)
            out_specs.append(spec)
            out_shape.append(jax.ShapeDtypeStruct((Bd, wd * HEADS, HD), F32))
    od, *rolled = pl.pallas_call(
        functools.partial(_dil_dec_body, T),
        grid=(Bd,),
        in_specs=specs,
        out_specs=out_specs,
        out_shape=out_shape,
        compiler_params=_cparams("parallel"),
        name="dil_decode",
    )(*args)
    bufs = [(rolled[2 * gi].reshape(Bd, wd, HEADS, HD), rolled[2 * gi + 1].reshape(Bd, wd, HEADS, HD))
            for gi, (wd, _) in enumerate(DIL_GROUPS)]
    return od.reshape(Bd * T, ATT_COLS), bufs


def _mem_body(q_ref, k_ref, v_ref, o_ref):
    scale = HD ** -0.5
    for h in range(HEADS):
        sl = slice(h * HD, (h + 1) * HD)
        s = _dot_nt(q_ref[:, sl].astype(BF16), k_ref[:, sl].astype(BF16)) * scale
        e = jnp.exp(s - jnp.max(s, axis=-1, keepdims=True))
        p = e / jnp.sum(e, axis=-1, keepdims=True)
        o_ref[:, sl] = _dot(p.astype(BF16), v_ref[:, sl].astype(BF16))


def _mem_attn(qm, mkv, tq):
    B, S, _ = qm.shape
    M = mkv.shape[1]
    return pl.pallas_call(
        _mem_body,
        grid=(B, S // tq),
        in_specs=[
            pl.BlockSpec((None, tq, ATT_COLS), lambda b, i: (b, i, 0)),
            pl.BlockSpec((None, M, ATT_COLS), lambda b, i: (b, 0, 0)),
            pl.BlockSpec((None, M, ATT_COLS), lambda b, i: (b, 0, 1)),
        ],
        out_specs=pl.BlockSpec((None, tq, ATT_COLS), lambda b, i: (b, i, 0)),
        out_shape=jax.ShapeDtypeStruct((B, S, ATT_COLS), F32),
        compiler_params=_cparams("parallel", "arbitrary"),
        name="mem_attn",
    )(qm, mkv, mkv)


def _mem_dec_body(q_ref, k_ref, v_ref, o_ref):
    R, n = q_ref.shape[0], k_ref.shape[0]
    rb = lax.broadcasted_iota(jnp.int32, (R, n), 0)
    cb = lax.broadcasted_iota(jnp.int32, (R, n), 1)
    s = _dot_nt(q_ref[...].astype(BF16), k_ref[...].astype(BF16)) * (HD ** -0.5)
    s = jnp.where(cb % HEADS == rb % HEADS, s, -jnp.inf)
    e = jnp.exp(s - jnp.max(s, axis=-1, keepdims=True))
    p = e / jnp.sum(e, axis=-1, keepdims=True)
    o_ref[...] = _dot(p.astype(BF16), v_ref[...].astype(BF16))


def _mem_attn_decode(qm, mk, mv):
    Bd, T, _ = qm.shape
    M = mk.shape[1]
    R = T * HEADS
    kv = pl.BlockSpec((None, M * HEADS, HD), lambda b: (b, 0, 0))
    o = pl.pallas_call(
        _mem_dec_body,
        grid=(Bd,),
        in_specs=[pl.BlockSpec((None, R, HD), lambda b: (b, 0, 0)), kv, kv],
        out_specs=pl.BlockSpec((None, R, HD), lambda b: (b, 0, 0)),
        out_shape=jax.ShapeDtypeStruct((Bd, R, HD), F32),
        compiler_params=_cparams("parallel"),
        name="mem_attn_decode",
    )(qm.reshape(Bd, R, HD), mk.reshape(Bd, M * HEADS, HD), mv.reshape(Bd, M * HEADS, HD))
    return o.reshape(Bd * T, ATT_COLS)


def _merge_body(x_ref, og_ref, r_ref, od_ref, om_ref, gt_ref, ong_ref, wbg_ref, wbd_ref, wbm_ref, wo_ref,
                ln2_ref, wq_ref, h_ref, n2_ref, qh_ref):
    og = og_ref[...]
    r = r_ref[...]
    ong = ong_ref[...]
    parts = []
    for h in range(GLA_HEADS):
        sl = slice(h * GLA_DV, (h + 1) * GLA_DV)
        parts.append(_rms(og[:, sl], ong) * jax.nn.silu(r[:, sl]))
    br_gla = _dot(jnp.concatenate(parts, axis=-1).astype(BF16), wbg_ref[...])
    br_dil = _dot(od_ref[...].astype(BF16), wbd_ref[...])
    br_mem = _dot(om_ref[...].astype(BF16), wbm_ref[...])
    d = D_MODEL
    merged = gt_ref[:, 0:d] * br_gla + gt_ref[:, d:2 * d] * br_dil + gt_ref[:, 2 * d:3 * d] * br_mem
    hres = x_ref[...] + _dot(merged.astype(BF16), wo_ref[...])
    h_ref[...] = hres
    n2 = _rms(hres, ln2_ref[...]).astype(BF16)
    n2_ref[...] = n2
    qh_ref[...] = _dot(n2, wq_ref[...])


def _merge(x, og, zg, od, om, gt, ong, wbg, wbd, wbm, wo, ln2, wq, tm=256):
    nt, d = x.shape
    tm = min(tm, nt)
    nq = wq.shape[1]
    rblk = (2 * GLA_HEADS * GLA_DK + GLA_HEADS * GLA_DV) // d

    def tok(cols, blk=0):
        return pl.BlockSpec((tm, cols), lambda i: (i, blk))

    def const(shape):
        return pl.BlockSpec(shape, lambda i: (0, 0), pipeline_mode=pl.Buffered(1))

    return pl.pallas_call(
        _merge_body,
        grid=(nt // tm,),
        in_specs=[
            tok(d), tok(d), tok(d, rblk), tok(ATT_COLS), tok(ATT_COLS), tok(3 * d),
            const((1, GLA_DV)), const(wbg.shape), const(wbd.shape), const(wbm.shape), const(wo.shape),
            const((1, d)), const(wq.shape),
        ],
        out_specs=[tok(d), tok(d), tok(nq)],
        out_shape=[
            jax.ShapeDtypeStruct((nt, d), F32),
            jax.ShapeDtypeStruct((nt, d), BF16),
            jax.ShapeDtypeStruct((nt, nq), F32),
        ],
        compiler_params=_cparams("parallel"),
        name="merge",
    )(x, og, zg, od, om, gt, ong.reshape(1, GLA_DV), wbg, wbd, wbm, wo, ln2.reshape(1, d), wq)


PEER_CAND_GROUPS = 10


def _peer_cand_layout(tp):
    r = lax.broadcasted_iota(jnp.int32, (PEER_CAND_GROUPS * SUBLANES, tp), 0)
    grp = r // SUBLANES
    p = r % SUBLANES
    K = PEER_TOPK
    k1 = jnp.where(grp < 2, 0, jnp.where(grp < 9, grp - 1, SUBLANES + p))
    k2 = jnp.where(grp < 2, r, jnp.where(grp < 9, p, 0))
    valid = (k1 + 1) * (k2 + 1) <= K
    return (k1 * K + k2).astype(F32), valid


def _peer_cands(x1, x2):
    top, bot = x2[0:SUBLANES], x2[SUBLANES:2 * SUBLANES]
    g1 = [jnp.broadcast_to(x1[0:1], top.shape)] * 2 + [jnp.broadcast_to(x1[k:k + 1], top.shape) for k in range(1, 8)]
    g1.append(x1[SUBLANES:2 * SUBLANES])
    g2 = [top, bot] + [top] * 7 + [jnp.broadcast_to(x2[0:1], top.shape)]
    return jnp.concatenate(g1, axis=0), jnp.concatenate(g2, axis=0)


def _peer_topk_body(qh_ref, keys_ref, a_ref, b_ref, gw_ref, s_scr, sv_scr, si_scr, cand_scr, eid_scr, sc_scr, ev_scr):
    tp = qh_ref.shape[0]
    K = PEER_TOPK
    NK = PEER_NKEYS
    NH = PEER_HEADS
    rowf = lax.broadcasted_iota(jnp.int32, (NK, tp), 0).astype(F32)
    pos, valid = _peer_cand_layout(tp)
    for i in range(2 * NH):
        s_scr[i] = _dot_nt(keys_ref[i], qh_ref[:, i * LANES:(i + 1) * LANES].astype(BF16))

    def step1(k, carry):
        for i in range(2 * NH):
            s = s_scr[i]
            mx = jnp.max(s, axis=0, keepdims=True)
            idx = jnp.min(jnp.where(s == mx, rowf, float(NK)), axis=0, keepdims=True)
            s_scr[i] = jnp.where(rowf == idx, -jnp.inf, s)
            sv_scr[i, pl.ds(k, 1), :] = mx
            si_scr[i, pl.ds(k, 1), :] = idx
        return carry

    lax.fori_loop(0, K, step1, 0)

    for h in range(NH):
        c1, c2 = _peer_cands(sv_scr[2 * h], sv_scr[2 * h + 1])
        e1, e2 = _peer_cands(si_scr[2 * h], si_scr[2 * h + 1])
        cand_scr[h] = jnp.where(valid, c1 + c2, -jnp.inf)
        eid_scr[h] = e1 * float(NK) + e2

    def step2(k, carry):
        for h in range(NH):
            cand = cand_scr[h]
            mx = jnp.max(cand, axis=0, keepdims=True)
            first = jnp.min(jnp.where(cand == mx, pos, float(K * K)), axis=0, keepdims=True)
            hit = pos == first
            ev_scr[h, pl.ds(k, 1), :] = jnp.max(jnp.where(hit, eid_scr[h], -1.0), axis=0, keepdims=True)
            sc_scr[h, pl.ds(k, 1), :] = mx
            cand_scr[h] = jnp.where(hit, -jnp.inf, cand)
        return carry

    lax.fori_loop(0, K, step2, 0)

    a_rows, b_rows, g_rows = [], [], []
    for h in range(NH):
        sc, ev = sc_scr[h], ev_scr[h]
        e = jnp.exp(sc - jnp.max(sc, axis=0, keepdims=True))
        g_rows.append(e / jnp.sum(e, axis=0, keepdims=True))
        a = jnp.floor(ev * (1.0 / NK))
        a_rows.append(a)
        b_rows.append(ev - a * float(NK))
    a_ref[...] = jnp.transpose(jnp.concatenate(a_rows, axis=0))
    b_ref[...] = jnp.transpose(jnp.concatenate(b_rows, axis=0))
    gw_ref[...] = jnp.transpose(jnp.concatenate(g_rows, axis=0))


def _peer_topk(qh, keys, tp=128):
    nt, nq = qh.shape
    tp = min(tp, nt)
    spec = pl.BlockSpec((tp, LANES), lambda i: (i, 0))
    shp = jax.ShapeDtypeStruct((nt, LANES), F32)
    K, NH = PEER_TOPK, PEER_HEADS
    ncand = PEER_CAND_GROUPS * SUBLANES
    return pl.pallas_call(
        _peer_topk_body,
        grid=(nt // tp,),
        in_specs=[
            pl.BlockSpec((tp, nq), lambda i: (i, 0)),
            pl.BlockSpec(keys.shape, lambda i: (0, 0, 0)),
        ],
        out_specs=[spec, spec, spec],
        out_shape=[shp, shp, shp],
        scratch_shapes=[
            pltpu.VMEM((2 * NH, PEER_NKEYS, tp), F32), pltpu.VMEM((2 * NH, K, tp), F32), pltpu.VMEM((2 * NH, K, tp), F32),
            pltpu.VMEM((NH, ncand, tp), F32), pltpu.VMEM((NH, ncand, tp), F32),
            pltpu.VMEM((NH, K, tp), F32), pltpu.VMEM((NH, K, tp), F32),
        ],
        compiler_params=_cparams("parallel"),
        name="peer_topk",
    )(qh, keys)


def _peer_w_body(a_ref, b_ref, gw_ref, w_ref, scr):
    tw = a_ref.shape[0]
    NK = PEER_NKEYS
    stride = scr.shape[0] // NK
    sub = lax.broadcasted_iota(jnp.int32, (NK, LANES), 0).astype(F32)

    zero = jnp.zeros((NK, LANES), BF16)

    def tok2(p, carry):
        oas, gbs = [], []
        for t in (2 * p, 2 * p + 1):
            arow = a_ref[pl.ds(t, 1), :]
            brow = b_ref[pl.ds(t, 1), :]
            grow = gw_ref[pl.ds(t, 1), :]
            oas.append((arow == sub).astype(BF16))
            gbs.append(jnp.where(brow == sub, grow, 0.0).astype(BF16))
        lhs = jnp.concatenate(oas, axis=1)
        rhs = jnp.concatenate([jnp.concatenate([gbs[0], zero], axis=1),
                               jnp.concatenate([zero, gbs[1]], axis=1)], axis=0)
        w2 = _dot_nt(lhs, rhs)
        scr[pl.ds(2 * p, NK, stride=stride), :] = w2[:, 0:LANES]
        scr[pl.ds(2 * p + 1, NK, stride=stride), :] = w2[:, LANES:2 * LANES]
        return carry

    lax.fori_loop(0, tw // 2, tok2, 0, unroll=32)
    for i1 in range(NK):
        w_ref[:, i1 * NK:(i1 + 1) * NK] = scr[i1 * stride:i1 * stride + tw, :].astype(BF16)


def _peer_weights(a, b, gw, n_experts, tw=128):
    nt = a.shape[0]
    tw = min(tw, nt)
    stride = tw + SUBLANES
    tok = pl.BlockSpec((tw, LANES), lambda i: (i, 0))
    return pl.pallas_call(
        _peer_w_body,
        grid=(nt // tw,),
        in_specs=[tok, tok, tok],
        out_specs=pl.BlockSpec((tw, n_experts), lambda i: (i, 0)),
        out_shape=jax.ShapeDtypeStruct((nt, n_experts), BF16),
        scratch_shapes=[pltpu.VMEM((PEER_NKEYS * stride, LANES), F32)],
        compiler_params=_cparams("parallel"),
        name="peer_weights",
    )(a, b, gw)


def _peer_mix_body(n2_ref, w_ref, h_ref, u_ref, v_ref, y_ref, acc):
    j = pl.program_id(1)

    @pl.when(j == 0)
    def _():
        acc[...] = jnp.zeros(acc.shape, F32)

    hpre = _dot_nt(n2_ref[...], u_ref[...])
    gelu = 0.5 * hpre * (1.0 + lax.erf(hpre * (2.0 ** -0.5)))
    act = (gelu * w_ref[...].astype(F32)).astype(BF16)
    acc[...] += _dot(act, v_ref[...])

    @pl.when(j == pl.num_programs(1) - 1)
    def _():
        y_ref[...] = h_ref[...] + acc[...]


def _peer_mix(n2, w, h, u, v):
    nt, d = n2.shape
    T = min(PEER_T, nt)
    assert nt % T == 0 and u.shape[0] % PEER_EB == 0
    tok = pl.BlockSpec((T, d), lambda i, j: (i, 0))
    tab = pl.BlockSpec((PEER_EB, d), lambda i, j: (j, 0))
    return pl.pallas_call(
        _peer_mix_body,
        grid=(nt // T, u.shape[0] // PEER_EB),
        in_specs=[tok, pl.BlockSpec((T, PEER_EB), lambda i, j: (i, j)), tok, tab, tab],
        out_specs=tok,
        out_shape=jax.ShapeDtypeStruct((nt, d), F32),
        scratch_shapes=[pltpu.VMEM((T, d), F32)],
        compiler_params=_cparams("parallel", "arbitrary"),
        name="peer_mix",
    )(n2, w, h, u, v)


def _rope_tables(pos):
    half = HD // 2
    inv = ROPE_THETA ** (-jnp.arange(half, dtype=F32) / half)
    ang = pos.astype(F32)[:, None] * inv[None, :]
    cos, sin = jnp.cos(ang), jnp.sin(ang)
    return jnp.concatenate([cos, cos], axis=-1), jnp.concatenate([-sin, sin], axis=-1)


def _prep_weights(w_in, gla_wg2, peer_wq, peer_keys, peer_u, peer_v, w_br_gla, w_br_dil, w_br_mem, w_out):
    c0 = GLA_COLS
    c1 = c0 + GLA_RANK
    c2 = c1 + DIL_COLS
    c3 = c2 + ATT_COLS
    wa = jnp.pad(w_in[:, c0:c1], ((0, 0), (0, LANES - GLA_RANK))).astype(BF16)
    wg2 = jnp.pad(gla_wg2, ((0, LANES - GLA_RANK), (0, 0))).astype(BF16)
    return dict(
        w_gla=w_in[:, :c0].astype(BF16), wa=wa, wg2=wg2,
        w_dil=w_in[:, c1:c2].astype(BF16), w_qm=w_in[:, c2:c3].astype(BF16), w_gt=w_in[:, c3:].astype(BF16),
        wbg=w_br_gla.astype(BF16), wbd=w_br_dil.astype(BF16), wbm=w_br_mem.astype(BF16), wo=w_out.astype(BF16),
        wq=peer_wq.astype(BF16),
        keys=peer_keys.reshape(PEER_HEADS * 2, PEER_NKEYS, LANES).astype(BF16),
        u=peer_u.astype(BF16), v=peer_v.astype(BF16),
    )


def _layer(x, pos, mem_k, mem_v, s0, caches, p, W):
    B, S, d = x.shape
    nt = B * S
    xt = x.reshape(nt, d)
    n1 = _norm(xt, p["ln1_g"])
    zg = _proj("gla", n1, W["w_gla"])
    lg = _gla_gate(n1, W["wa"], W["wg2"], p["gla_bg"])
    cosf, sinf = _rope_tables(pos)
    cosf = jnp.broadcast_to(cosf[None], (B, S, HD)).reshape(nt, HD)
    sinf = jnp.broadcast_to(sinf[None], (B, S, HD)).reshape(nt, HD)
    hg = jnp.stack([p["dil_qn_g"], p["dil_kn_g"]])
    qm = _proj("mem_q", n1, W["w_qm"], extra=(p["mem_qn_g"].reshape(1, HD),))
    gt = _proj("sigmoid", n1, W["w_gt"])

    prompt = caches is None
    og, s_fin = _gla(zg.reshape(B, S, GLA_COLS), lg.reshape(B, S, GLA_HEADS * GLA_DK), s0,
                     GLA_CHUNK if prompt else S)

    new_bufs = []
    gcols = 3 * ATT_COLS
    if prompt:
        parts = []
        for gi, (wd, r) in enumerate(DIL_GROUPS):
            zr = _proj_dil_group(n1, W["w_dil"][:, gi * gcols:(gi + 1) * gcols], hg, cosf, sinf, B, S, r)
            parts.extend(_dil_prompt(zr, S))
            keep = min(wd, S)
            kv = zr[:, :, (S - keep) // r:, ATT_COLS:]
            kv = jnp.swapaxes(kv, 1, 2).reshape(B, keep, 2, HEADS, HD)
            new_bufs.append((kv[:, :, 0], kv[:, :, 1]))
        od = _dil_mix(parts)
    else:
        zd3 = _proj("dil", n1, W["w_dil"], extra=(hg, cosf, sinf)).reshape(B, S, DIL_COLS)
        od, new_bufs = _dil_decode(zd3, caches)

    qm3 = qm.reshape(B, S, ATT_COLS)
    if prompt:
        om = _mem_attn(qm3, mem_k, 512).reshape(nt, ATT_COLS)
    else:
        om = _mem_attn_decode(qm3, mem_k, mem_v)

    h, n2, qh = _merge(xt, og.reshape(nt, d), zg, od, om, gt, p["gla_onorm_g"], W["wbg"], W["wbd"], W["wbm"],
                       W["wo"], p["ln2_g"], W["wq"])
    a, b, gw = _peer_topk(qh, W["keys"])
    y = _peer_mix(n2, _peer_weights(a, b, gw, W["u"].shape[0]), h, W["u"], W["v"])
    return y.reshape(B, S, d), s_fin, new_bufs


def kernel(x_prompt, x_sample, mem_prompt, state_gla, cache_dil_k0, cache_dil_v0, cache_dil_k1, cache_dil_v1, cache_dil_k2, cache_dil_v2, cache_mem_k, cache_mem_v, ln1_g, w_in, gla_wg2, gla_bg, gla_onorm_g, dil_qn_g, dil_kn_g, mem_norm_g, w_mem_kv, mem_qn_g, mem_kn_g, w_br_gla, w_br_dil, w_br_mem, w_out, ln2_g, peer_wq, peer_keys, peer_u, peer_v):
    B, S, d = x_prompt.shape
    Bd, T, _ = x_sample.shape
    p = dict(ln1_g=ln1_g, gla_bg=gla_bg, gla_onorm_g=gla_onorm_g, dil_qn_g=dil_qn_g, dil_kn_g=dil_kn_g,
             mem_qn_g=mem_qn_g, ln2_g=ln2_g)
    W = _prep_weights(w_in, gla_wg2, peer_wq, peer_keys, peer_u, peer_v, w_br_gla, w_br_dil, w_br_mem, w_out)

    M = mem_prompt.shape[1]
    mkv = _proj("mem_kv", _norm(mem_prompt.reshape(B * M, d), mem_norm_g), w_mem_kv.astype(BF16),
                extra=(mem_kn_g.reshape(1, HD),))
    mem_k_p = mkv[:, :ATT_COLS].reshape(B, M, HEADS, HD)
    mem_v_p = mkv[:, ATT_COLS:].reshape(B, M, HEADS, HD)

    s0 = jnp.zeros((B, GLA_HEADS, GLA_DK, GLA_DV), F32)
    mkv3 = mkv.reshape(B, M, 2 * ATT_COLS)
    y_prompt, gla_state_p, bufs_p = _layer(x_prompt, jnp.arange(S, dtype=jnp.int32), mkv3, mkv3, s0, None, p, W)

    caches = ((cache_dil_k0, cache_dil_v0), (cache_dil_k1, cache_dil_v1), (cache_dil_k2, cache_dil_v2))
    pos_s = PAST_LEN + jnp.arange(T, dtype=jnp.int32)
    y_sample, gla_state_s, bufs_s = _layer(x_sample, pos_s, cache_mem_k, cache_mem_v, state_gla, caches, p, W)

    (dk0_p, dv0_p), (dk1_p, dv1_p), (dk2_p, dv2_p) = bufs_p
    (dk0_s, dv0_s), (dk1_s, dv1_s), (dk2_s, dv2_s) = bufs_s
    return (y_prompt, y_sample,
            gla_state_p, dk0_p, dv0_p, dk1_p, dv1_p, dk2_p, dv2_p, mem_k_p, mem_v_p,
            gla_state_s, dk0_s, dv0_s, dk1_s, dv1_s, dk2_s, dv2_s)
```

```python
import functools

import jax
import jax.numpy as jnp
from jax import lax
from jax.experimental import pallas as pl
from jax.experimental.pallas import tpu as pltpu

F32 = jnp.float32
BF16 = jnp.bfloat16

D_MODEL = 1024
PAST_LEN = 8192
GLA_HEADS = 4
GLA_DK = 128
GLA_DV = 256
GLA_RANK = 16
GLA_TAU = 16.0
DIL_GROUPS = ((128, 1), (512, 4), (2048, 16))
N_DIL = 3
HEADS = 4
HD = 128
SPAN = 128
MEM_LEN = 256
ROPE_THETA = 10000.0
PEER_HEADS = 8
PEER_NKEYS = 128
PEER_TOPK = 16
EPS = 1e-6
NEG = -1e30

LANES = 128
SUBLANES = 8
GLA_COLS = 2 * GLA_HEADS * GLA_DK + 2 * GLA_HEADS * GLA_DV
DIL_COLS = N_DIL * 3 * HEADS * HD
ATT_COLS = HEADS * HD
VMEM_LIMIT = 56 * 1024 * 1024

PEER_T = 1024
PEER_EB = 1024


def _cparams(*sem):
    return pltpu.CompilerParams(dimension_semantics=sem, vmem_limit_bytes=VMEM_LIMIT)


def _rms(x, g):
    return x * lax.rsqrt(jnp.mean(x * x, axis=-1, keepdims=True) + EPS) * g


def _dot(a, b):
    return jnp.dot(a, b, preferred_element_type=F32)


def _dot_nt(a, b):
    return lax.dot_general(a, b, (((1,), (1,)), ((), ())), preferred_element_type=F32)


def _split3(x):
    hi = x.astype(BF16)
    r1 = x - hi.astype(F32)
    mid = r1.astype(BF16)
    lo = (r1 - mid.astype(F32)).astype(BF16)
    return hi, mid, lo


def _norm_body(x_ref, g_ref, n_ref):
    n_ref[...] = _rms(x_ref[...], g_ref[...]).astype(BF16)


def _norm(x, g, tm=1024):
    nt, d = x.shape
    tm = min(tm, nt)
    return pl.pallas_call(
        _norm_body,
        grid=(nt // tm,),
        in_specs=[pl.BlockSpec((tm, d), lambda i: (i, 0)), pl.BlockSpec((1, d), lambda i: (0, 0))],
        out_specs=pl.BlockSpec((tm, d), lambda i: (i, 0)),
        out_shape=jax.ShapeDtypeStruct((nt, d), BF16),
        compiler_params=_cparams("parallel"),
        name="norm",
    )(x, g.reshape(1, d))


def _proj_body(kind, *refs):
    if kind == "dil":
        n_ref, w_ref, hg_ref, cos_ref, sin_ref, o_ref = refs
    elif kind in ("mem_q", "mem_kv"):
        n_ref, w_ref, hg_ref, o_ref = refs
    else:
        n_ref, w_ref, o_ref = refs
    j = pl.program_id(1)
    z = _dot(n_ref[...], w_ref[...])
    if kind == "gla":
        col = lax.broadcasted_iota(jnp.int32, (1, z.shape[1]), 1) + j * z.shape[1]
        o_ref[...] = z * jnp.where(col < GLA_HEADS * GLA_DK, GLA_DK ** -0.5, 1.0).astype(F32)
    elif kind == "sigmoid":
        o_ref[...] = jax.nn.sigmoid(z)
    elif kind == "mem_q":
        hg = hg_ref[...]
        o_ref[...] = jnp.concatenate([_rms(z[:, h * HD:(h + 1) * HD], hg) for h in range(HEADS)], axis=-1)
    elif kind == "mem_kv":
        @pl.when(j == 0)
        def _():
            hg = hg_ref[...]
            o_ref[...] = jnp.concatenate([_rms(z[:, h * HD:(h + 1) * HD], hg) for h in range(HEADS)], axis=-1)

        @pl.when(j != 0)
        def _():
            o_ref[...] = z
    elif kind == "dil":
        c = j % 3

        @pl.when(c == 2)
        def _():
            o_ref[...] = z

        @pl.when(c != 2)
        def _():
            hg = jnp.where(c == 0, hg_ref[0:1, :], hg_ref[1:2, :])
            cs = cos_ref[...]
            sn = sin_ref[...]
            parts = []
            for h in range(HEADS):
                zh = _rms(z[:, h * HD:(h + 1) * HD], hg)
                parts.append(zh * cs + pltpu.roll(zh, HD // 2, 1) * sn)
            o_ref[...] = jnp.concatenate(parts, axis=-1)
    else:
        raise ValueError(kind)


def _proj(kind, n, w, extra=(), tm=1024, tn=1024):
    nt, d = n.shape
    ncol = w.shape[1]
    tm = min(tm, nt)
    if kind in ("dil", "mem_q", "mem_kv"):
        tn = ATT_COLS
    assert nt % tm == 0 and ncol % tn == 0
    in_specs = [
        pl.BlockSpec((tm, d), lambda i, j: (i, 0)),
        pl.BlockSpec((d, tn), lambda i, j: (0, j)),
    ]
    if kind == "dil":
        in_specs += [
            pl.BlockSpec((2, HD), lambda i, j: (0, 0)),
            pl.BlockSpec((tm, HD), lambda i, j: (i, 0)),
            pl.BlockSpec((tm, HD), lambda i, j: (i, 0)),
        ]
    elif kind in ("mem_q", "mem_kv"):
        in_specs += [pl.BlockSpec((1, HD), lambda i, j: (0, 0))]
    return pl.pallas_call(
        functools.partial(_proj_body, kind),
        grid=(nt // tm, ncol // tn),
        in_specs=in_specs,
        out_specs=pl.BlockSpec((tm, tn), lambda i, j: (i, j)),
        out_shape=jax.ShapeDtypeStruct((nt, ncol), F32),
        compiler_params=_cparams("parallel", "arbitrary"),
        name="proj_" + kind,
    )(n, w, *extra)


def _gate_body(n_ref, wa_ref, wg2_ref, bg_ref, o_ref):
    a = _dot(n_ref[...], wa_ref[...])
    pre = _dot(a.astype(BF16), wg2_ref[...]) + bg_ref[...]
    o_ref[...] = jax.nn.log_sigmoid(pre) / GLA_TAU


def _gla_gate(n, wa, wg2, bg, tm=1024):
    nt, d = n.shape
    tm = min(tm, nt)
    ncol = wg2.shape[1]
    return pl.pallas_call(
        _gate_body,
        grid=(nt // tm,),
        in_specs=[
            pl.BlockSpec((tm, d), lambda i: (i, 0)),
            pl.BlockSpec((d, LANES), lambda i: (0, 0)),
            pl.BlockSpec((LANES, ncol), lambda i: (0, 0)),
            pl.BlockSpec((1, ncol), lambda i: (0, 0)),
        ],
        out_specs=pl.BlockSpec((tm, ncol), lambda i: (i, 0)),
        out_shape=jax.ShapeDtypeStruct((nt, ncol), F32),
        compiler_params=_cparams("parallel"),
        name="gla_gate",
    )(n, wa, wg2, bg.reshape(1, ncol))


GLA_CHUNK = 128
GLA_SUB = 16


def _gla_intra(q, k, b, qs_scr, bs_scr, h, n_sub):
    P, SB = GLA_CHUNK, GLA_SUB
    rowp = lax.broadcasted_iota(jnp.int32, (P, 1), 0)
    qparts, kparts = [], []
    for j in range(n_sub - 1):
        lo, hi = SB * j, SB * (j + 1)
        be = b[hi - 1:hi, :]
        qparts.append(jnp.where(rowp >= hi, q * jnp.exp(b - be), 0.0).astype(BF16))
        kblk = (k[lo:hi] * jnp.exp(be - b[lo:hi])).astype(BF16)
        pieces = [kblk]
        if lo:
            pieces.insert(0, jnp.zeros((lo, k.shape[1]), BF16))
        pieces.append(jnp.zeros((P - hi, k.shape[1]), BF16))
        kparts.append(jnp.concatenate(pieces, axis=0))
    if qparts:
        a_off = _dot_nt(jnp.concatenate(qparts, axis=1), jnp.concatenate(kparts, axis=1))
    else:
        a_off = jnp.zeros((P, P), F32)
    HS = SB // 2
    lane = lax.broadcasted_iota(jnp.int32, (HS, P), 1)
    srow = lax.broadcasted_iota(jnp.int32, (HS, P), 0)
    tiles = []
    for j in range(P // SB):
        if j >= n_sub:
            tiles.append(jnp.zeros((SB, P), F32))
            continue
        lo, mid, hi = SB * j, SB * j + HS, SB * (j + 1)
        halves = []
        for s0 in (lo, mid):
            kj, bj = k[s0:s0 + HS], b[s0:s0 + HS]
            at = jnp.zeros((HS, P), F32)
            for t in range(s0, hi):
                bt = bs_scr[h, pl.ds(t, 1), :]
                qt = qs_scr[h, pl.ds(t, 1), :]
                colv = jnp.sum(qt * kj * jnp.exp(jnp.minimum(bt - bj, 0.0)), axis=-1, keepdims=True)
                at = jnp.where(lane == t, colv, at)
            halves.append(jnp.where(srow + s0 <= lane, at, 0.0))
        tiles.extend(halves)
    return a_off + jnp.transpose(jnp.concatenate(tiles, axis=0))


def _gla_body(tc, q_ref, k_ref, v_ref, g_ref, s0_ref, o_ref, sf_ref, s_scr, qp, kp, vp, gp, qs_scr, bs_scr):
    c = pl.program_id(1)
    P = GLA_CHUNK
    H, dk, dv = GLA_HEADS, GLA_DK, GLA_DV
    n_sub = -(-tc // GLA_SUB)

    @pl.when(c == 0)
    def _():
        s_scr[...] = s0_ref[...]

    if tc == P:
        q, k, v, g = q_ref[...], k_ref[...], v_ref[...], g_ref[...]
    else:
        for pad, ref in ((qp, q_ref), (kp, k_ref), (vp, v_ref), (gp, g_ref)):
            pad[...] = jnp.zeros(pad.shape, F32)
            pad[0:tc, :] = ref[...]
        q, k, v, g = qp[...], kp[...], vp[...], gp[...]

    row = lax.broadcasted_iota(jnp.int32, (P, P), 0)
    col = lax.broadcasted_iota(jnp.int32, (P, P), 1)
    trib = (row >= col).astype(BF16)
    g1, g2, g3 = _split3(g)
    b = _dot(trib, g1) + _dot(trib, g2) + _dot(trib, g3)
    b_end = b[P - 1:P, :]
    qe = (q * jnp.exp(b)).astype(BF16)
    kd = k * jnp.exp(b_end - b)
    vb = v.astype(BF16)
    for h in range(H):
        ks = slice(h * dk, (h + 1) * dk)
        qs_scr[h] = q[:, ks]
        bs_scr[h] = b[:, ks]
    for h in range(H):
        ks = slice(h * dk, (h + 1) * dk)
        vs = slice(h * dv, (h + 1) * dv)
        a = _gla_intra(q[:, ks], k[:, ks], b[:, ks], qs_scr, bs_scr, h, n_sub)
        s = s_scr[h]
        o = _dot(qe[:, ks], s.astype(BF16)) + _dot(a.astype(BF16), vb[:, vs])
        o_ref[:, vs] = o[0:tc, :]
        decay = jnp.exp(jnp.sum(jnp.transpose(g[:, ks]), axis=1, keepdims=True))
        s_new = decay * s + _dot(jnp.transpose(kd[:, ks]).astype(BF16), vb[:, vs])
        s_scr[h] = s_new

        @pl.when(c == pl.num_programs(1) - 1)
        def _():
            sf_ref[h] = s_new


def _gla(zg, lg, s0, tc):
    B, S, _ = zg.shape
    H, dk, dv = GLA_HEADS, GLA_DK, GLA_DV
    assert S % tc == 0
    P = GLA_CHUNK
    qk = H * dk
    st = pl.BlockSpec((None, H, dk, dv), lambda b, c: (b, 0, 0, 0))
    return pl.pallas_call(
        functools.partial(_gla_body, tc),
        grid=(B, S // tc),
        in_specs=[
            pl.BlockSpec((None, tc, qk), lambda b, c: (b, c, 0)),
            pl.BlockSpec((None, tc, qk), lambda b, c: (b, c, 1)),
            pl.BlockSpec((None, tc, H * dv), lambda b, c: (b, c, 2 * qk // (H * dv))),
            pl.BlockSpec((None, tc, qk), lambda b, c: (b, c, 0)),
            st,
        ],
        out_specs=[pl.BlockSpec((None, tc, H * dv), lambda b, c: (b, c, 0)), st],
        out_shape=[
            jax.ShapeDtypeStruct((B, S, H * dv), F32),
            jax.ShapeDtypeStruct((B, H, dk, dv), F32),
        ],
        scratch_shapes=[
            pltpu.VMEM((H, dk, dv), F32),
            pltpu.VMEM((P, qk), F32), pltpu.VMEM((P, qk), F32), pltpu.VMEM((P, H * dv), F32), pltpu.VMEM((P, qk), F32),
            pltpu.VMEM((H, P, dk), F32), pltpu.VMEM((H, P, dk), F32),
        ],
        compiler_params=_cparams("parallel", "arbitrary"),
        name="gla",
    )(zg, zg, zg, lg, s0)


def _proj_dil_body(r, spb, nkb, n_ref, w_ref, hg_ref, cos_ref, sin_ref, o_ref, kc_ref, vc_ref, z_scr):
    i = pl.program_id(0)
    j = pl.program_id(1)
    z = _dot(n_ref[...], w_ref[...])

    @pl.when(j == 2)
    def _():
        for h in range(HEADS):
            z_scr[h] = z[:, h * HD:(h + 1) * HD]

    @pl.when(j != 2)
    def _():
        hg = jnp.where(j == 0, hg_ref[0:1, :], hg_ref[1:2, :])
        cs = cos_ref[...]
        sn = sin_ref[...]
        zs = [z[:, h * HD:(h + 1) * HD] for h in range(HEADS)]
        ms = [jnp.mean(zh * zh, axis=-1, keepdims=True) for zh in zs]
        ns = [zh * lax.rsqrt(m + EPS) * hg for zh, m in zip(zs, ms)]
        rs = [pltpu.roll(nh, HD // 2, 1) for nh in ns]
        for h in range(HEADS):
            z_scr[h] = ns[h] * cs + rs[h] * sn

    tm = z_scr.shape[1]
    rows = tm // r
    for rho in range(r):
        for h in range(HEADS):
            zh = z_scr[h, pl.ds(rho, rows, stride=r), :] if r > 1 else z_scr[h]
            o_ref[rho, :, h * HD:(h + 1) * HD] = zh

    kb = kc_ref.shape[0] // HEADS
    for jj, dst in ((1, kc_ref), (2, vc_ref)):
        @pl.when(jnp.logical_and(j == jj, i % spb >= spb - nkb))
        def _():
            for h in range(HEADS):
                dst[pl.ds(h, kb, stride=HEADS), :] = z_scr[h, tm - kb:tm, :]


def _proj_dil_group(n, w, hg, cosf, sinf, B, S, r, keep, tm=1024):
    nt, d = n.shape
    assert S % tm == 0 and tm % (r * SUBLANES) == 0
    spb = S // tm
    kb = min(keep, tm)
    assert keep % kb == 0 and keep <= S
    nkb = keep // kb
    cache_spec = pl.BlockSpec((None, kb * HEADS, HD), lambda i, j: (i // spb, jnp.maximum(i % spb - (spb - nkb), 0), 0))
    cache_shape = jax.ShapeDtypeStruct((B, keep * HEADS, HD), F32)
    return pl.pallas_call(
        functools.partial(_proj_dil_body, r, spb, nkb),
        grid=(nt // tm, 3),
        in_specs=[
            pl.BlockSpec((tm, d), lambda i, j: (i, 0)),
            pl.BlockSpec((d, ATT_COLS), lambda i, j: (0, j)),
            pl.BlockSpec((2, HD), lambda i, j: (0, 0)),
            pl.BlockSpec((tm, HD), lambda i, j: (i, 0)),
            pl.BlockSpec((tm, HD), lambda i, j: (i, 0)),
        ],
        out_specs=[pl.BlockSpec((None, r, tm // r, ATT_COLS), lambda i, j: (i // spb, 0, i % spb, j)),
                   cache_spec, cache_spec],
        out_shape=[jax.ShapeDtypeStruct((B, r, S // r, 3 * ATT_COLS), F32), cache_shape, cache_shape],
        scratch_shapes=[pltpu.VMEM((HEADS, tm, HD), F32)],
        compiler_params=_cparams("arbitrary", "arbitrary"),
        name="proj_dil_r%d" % r,
    )(n, w, hg, cosf, sinf)


DIL_QBLOCKS = 2


def _dil_body(r, q_ref, kc_ref, kp_ref, vc_ref, vp_ref, o_ref, l_ref):
    i = pl.program_id(1)
    rho = pl.program_id(2)
    scale = HD ** -0.5
    QB = q_ref.shape[0]
    mq = lax.broadcasted_iota(jnp.int32, (QB, QB), 0)
    mk = lax.broadcasted_iota(jnp.int32, (QB, QB), 1)
    mask_c = jnp.logical_and(mk <= mq, mq - mk <= SPAN)
    pq = lax.broadcasted_iota(jnp.int32, (QB, SPAN), 0)
    pk = lax.broadcasted_iota(jnp.int32, (QB, SPAN), 1)
    mask_p = jnp.logical_and(pk >= pq, i > 0)
    sls = [slice(h * HD, (h + 1) * HD) for h in range(HEADS)]
    qs = [q_ref[:, sl].astype(BF16) for sl in sls]
    scs = [jnp.where(mask_c, _dot_nt(q, kc_ref[:, sl].astype(BF16)) * scale, NEG) for q, sl in zip(qs, sls)]
    sps = [jnp.where(mask_p, _dot_nt(q, kp_ref[:, sl].astype(BF16)) * scale, NEG) for q, sl in zip(qs, sls)]
    ms = [jnp.maximum(jnp.max(sc, axis=-1, keepdims=True), jnp.max(sp, axis=-1, keepdims=True))
          for sc, sp in zip(scs, sps)]
    ecs = [jnp.exp(sc - m) for sc, m in zip(scs, ms)]
    eps = [jnp.exp(sp - m) for sp, m in zip(sps, ms)]
    dens = [jnp.sum(ec, axis=-1, keepdims=True) + jnp.sum(ep, axis=-1, keepdims=True) for ec, ep in zip(ecs, eps)]
    for h, sl in enumerate(sls):
        o = (_dot((ecs[h] / dens[h]).astype(BF16), vc_ref[:, sl].astype(BF16))
             + _dot((eps[h] / dens[h]).astype(BF16), vp_ref[:, sl].astype(BF16)))
        lse = jnp.broadcast_to(ms[h] + jnp.log(dens[h]), (QB, HD))
        if r == 1:
            o_ref[h] = o
            l_ref[h] = lse
        else:
            o_ref[h, pl.ds(rho, QB, stride=r), :] = o
            l_ref[h, pl.ds(rho, QB, stride=r), :] = lse


def _dil_prompt(zr, S):
    B, r, L, _ = zr.shape
    nq = DIL_QBLOCKS
    QB = nq * SPAN
    assert L % QB == 0

    def cur(c):
        return pl.BlockSpec((None, None, QB, ATT_COLS), lambda b, i, p: (b, p, i, c))

    def prev(c):
        return pl.BlockSpec((None, None, SPAN, ATT_COLS), lambda b, i, p: (b, p, jnp.maximum(nq * i - 1, 0), c))

    out_spec = pl.BlockSpec((None, HEADS, QB * r, HD), lambda b, i, p: (b, 0, i, 0))
    return pl.pallas_call(
        functools.partial(_dil_body, r),
        grid=(B, L // QB, r),
        in_specs=[cur(0), cur(1), prev(1), cur(2), prev(2)],
        out_specs=[out_spec, out_spec],
        out_shape=[jax.ShapeDtypeStruct((B, HEADS, S, HD), F32)] * 2,
        compiler_params=_cparams("parallel", "parallel", "arbitrary"),
        name="dil_prompt_r%d" % r,
    )(zr, zr, zr, zr, zr)


def _dil_mix_body(o0, l0, o1, l1, o2, l2, od_ref):
    for h in range(HEADS):
        a0, a1, a2 = l0[h], l1[h], l2[h]
        m = jnp.maximum(jnp.maximum(a0, a1), a2)
        e0, e1, e2 = jnp.exp(a0 - m), jnp.exp(a1 - m), jnp.exp(a2 - m)
        den = e0 + e1 + e2
        od_ref[:, h * HD:(h + 1) * HD] = (e0 / den) * o0[h] + (e1 / den) * o1[h] + (e2 / den) * o2[h]


def _dil_mix(parts, tm=1024):
    B, _, S, _ = parts[0].shape
    spb = S // tm
    spec = pl.BlockSpec((None, HEADS, tm, HD), lambda i: (i // spb, 0, i % spb, 0))
    return pl.pallas_call(
        _dil_mix_body,
        grid=(B * spb,),
        in_specs=[spec] * 6,
        out_specs=pl.BlockSpec((tm, ATT_COLS), lambda i: (i, 0)),
        out_shape=jax.ShapeDtypeStruct((B * S, ATT_COLS), F32),
        compiler_params=_cparams("parallel"),
        name="dil_mix",
    )(*parts)


def _masked_attn(q, k, v, kn, vn, mask_b, mask_n, scale):
    sb = jnp.where(mask_b, _dot_nt(q, k) * scale, NEG)
    sn = jnp.where(mask_n, _dot_nt(q, kn) * scale, NEG)
    m = jnp.maximum(jnp.max(sb, axis=-1, keepdims=True), jnp.max(sn, axis=-1, keepdims=True))
    eb = jnp.exp(sb - m)
    en = jnp.exp(sn - m)
    den = jnp.sum(eb, axis=-1, keepdims=True) + jnp.sum(en, axis=-1, keepdims=True)
    o = _dot((eb / den).astype(BF16), v) + _dot((en / den).astype(BF16), vn)
    return o, m + jnp.log(den)


def _dil_dec_body(T, q_ref, kn_ref, vn_ref, k0, v0, k1, v1, k2, v2, od_ref, nk0, nv0, nk1, nv1, nk2, nv2):
    scale = HD ** -0.5
    R = T * HEADS
    caches = ((k0, v0, nk0, nv0), (k1, v1, nk1, nv1), (k2, v2, nk2, nv2))
    rn = lax.broadcasted_iota(jnp.int32, (R, R), 0)
    cn = lax.broadcasted_iota(jnp.int32, (R, R), 1)
    outs, lses = [], []
    for gi, (_, r) in enumerate(DIL_GROUPS):
        kc, vc, nkc, nvc = caches[gi]
        rows = kc.shape[0]
        for src, new, dst in ((kc, kn_ref, nkc), (vc, vn_ref, nvc)):
            dst[pl.ds(0, rows - R), :] = src[pl.ds(R, rows - R), :]
            dst[pl.ds(rows - R, R), :] = new[gi]
        if r <= T:
            n = rows
            kb = kc[...].astype(BF16)
            vb = vc[...].astype(BF16)
        else:
            n = rows // (r * HEADS) * R
            kb = kc[...].reshape(rows // (r * HEADS), r * HEADS, HD)[:, 0:R, :].reshape(n, HD).astype(BF16)
            vb = vc[...].reshape(rows // (r * HEADS), r * HEADS, HD)[:, 0:R, :].reshape(n, HD).astype(BF16)
        rb = lax.broadcasted_iota(jnp.int32, (R, n), 0)
        cb = lax.broadcasted_iota(jnp.int32, (R, n), 1)
        if r == 1:
            mask_b = jnp.logical_and(cb % HEADS == rb % HEADS, cb // HEADS >= rb // HEADS)
            mask_n = jnp.logical_and(cn % HEADS == rn % HEADS, cn // HEADS <= rn // HEADS)
        else:
            mask_b = cb % R == rb
            mask_n = cn == rn
        o, lse = _masked_attn(q_ref[gi].astype(BF16), kb, vb, kn_ref[gi].astype(BF16), vn_ref[gi].astype(BF16),
                              mask_b, mask_n, scale)
        outs.append(o)
        lses.append(lse)
    l0, l1, l2 = lses
    m = jnp.maximum(jnp.maximum(l0, l1), l2)
    e0, e1, e2 = jnp.exp(l0 - m), jnp.exp(l1 - m), jnp.exp(l2 - m)
    den = e0 + e1 + e2
    od_ref[...] = (e0 / den) * outs[0] + (e1 / den) * outs[1] + (e2 / den) * outs[2]


def _dil_decode(zd, caches):
    Bd, T, _ = zd.shape
    R = T * HEADS
    z6 = zd.reshape(Bd, T, N_DIL, 3, HEADS, HD)
    qkv = [jnp.transpose(z6[:, :, :, c], (0, 2, 1, 3, 4)).reshape(Bd, N_DIL, R, HD) for c in range(3)]
    new = pl.BlockSpec((None, N_DIL, R, HD), lambda b: (b, 0, 0, 0))
    args, specs = list(qkv), [new, new, new]
    out_specs = [pl.BlockSpec((None, R, HD), lambda b: (b, 0, 0))]
    out_shape = [jax.ShapeDtypeStruct((Bd, R, HD), F32)]
    for (wd, r), (kc, vc) in zip(DIL_GROUPS, caches):
        assert kc.shape[1] == wd and wd > T and R % SUBLANES == 0 and (r <= T or wd % r == 0)
        for a in (kc, vc):
            spec = pl.BlockSpec((None, wd * HEADS, HD), lambda b: (b, 0, 0))
            args.append(a.reshape(Bd, wd * HEADS, HD))
            specs.append(spec)
            out_specs.append(spec)
            out_shape.append(jax.ShapeDtypeStruct((Bd, wd * HEADS, HD), F32))
    od, *rolled = pl.pallas_call(
        functools.partial(_dil_dec_body, T),
        grid=(Bd,),
        in_specs=specs,
        out_specs=out_specs,
        out_shape=out_shape,
        compiler_params=_cparams("parallel"),
        name="dil_decode",
    )(*args)
    bufs = [(rolled[2 * gi].reshape(Bd, wd, HEADS, HD), rolled[2 * gi + 1].reshape(Bd, wd, HEADS, HD))
            for gi, (wd, _) in enumerate(DIL_GROUPS)]
    return od.reshape(Bd * T, ATT_COLS), bufs


def _mem_body(q_ref, k_ref, v_ref, o_ref):
    scale = HD ** -0.5
    for h in range(HEADS):
        sl = slice(h * HD, (h + 1) * HD)
        s = _dot_nt(q_ref[:, sl].astype(BF16), k_ref[:, sl].astype(BF16)) * scale
        e = jnp.exp(s - jnp.max(s, axis=-1, keepdims=True))
        p = e / jnp.sum(e, axis=-1, keepdims=True)
        o_ref[:, sl] = _dot(p.astype(BF16), v_ref[:, sl].astype(BF16))


def _mem_attn(qm, mkv, tq):
    B, S, _ = qm.shape
    M = mkv.shape[1]
    return pl.pallas_call(
        _mem_body,
        grid=(B, S // tq),
        in_specs=[
            pl.BlockSpec((None, tq, ATT_COLS), lambda b, i: (b, i, 0)),
            pl.BlockSpec((None, M, ATT_COLS), lambda b, i: (b, 0, 0)),
            pl.BlockSpec((None, M, ATT_COLS), lambda b, i: (b, 0, 1)),
        ],
        out_specs=pl.BlockSpec((None, tq, ATT_COLS), lambda b, i: (b, i, 0)),
        out_shape=jax.ShapeDtypeStruct((B, S, ATT_COLS), F32),
        compiler_params=_cparams("parallel", "arbitrary"),
        name="mem_attn",
    )(qm, mkv, mkv)


def _mem_dec_body(q_ref, k_ref, v_ref, o_ref):
    R, n = q_ref.shape[0], k_ref.shape[0]
    rb = lax.broadcasted_iota(jnp.int32, (R, n), 0)
    cb = lax.broadcasted_iota(jnp.int32, (R, n), 1)
    s = _dot_nt(q_ref[...].astype(BF16), k_ref[...].astype(BF16)) * (HD ** -0.5)
    s = jnp.where(cb % HEADS == rb % HEADS, s, -jnp.inf)
    e = jnp.exp(s - jnp.max(s, axis=-1, keepdims=True))
    p = e / jnp.sum(e, axis=-1, keepdims=True)
    o_ref[...] = _dot(p.astype(BF16), v_ref[...].astype(BF16))


def _mem_attn_decode(qm, mk, mv):
    Bd, T, _ = qm.shape
    M = mk.shape[1]
    R = T * HEADS
    kv = pl.BlockSpec((None, M * HEADS, HD), lambda b: (b, 0, 0))
    o = pl.pallas_call(
        _mem_dec_body,
        grid=(Bd,),
        in_specs=[pl.BlockSpec((None, R, HD), lambda b: (b, 0, 0)), kv, kv],
        out_specs=pl.BlockSpec((None, R, HD), lambda b: (b, 0, 0)),
        out_shape=jax.ShapeDtypeStruct((Bd, R, HD), F32),
        compiler_params=_cparams("parallel"),
        name="mem_attn_decode",
    )(qm.reshape(Bd, R, HD), mk.reshape(Bd, M * HEADS, HD), mv.reshape(Bd, M * HEADS, HD))
    return o.reshape(Bd * T, ATT_COLS)


def _merge_body(x_ref, og_ref, r_ref, od_ref, om_ref, gt_ref, ong_ref, wbg_ref, wbd_ref, wbm_ref, wo_ref,
                ln2_ref, wq_ref, h_ref, n2_ref, qh_ref):
    og = og_ref[...]
    r = r_ref[...]
    ong = ong_ref[...]
    parts = []
    for h in range(GLA_HEADS):
        sl = slice(h * GLA_DV, (h + 1) * GLA_DV)
        parts.append(_rms(og[:, sl], ong) * jax.nn.silu(r[:, sl]))
    br_gla = _dot(jnp.concatenate(parts, axis=-1).astype(BF16), wbg_ref[...])
    br_dil = _dot(od_ref[...].astype(BF16), wbd_ref[...])
    br_mem = _dot(om_ref[...].astype(BF16), wbm_ref[...])
    d = D_MODEL
    merged = gt_ref[:, 0:d] * br_gla + gt_ref[:, d:2 * d] * br_dil + gt_ref[:, 2 * d:3 * d] * br_mem
    hres = x_ref[...] + _dot(merged.astype(BF16), wo_ref[...])
    h_ref[...] = hres
    n2 = _rms(hres, ln2_ref[...]).astype(BF16)
    n2_ref[...] = n2
    qh_ref[...] = _dot(n2, wq_ref[...])


def _merge(x, og, zg, od, om, gt, ong, wbg, wbd, wbm, wo, ln2, wq, tm=256):
    nt, d = x.shape
    tm = min(tm, nt)
    nq = wq.shape[1]
    rblk = (2 * GLA_HEADS * GLA_DK + GLA_HEADS * GLA_DV) // d

    def tok(cols, blk=0):
        return pl.BlockSpec((tm, cols), lambda i: (i, blk))

    def const(shape):
        return pl.BlockSpec(shape, lambda i: (0, 0), pipeline_mode=pl.Buffered(1))

    return pl.pallas_call(
        _merge_body,
        grid=(nt // tm,),
        in_specs=[
            tok(d), tok(d), tok(d, rblk), tok(ATT_COLS), tok(ATT_COLS), tok(3 * d),
            const((1, GLA_DV)), const(wbg.shape), const(wbd.shape), const(wbm.shape), const(wo.shape),
            const((1, d)), const(wq.shape),
        ],
        out_specs=[tok(d), tok(d), tok(nq)],
        out_shape=[
            jax.ShapeDtypeStruct((nt, d), F32),
            jax.ShapeDtypeStruct((nt, d), BF16),
            jax.ShapeDtypeStruct((nt, nq), F32),
        ],
        compiler_params=_cparams("parallel"),
        name="merge",
    )(x, og, zg, od, om, gt, ong.reshape(1, GLA_DV), wbg, wbd, wbm, wo, ln2.reshape(1, d), wq)


PEER_CAND_GROUPS = 10


def _peer_cand_layout(tp):
    r = lax.broadcasted_iota(jnp.int32, (PEER_CAND_GROUPS * SUBLANES, tp), 0)
    grp = r // SUBLANES
    p = r % SUBLANES
    K = PEER_TOPK
    k1 = jnp.where(grp < 2, 0, jnp.where(grp < 9, grp - 1, SUBLANES + p))
    k2 = jnp.where(grp < 2, r, jnp.where(grp < 9, p, 0))
    valid = (k1 + 1) * (k2 + 1) <= K
    return (k1 * K + k2).astype(F32), valid


def _peer_cands(x1, x2):
    top, bot = x2[0:SUBLANES], x2[SUBLANES:2 * SUBLANES]
    g1 = [jnp.broadcast_to(x1[0:1], top.shape)] * 2 + [jnp.broadcast_to(x1[k:k + 1], top.shape) for k in range(1, 8)]
    g1.append(x1[SUBLANES:2 * SUBLANES])
    g2 = [top, bot] + [top] * 7 + [jnp.broadcast_to(x2[0:1], top.shape)]
    return jnp.concatenate(g1, axis=0), jnp.concatenate(g2, axis=0)


def _peer_topk_body(qh_ref, keys_ref, a_ref, b_ref, gw_ref, s_scr, sv_scr, si_scr, cand_scr, eid_scr, sc_scr, ev_scr):
    tp = qh_ref.shape[0]
    K = PEER_TOPK
    NK = PEER_NKEYS
    NH = PEER_HEADS
    rowf = lax.broadcasted_iota(jnp.int32, (NK, tp), 0).astype(F32)
    pos, valid = _peer_cand_layout(tp)
    for i in range(2 * NH):
        s_scr[i] = _dot_nt(keys_ref[i], qh_ref[:, i * LANES:(i + 1) * LANES].astype(BF16))

    def step1(k, carry):
        for i in range(2 * NH):
            s = s_scr[i]
            mx = jnp.max(s, axis=0, keepdims=True)
            idx = jnp.min(jnp.where(s == mx, rowf, float(NK)), axis=0, keepdims=True)
            s_scr[i] = jnp.where(rowf == idx, -jnp.inf, s)
            sv_scr[i, pl.ds(k, 1), :] = mx
            si_scr[i, pl.ds(k, 1), :] = idx
        return carry

    lax.fori_loop(0, K, step1, 0)

    for h in range(NH):
        c1, c2 = _peer_cands(sv_scr[2 * h], sv_scr[2 * h + 1])
        e1, e2 = _peer_cands(si_scr[2 * h], si_scr[2 * h + 1])
        cand_scr[h] = jnp.where(valid, c1 + c2, -jnp.inf)
        eid_scr[h] = e1 * float(NK) + e2

    def step2(k, carry):
        for h in range(NH):
            cand = cand_scr[h]
            mx = jnp.max(cand, axis=0, keepdims=True)
            first = jnp.min(jnp.where(cand == mx, pos, float(K * K)), axis=0, keepdims=True)
            hit = pos == first
            ev_scr[h, pl.ds(k, 1), :] = jnp.max(jnp.where(hit, eid_scr[h], -1.0), axis=0, keepdims=True)
            sc_scr[h, pl.ds(k, 1), :] = mx
            cand_scr[h] = jnp.where(hit, -jnp.inf, cand)
        return carry

    lax.fori_loop(0, K, step2, 0)

    a_rows, b_rows, g_rows = [], [], []
    for h in range(NH):
        sc, ev = sc_scr[h], ev_scr[h]
        e = jnp.exp(sc - jnp.max(sc, axis=0, keepdims=True))
        g_rows.append(e / jnp.sum(e, axis=0, keepdims=True))
        a = jnp.floor(ev * (1.0 / NK))
        a_rows.append(a)
        b_rows.append(ev - a * float(NK))
    a_ref[...] = jnp.transpose(jnp.concatenate(a_rows, axis=0))
    b_ref[...] = jnp.transpose(jnp.concatenate(b_rows, axis=0))
    gw_ref[...] = jnp.transpose(jnp.concatenate(g_rows, axis=0))


def _peer_topk(qh, keys, tp=128):
    nt, nq = qh.shape
    tp = min(tp, nt)
    spec = pl.BlockSpec((tp, LANES), lambda i: (i, 0))
    shp = jax.ShapeDtypeStruct((nt, LANES), F32)
    K, NH = PEER_TOPK, PEER_HEADS
    ncand = PEER_CAND_GROUPS * SUBLANES
    return pl.pallas_call(
        _peer_topk_body,
        grid=(nt // tp,),
        in_specs=[
            pl.BlockSpec((tp, nq), lambda i: (i, 0)),
            pl.BlockSpec(keys.shape, lambda i: (0, 0, 0)),
        ],
        out_specs=[spec, spec, spec],
        out_shape=[shp, shp, shp],
        scratch_shapes=[
            pltpu.VMEM((2 * NH, PEER_NKEYS, tp), F32), pltpu.VMEM((2 * NH, K, tp), F32), pltpu.VMEM((2 * NH, K, tp), F32),
            pltpu.VMEM((NH, ncand, tp), F32), pltpu.VMEM((NH, ncand, tp), F32),
            pltpu.VMEM((NH, K, tp), F32), pltpu.VMEM((NH, K, tp), F32),
        ],
        compiler_params=_cparams("parallel"),
        name="peer_topk",
    )(qh, keys)


def _peer_w_body(a_ref, b_ref, gw_ref, w_ref, scr):
    tw = a_ref.shape[0]
    NK = PEER_NKEYS
    stride = scr.shape[0] // NK
    sub = lax.broadcasted_iota(jnp.int32, (NK, LANES), 0).astype(F32)

    zero = jnp.zeros((NK, LANES), BF16)

    def tok2(p, carry):
        oas, gbs = [], []
        for t in (2 * p, 2 * p + 1):
            arow = a_ref[pl.ds(t, 1), :]
            brow = b_ref[pl.ds(t, 1), :]
            grow = gw_ref[pl.ds(t, 1), :]
            oas.append((arow == sub).astype(BF16))
            gbs.append(jnp.where(brow == sub, grow, 0.0).astype(BF16))
        lhs = jnp.concatenate(oas, axis=1)
        rhs = jnp.concatenate([jnp.concatenate([gbs[0], zero], axis=1),
                               jnp.concatenate([zero, gbs[1]], axis=1)], axis=0)
        w2 = _dot_nt(lhs, rhs)
        scr[pl.ds(2 * p, NK, stride=stride), :] = w2[:, 0:LANES]
        scr[pl.ds(2 * p + 1, NK, stride=stride), :] = w2[:, LANES:2 * LANES]
        return carry

    lax.fori_loop(0, tw // 2, tok2, 0, unroll=32)
    for i1 in range(NK):
        w_ref[:, i1 * NK:(i1 + 1) * NK] = scr[i1 * stride:i1 * stride + tw, :].astype(BF16)


def _peer_weights(a, b, gw, n_experts, tw=128):
    nt = a.shape[0]
    tw = min(tw, nt)
    stride = tw + SUBLANES
    tok = pl.BlockSpec((tw, LANES), lambda i: (i, 0))
    return pl.pallas_call(
        _peer_w_body,
        grid=(nt // tw,),
        in_specs=[tok, tok, tok],
        out_specs=pl.BlockSpec((tw, n_experts), lambda i: (i, 0)),
        out_shape=jax.ShapeDtypeStruct((nt, n_experts), BF16),
        scratch_shapes=[pltpu.VMEM((PEER_NKEYS * stride, LANES), F32)],
        compiler_params=_cparams("parallel"),
        name="peer_weights",
    )(a, b, gw)


def _peer_mix_body(n2_ref, w_ref, h_ref, u_ref, v_ref, y_ref, acc):
    j = pl.program_id(1)

    @pl.when(j == 0)
    def _():
        acc[...] = jnp.zeros(acc.shape, F32)

    hpre = _dot_nt(n2_ref[...], u_ref[...])
    gelu = 0.5 * hpre * (1.0 + lax.erf(hpre * (2.0 ** -0.5)))
    act = (gelu * w_ref[...].astype(F32)).astype(BF16)
    acc[...] += _dot(act, v_ref[...])

    @pl.when(j == pl.num_programs(1) - 1)
    def _():
        y_ref[...] = h_ref[...] + acc[...]


def _peer_mix(n2, w, h, u, v):
    nt, d = n2.shape
    T = min(PEER_T, nt)
    assert nt % T == 0 and u.shape[0] % PEER_EB == 0
    tok = pl.BlockSpec((T, d), lambda i, j: (i, 0))
    tab = pl.BlockSpec((PEER_EB, d), lambda i, j: (j, 0))
    return pl.pallas_call(
        _peer_mix_body,
        grid=(nt // T, u.shape[0] // PEER_EB),
        in_specs=[tok, pl.BlockSpec((T, PEER_EB), lambda i, j: (i, j)), tok, tab, tab],
        out_specs=tok,
        out_shape=jax.ShapeDtypeStruct((nt, d), F32),
        scratch_shapes=[pltpu.VMEM((T, d), F32)],
        compiler_params=_cparams("parallel", "arbitrary"),
        name="peer_mix",
    )(n2, w, h, u, v)


def _rope_tables(pos):
    half = HD // 2
    inv = ROPE_THETA ** (-jnp.arange(half, dtype=F32) / half)
    ang = pos.astype(F32)[:, None] * inv[None, :]
    cos, sin = jnp.cos(ang), jnp.sin(ang)
    return jnp.concatenate([cos, cos], axis=-1), jnp.concatenate([-sin, sin], axis=-1)


def _prep_weights(w_in, gla_wg2, peer_wq, peer_keys, peer_u, peer_v, w_br_gla, w_br_dil, w_br_mem, w_out):
    c0 = GLA_COLS
    c1 = c0 + GLA_RANK
    c2 = c1 + DIL_COLS
    c3 = c2 + ATT_COLS
    wa = jnp.pad(w_in[:, c0:c1], ((0, 0), (0, LANES - GLA_RANK))).astype(BF16)
    wg2 = jnp.pad(gla_wg2, ((0, LANES - GLA_RANK), (0, 0))).astype(BF16)
    return dict(
        w_gla=w_in[:, :c0].astype(BF16), wa=wa, wg2=wg2,
        w_dil=w_in[:, c1:c2].astype(BF16), w_qm=w_in[:, c2:c3].astype(BF16), w_gt=w_in[:, c3:].astype(BF16),
        wbg=w_br_gla.astype(BF16), wbd=w_br_dil.astype(BF16), wbm=w_br_mem.astype(BF16), wo=w_out.astype(BF16),
        wq=peer_wq.astype(BF16),
        keys=peer_keys.reshape(PEER_HEADS * 2, PEER_NKEYS, LANES).astype(BF16),
        u=peer_u.astype(BF16), v=peer_v.astype(BF16),
    )


def _layer(x, pos, mem_k, mem_v, s0, caches, p, W):
    B, S, d = x.shape
    nt = B * S
    xt = x.reshape(nt, d)
    n1 = _norm(xt, p["ln1_g"])
    zg = _proj("gla", n1, W["w_gla"])
    lg = _gla_gate(n1, W["wa"], W["wg2"], p["gla_bg"])
    cosf, sinf = _rope_tables(pos)
    cosf = jnp.broadcast_to(cosf[None], (B, S, HD)).reshape(nt, HD)
    sinf = jnp.broadcast_to(sinf[None], (B, S, HD)).reshape(nt, HD)
    hg = jnp.stack([p["dil_qn_g"], p["dil_kn_g"]])
    qm = _proj("mem_q", n1, W["w_qm"], extra=(p["mem_qn_g"].reshape(1, HD),))
    gt = _proj("sigmoid", n1, W["w_gt"])

    prompt = caches is None
    og, s_fin = _gla(zg.reshape(B, S, GLA_COLS), lg.reshape(B, S, GLA_HEADS * GLA_DK), s0,
                     GLA_CHUNK if prompt else S)

    new_bufs = []
    gcols = 3 * ATT_COLS
    if prompt:
        parts = []
        for gi, (wd, r) in enumerate(DIL_GROUPS):
            keep = min(wd, S)
            zr, kc, vc = _proj_dil_group(n1, W["w_dil"][:, gi * gcols:(gi + 1) * gcols], hg, cosf, sinf, B, S, r, keep)
            parts.extend(_dil_prompt(zr, S))
            new_bufs.append((kc.reshape(B, keep, HEADS, HD), vc.reshape(B, keep, HEADS, HD)))
        od = _dil_mix(parts)
    else:
        zd3 = _proj("dil", n1, W["w_dil"], extra=(hg, cosf, sinf)).reshape(B, S, DIL_COLS)
        od, new_bufs = _dil_decode(zd3, caches)

    qm3 = qm.reshape(B, S, ATT_COLS)
    if prompt:
        om = _mem_attn(qm3, mem_k, 512).reshape(nt, ATT_COLS)
    else:
        om = _mem_attn_decode(qm3, mem_k, mem_v)

    h, n2, qh = _merge(xt, og.reshape(nt, d), zg, od, om, gt, p["gla_onorm_g"], W["wbg"], W["wbd"], W["wbm"],
                       W["wo"], p["ln2_g"], W["wq"])
    a, b, gw = _peer_topk(qh, W["keys"])
    y = _peer_mix(n2, _peer_weights(a, b, gw, W["u"].shape[0]), h, W["u"], W["v"])
    return y.reshape(B, S, d), s_fin, new_bufs


def kernel(x_prompt, x_sample, mem_prompt, state_gla, cache_dil_k0, cache_dil_v0, cache_dil_k1, cache_dil_v1, cache_dil_k2, cache_dil_v2, cache_mem_k, cache_mem_v, ln1_g, w_in, gla_wg2, gla_bg, gla_onorm_g, dil_qn_g, dil_kn_g, mem_norm_g, w_mem_kv, mem_qn_g, mem_kn_g, w_br_gla, w_br_dil, w_br_mem, w_out, ln2_g, peer_wq, peer_keys, peer_u, peer_v):
    B, S, d = x_prompt.shape
    Bd, T, _ = x_sample.shape
    p = dict(ln1_g=ln1_g, gla_bg=gla_bg, gla_onorm_g=gla_onorm_g, dil_qn_g=dil_qn_g, dil_kn_g=dil_kn_g,
             mem_qn_g=mem_qn_g, ln2_g=ln2_g)
    W = _prep_weights(w_in, gla_wg2, peer_wq, peer_keys, peer_u, peer_v, w_br_gla, w_br_dil, w_br_mem, w_out)

    M = mem_prompt.shape[1]
    mkv = _proj("mem_kv", _norm(mem_prompt.reshape(B * M, d), mem_norm_g), w_mem_kv.astype(BF16),
                extra=(mem_kn_g.reshape(1, HD),))
    mem_k_p = mkv[:, :ATT_COLS].reshape(B, M, HEADS, HD)
    mem_v_p = mkv[:, ATT_COLS:].reshape(B, M, HEADS, HD)

    s0 = jnp.zeros((B, GLA_HEADS, GLA_DK, GLA_DV), F32)
    mkv3 = mkv.reshape(B, M, 2 * ATT_COLS)
    y_prompt, gla_state_p, bufs_p = _layer(x_prompt, jnp.arange(S, dtype=jnp.int32), mkv3, mkv3, s0, None, p, W)

    caches = ((cache_dil_k0, cache_dil_v0), (cache_dil_k1, cache_dil_v1), (cache_dil_k2, cache_dil_v2))
    pos_s = PAST_LEN + jnp.arange(T, dtype=jnp.int32)
    y_sample, gla_state_s, bufs_s = _layer(x_sample, pos_s, cache_mem_k, cache_mem_v, state_gla, caches, p, W)

    (dk0_p, dv0_p), (dk1_p, dv1_p), (dk2_p, dv2_p) = bufs_p
    (dk0_s, dv0_s), (dk1_s, dv1_s), (dk2_s, dv2_s) = bufs_s
    return (y_prompt, y_sample,
            gla_state_p, dk0_p, dv0_p, dk1_p, dv1_p, dk2_p, dv2_p, mem_k_p, mem_v_p,
            gla_state_s, dk0_s, dv0_s, dk1_s, dv1_s, dk2_s, dv2_s)
```

```python
import functools

import jax
import jax.numpy as jnp
from jax import lax
from jax.experimental import pallas as pl
from jax.experimental.pallas import tpu as pltpu

F32 = jnp.float32
BF16 = jnp.bfloat16

D_MODEL = 1024
PAST_LEN = 8192
GLA_HEADS = 4
GLA_DK = 128
GLA_DV = 256
GLA_RANK = 16
GLA_TAU = 16.0
DIL_GROUPS = ((128, 1), (512, 4), (2048, 16))
N_DIL = 3
HEADS = 4
HD = 128
SPAN = 128
MEM_LEN = 256
ROPE_THETA = 10000.0
PEER_HEADS = 8
PEER_NKEYS = 128
PEER_TOPK = 16
EPS = 1e-6
NEG = -1e30

LANES = 128
SUBLANES = 8
GLA_COLS = 2 * GLA_HEADS * GLA_DK + 2 * GLA_HEADS * GLA_DV
DIL_COLS = N_DIL * 3 * HEADS * HD
ATT_COLS = HEADS * HD
VMEM_LIMIT = 56 * 1024 * 1024

PEER_T = 512
PEER_EB = 2048


def _cparams(*sem):
    return pltpu.CompilerParams(dimension_semantics=sem, vmem_limit_bytes=VMEM_LIMIT)


def _rms(x, g):
    return x * lax.rsqrt(jnp.mean(x * x, axis=-1, keepdims=True) + EPS) * g


def _dot(a, b):
    return jnp.dot(a, b, preferred_element_type=F32)


def _dot_nt(a, b):
    return lax.dot_general(a, b, (((1,), (1,)), ((), ())), preferred_element_type=F32)


def _split3(x):
    hi = x.astype(BF16)
    r1 = x - hi.astype(F32)
    mid = r1.astype(BF16)
    lo = (r1 - mid.astype(F32)).astype(BF16)
    return hi, mid, lo


def _norm_body(x_ref, g_ref, n_ref):
    n_ref[...] = _rms(x_ref[...], g_ref[...]).astype(BF16)


def _norm(x, g, tm=1024):
    nt, d = x.shape
    tm = min(tm, nt)
    return pl.pallas_call(
        _norm_body,
        grid=(nt // tm,),
        in_specs=[pl.BlockSpec((tm, d), lambda i: (i, 0)), pl.BlockSpec((1, d), lambda i: (0, 0))],
        out_specs=pl.BlockSpec((tm, d), lambda i: (i, 0)),
        out_shape=jax.ShapeDtypeStruct((nt, d), BF16),
        compiler_params=_cparams("parallel"),
        name="norm",
    )(x, g.reshape(1, d))


def _proj_body(kind, *refs):
    if kind == "dil":
        n_ref, w_ref, hg_ref, cos_ref, sin_ref, o_ref = refs
    elif kind in ("mem_q", "mem_kv"):
        n_ref, w_ref, hg_ref, o_ref = refs
    else:
        n_ref, w_ref, o_ref = refs
    j = pl.program_id(1)
    z = _dot(n_ref[...], w_ref[...])
    if kind == "gla":
        col = lax.broadcasted_iota(jnp.int32, (1, z.shape[1]), 1) + j * z.shape[1]
        o_ref[...] = z * jnp.where(col < GLA_HEADS * GLA_DK, GLA_DK ** -0.5, 1.0).astype(F32)
    elif kind == "sigmoid":
        o_ref[...] = jax.nn.sigmoid(z)
    elif kind == "mem_q":
        hg = hg_ref[...]
        o_ref[...] = jnp.concatenate([_rms(z[:, h * HD:(h + 1) * HD], hg) for h in range(HEADS)], axis=-1)
    elif kind == "mem_kv":
        @pl.when(j == 0)
        def _():
            hg = hg_ref[...]
            o_ref[...] = jnp.concatenate([_rms(z[:, h * HD:(h + 1) * HD], hg) for h in range(HEADS)], axis=-1)

        @pl.when(j != 0)
        def _():
            o_ref[...] = z
    elif kind == "dil":
        c = j % 3

        @pl.when(c == 2)
        def _():
            o_ref[...] = z

        @pl.when(c != 2)
        def _():
            hg = jnp.where(c == 0, hg_ref[0:1, :], hg_ref[1:2, :])
            cs = cos_ref[...]
            sn = sin_ref[...]
            parts = []
            for h in range(HEADS):
                zh = _rms(z[:, h * HD:(h + 1) * HD], hg)
                parts.append(zh * cs + pltpu.roll(zh, HD // 2, 1) * sn)
            o_ref[...] = jnp.concatenate(parts, axis=-1)
    else:
        raise ValueError(kind)


def _proj(kind, n, w, extra=(), tm=1024, tn=1024):
    nt, d = n.shape
    ncol = w.shape[1]
    tm = min(tm, nt)
    if kind in ("dil", "mem_q", "mem_kv"):
        tn = ATT_COLS
    assert nt % tm == 0 and ncol % tn == 0
    in_specs = [
        pl.BlockSpec((tm, d), lambda i, j: (i, 0)),
        pl.BlockSpec((d, tn), lambda i, j: (0, j)),
    ]
    if kind == "dil":
        in_specs += [
            pl.BlockSpec((2, HD), lambda i, j: (0, 0)),
            pl.BlockSpec((tm, HD), lambda i, j: (i, 0)),
            pl.BlockSpec((tm, HD), lambda i, j: (i, 0)),
        ]
    elif kind in ("mem_q", "mem_kv"):
        in_specs += [pl.BlockSpec((1, HD), lambda i, j: (0, 0))]
    return pl.pallas_call(
        functools.partial(_proj_body, kind),
        grid=(nt // tm, ncol // tn),
        in_specs=in_specs,
        out_specs=pl.BlockSpec((tm, tn), lambda i, j: (i, j)),
        out_shape=jax.ShapeDtypeStruct((nt, ncol), F32),
        compiler_params=_cparams("parallel", "arbitrary"),
        name="proj_" + kind,
    )(n, w, *extra)


def _gate_body(n_ref, wa_ref, wg2_ref, bg_ref, o_ref):
    a = _dot(n_ref[...], wa_ref[...])
    pre = _dot(a.astype(BF16), wg2_ref[...]) + bg_ref[...]
    o_ref[...] = jax.nn.log_sigmoid(pre) / GLA_TAU


def _gla_gate(n, wa, wg2, bg, tm=1024):
    nt, d = n.shape
    tm = min(tm, nt)
    ncol = wg2.shape[1]
    return pl.pallas_call(
        _gate_body,
        grid=(nt // tm,),
        in_specs=[
            pl.BlockSpec((tm, d), lambda i: (i, 0)),
            pl.BlockSpec((d, LANES), lambda i: (0, 0)),
            pl.BlockSpec((LANES, ncol), lambda i: (0, 0)),
            pl.BlockSpec((1, ncol), lambda i: (0, 0)),
        ],
        out_specs=pl.BlockSpec((tm, ncol), lambda i: (i, 0)),
        out_shape=jax.ShapeDtypeStruct((nt, ncol), F32),
        compiler_params=_cparams("parallel"),
        name="gla_gate",
    )(n, wa, wg2, bg.reshape(1, ncol))


GLA_CHUNK = 128
GLA_SUB = 16


def _gla_intra(q, k, b, qs_scr, bs_scr, h, n_sub):
    P, SB = GLA_CHUNK, GLA_SUB
    rowp = lax.broadcasted_iota(jnp.int32, (P, 1), 0)
    qparts, kparts = [], []
    for j in range(n_sub - 1):
        lo, hi = SB * j, SB * (j + 1)
        be = b[hi - 1:hi, :]
        qparts.append(jnp.where(rowp >= hi, q * jnp.exp(b - be), 0.0).astype(BF16))
        kblk = (k[lo:hi] * jnp.exp(be - b[lo:hi])).astype(BF16)
        pieces = [kblk]
        if lo:
            pieces.insert(0, jnp.zeros((lo, k.shape[1]), BF16))
        pieces.append(jnp.zeros((P - hi, k.shape[1]), BF16))
        kparts.append(jnp.concatenate(pieces, axis=0))
    if qparts:
        a_off = _dot_nt(jnp.concatenate(qparts, axis=1), jnp.concatenate(kparts, axis=1))
    else:
        a_off = jnp.zeros((P, P), F32)
    HS = SB // 2
    lane = lax.broadcasted_iota(jnp.int32, (HS, P), 1)
    srow = lax.broadcasted_iota(jnp.int32, (HS, P), 0)
    tiles = []
    for j in range(P // SB):
        if j >= n_sub:
            tiles.append(jnp.zeros((SB, P), F32))
            continue
        lo, mid, hi = SB * j, SB * j + HS, SB * (j + 1)
        halves = []
        for s0 in (lo, mid):
            kj, bj = k[s0:s0 + HS], b[s0:s0 + HS]
            at = jnp.zeros((HS, P), F32)
            for t in range(s0, hi):
                bt = bs_scr[h, pl.ds(t, 1), :]
                qt = qs_scr[h, pl.ds(t, 1), :]
                colv = jnp.sum(qt * kj * jnp.exp(jnp.minimum(bt - bj, 0.0)), axis=-1, keepdims=True)
                at = jnp.where(lane == t, colv, at)
            halves.append(jnp.where(srow + s0 <= lane, at, 0.0))
        tiles.extend(halves)
    return a_off + jnp.transpose(jnp.concatenate(tiles, axis=0))


def _gla_body(tc, q_ref, k_ref, v_ref, g_ref, s0_ref, o_ref, sf_ref, s_scr, qp, kp, vp, gp, qs_scr, bs_scr):
    c = pl.program_id(1)
    P = GLA_CHUNK
    H, dk, dv = GLA_HEADS, GLA_DK, GLA_DV
    n_sub = -(-tc // GLA_SUB)

    @pl.when(c == 0)
    def _():
        s_scr[...] = s0_ref[...]

    if tc == P:
        q, k, v, g = q_ref[...], k_ref[...], v_ref[...], g_ref[...]
    else:
        for pad, ref in ((qp, q_ref), (kp, k_ref), (vp, v_ref), (gp, g_ref)):
            pad[...] = jnp.zeros(pad.shape, F32)
            pad[0:tc, :] = ref[...]
        q, k, v, g = qp[...], kp[...], vp[...], gp[...]

    row = lax.broadcasted_iota(jnp.int32, (P, P), 0)
    col = lax.broadcasted_iota(jnp.int32, (P, P), 1)
    trib = (row >= col).astype(BF16)
    g1, g2, g3 = _split3(g)
    b = _dot(trib, g1) + _dot(trib, g2) + _dot(trib, g3)
    b_end = b[P - 1:P, :]
    qe = (q * jnp.exp(b)).astype(BF16)
    kd = k * jnp.exp(b_end - b)
    vb = v.astype(BF16)
    for h in range(H):
        ks = slice(h * dk, (h + 1) * dk)
        qs_scr[h] = q[:, ks]
        bs_scr[h] = b[:, ks]
    for h in range(H):
        ks = slice(h * dk, (h + 1) * dk)
        vs = slice(h * dv, (h + 1) * dv)
        a = _gla_intra(q[:, ks], k[:, ks], b[:, ks], qs_scr, bs_scr, h, n_sub)
        s = s_scr[h]
        o = _dot(qe[:, ks], s.astype(BF16)) + _dot(a.astype(BF16), vb[:, vs])
        o_ref[:, vs] = o[0:tc, :]
        decay = jnp.exp(jnp.sum(jnp.transpose(g[:, ks]), axis=1, keepdims=True))
        s_new = decay * s + _dot(jnp.transpose(kd[:, ks]).astype(BF16), vb[:, vs])
        s_scr[h] = s_new

        @pl.when(c == pl.num_programs(1) - 1)
        def _():
            sf_ref[h] = s_new


def _gla(zg, lg, s0, tc):
    B, S, _ = zg.shape
    H, dk, dv = GLA_HEADS, GLA_DK, GLA_DV
    assert S % tc == 0
    P = GLA_CHUNK
    qk = H * dk
    st = pl.BlockSpec((None, H, dk, dv), lambda b, c: (b, 0, 0, 0))
    return pl.pallas_call(
        functools.partial(_gla_body, tc),
        grid=(B, S // tc),
        in_specs=[
            pl.BlockSpec((None, tc, qk), lambda b, c: (b, c, 0)),
            pl.BlockSpec((None, tc, qk), lambda b, c: (b, c, 1)),
            pl.BlockSpec((None, tc, H * dv), lambda b, c: (b, c, 2 * qk // (H * dv))),
            pl.BlockSpec((None, tc, qk), lambda b, c: (b, c, 0)),
            st,
        ],
        out_specs=[pl.BlockSpec((None, tc, H * dv), lambda b, c: (b, c, 0)), st],
        out_shape=[
            jax.ShapeDtypeStruct((B, S, H * dv), F32),
            jax.ShapeDtypeStruct((B, H, dk, dv), F32),
        ],
        scratch_shapes=[
            pltpu.VMEM((H, dk, dv), F32),
            pltpu.VMEM((P, qk), F32), pltpu.VMEM((P, qk), F32), pltpu.VMEM((P, H * dv), F32), pltpu.VMEM((P, qk), F32),
            pltpu.VMEM((H, P, dk), F32), pltpu.VMEM((H, P, dk), F32),
        ],
        compiler_params=_cparams("parallel", "arbitrary"),
        name="gla",
    )(zg, zg, zg, lg, s0)


def _proj_dil_body(r, spb, nkb, n_ref, w_ref, hg_ref, cos_ref, sin_ref, o_ref, kc_ref, vc_ref, z_scr):
    i = pl.program_id(0)
    j = pl.program_id(1)
    z = _dot(n_ref[...], w_ref[...])

    @pl.when(j == 2)
    def _():
        for h in range(HEADS):
            z_scr[h] = z[:, h * HD:(h + 1) * HD]

    @pl.when(j != 2)
    def _():
        hg = jnp.where(j == 0, hg_ref[0:1, :], hg_ref[1:2, :])
        cs = cos_ref[...]
        sn = sin_ref[...]
        zs = [z[:, h * HD:(h + 1) * HD] for h in range(HEADS)]
        ms = [jnp.mean(zh * zh, axis=-1, keepdims=True) for zh in zs]
        ns = [zh * lax.rsqrt(m + EPS) * hg for zh, m in zip(zs, ms)]
        rs = [pltpu.roll(nh, HD // 2, 1) for nh in ns]
        for h in range(HEADS):
            z_scr[h] = ns[h] * cs + rs[h] * sn

    tm = z_scr.shape[1]
    rows = tm // r
    for rho in range(r):
        for h in range(HEADS):
            zh = z_scr[h, pl.ds(rho, rows, stride=r), :] if r > 1 else z_scr[h]
            o_ref[rho, :, h * HD:(h + 1) * HD] = zh

    kb = kc_ref.shape[0] // HEADS
    for jj, dst in ((1, kc_ref), (2, vc_ref)):
        @pl.when(jnp.logical_and(j == jj, i % spb >= spb - nkb))
        def _():
            for h in range(HEADS):
                dst[pl.ds(h, kb, stride=HEADS), :] = z_scr[h, tm - kb:tm, :]


def _proj_dil_group(n, w, hg, cosf, sinf, B, S, r, keep, tm=1024):
    nt, d = n.shape
    assert S % tm == 0 and tm % (r * SUBLANES) == 0
    spb = S // tm
    kb = min(keep, tm)
    assert keep % kb == 0 and keep <= S
    nkb = keep // kb
    cache_spec = pl.BlockSpec((None, kb * HEADS, HD), lambda i, j: (i // spb, jnp.maximum(i % spb - (spb - nkb), 0), 0))
    cache_shape = jax.ShapeDtypeStruct((B, keep * HEADS, HD), F32)
    return pl.pallas_call(
        functools.partial(_proj_dil_body, r, spb, nkb),
        grid=(nt // tm, 3),
        in_specs=[
            pl.BlockSpec((tm, d), lambda i, j: (i, 0)),
            pl.BlockSpec((d, ATT_COLS), lambda i, j: (0, j)),
            pl.BlockSpec((2, HD), lambda i, j: (0, 0)),
            pl.BlockSpec((tm, HD), lambda i, j: (i, 0)),
            pl.BlockSpec((tm, HD), lambda i, j: (i, 0)),
        ],
        out_specs=[pl.BlockSpec((None, r, tm // r, ATT_COLS), lambda i, j: (i // spb, 0, i % spb, j)),
                   cache_spec, cache_spec],
        out_shape=[jax.ShapeDtypeStruct((B, r, S // r, 3 * ATT_COLS), F32), cache_shape, cache_shape],
        scratch_shapes=[pltpu.VMEM((HEADS, tm, HD), F32)],
        compiler_params=_cparams("arbitrary", "arbitrary"),
        name="proj_dil_r%d" % r,
    )(n, w, hg, cosf, sinf)


DIL_QBLOCKS = 2


def _dil_body(r, q_ref, kc_ref, kp_ref, vc_ref, vp_ref, o_ref, l_ref):
    i = pl.program_id(1)
    rho = pl.program_id(2)
    scale = HD ** -0.5
    QB = q_ref.shape[0]
    mq = lax.broadcasted_iota(jnp.int32, (QB, QB), 0)
    mk = lax.broadcasted_iota(jnp.int32, (QB, QB), 1)
    mask_c = jnp.logical_and(mk <= mq, mq - mk <= SPAN)
    pq = lax.broadcasted_iota(jnp.int32, (QB, SPAN), 0)
    pk = lax.broadcasted_iota(jnp.int32, (QB, SPAN), 1)
    mask_p = jnp.logical_and(pk >= pq, i > 0)
    sls = [slice(h * HD, (h + 1) * HD) for h in range(HEADS)]
    qs = [q_ref[:, sl].astype(BF16) for sl in sls]
    scs = [jnp.where(mask_c, _dot_nt(q, kc_ref[:, sl].astype(BF16)) * scale, NEG) for q, sl in zip(qs, sls)]
    sps = [jnp.where(mask_p, _dot_nt(q, kp_ref[:, sl].astype(BF16)) * scale, NEG) for q, sl in zip(qs, sls)]
    ms = [jnp.maximum(jnp.max(sc, axis=-1, keepdims=True), jnp.max(sp, axis=-1, keepdims=True))
          for sc, sp in zip(scs, sps)]
    ecs = [jnp.exp(sc - m) for sc, m in zip(scs, ms)]
    eps = [jnp.exp(sp - m) for sp, m in zip(sps, ms)]
    dens = [jnp.sum(ec, axis=-1, keepdims=True) + jnp.sum(ep, axis=-1, keepdims=True) for ec, ep in zip(ecs, eps)]
    for h, sl in enumerate(sls):
        o = (_dot((ecs[h] / dens[h]).astype(BF16), vc_ref[:, sl].astype(BF16))
             + _dot((eps[h] / dens[h]).astype(BF16), vp_ref[:, sl].astype(BF16)))
        lse = jnp.broadcast_to(ms[h] + jnp.log(dens[h]), (QB, HD))
        if r == 1:
            o_ref[h] = o
            l_ref[h] = lse
        else:
            o_ref[h, pl.ds(rho, QB, stride=r), :] = o
            l_ref[h, pl.ds(rho, QB, stride=r), :] = lse


def _dil_prompt(zr, S):
    B, r, L, _ = zr.shape
    nq = DIL_QBLOCKS
    QB = nq * SPAN
    assert L % QB == 0

    def cur(c):
        return pl.BlockSpec((None, None, QB, ATT_COLS), lambda b, i, p: (b, p, i, c))

    def prev(c):
        return pl.BlockSpec((None, None, SPAN, ATT_COLS), lambda b, i, p: (b, p, jnp.maximum(nq * i - 1, 0), c))

    out_spec = pl.BlockSpec((None, HEADS, QB * r, HD), lambda b, i, p: (b, 0, i, 0))
    return pl.pallas_call(
        functools.partial(_dil_body, r),
        grid=(B, L // QB, r),
        in_specs=[cur(0), cur(1), prev(1), cur(2), prev(2)],
        out_specs=[out_spec, out_spec],
        out_shape=[jax.ShapeDtypeStruct((B, HEADS, S, HD), F32)] * 2,
        compiler_params=_cparams("parallel", "parallel", "arbitrary"),
        name="dil_prompt_r%d" % r,
    )(zr, zr, zr, zr, zr)


def _dil_mix_body(o0, l0, o1, l1, o2, l2, od_ref):
    for h in range(HEADS):
        a0, a1, a2 = l0[h], l1[h], l2[h]
        m = jnp.maximum(jnp.maximum(a0, a1), a2)
        e0, e1, e2 = jnp.exp(a0 - m), jnp.exp(a1 - m), jnp.exp(a2 - m)
        den = e0 + e1 + e2
        od_ref[:, h * HD:(h + 1) * HD] = (e0 / den) * o0[h] + (e1 / den) * o1[h] + (e2 / den) * o2[h]


def _dil_mix(parts, tm=1024):
    B, _, S, _ = parts[0].shape
    spb = S // tm
    spec = pl.BlockSpec((None, HEADS, tm, HD), lambda i: (i // spb, 0, i % spb, 0))
    return pl.pallas_call(
        _dil_mix_body,
        grid=(B * spb,),
        in_specs=[spec] * 6,
        out_specs=pl.BlockSpec((tm, ATT_COLS), lambda i: (i, 0)),
        out_shape=jax.ShapeDtypeStruct((B * S, ATT_COLS), F32),
        compiler_params=_cparams("parallel"),
        name="dil_mix",
    )(*parts)


def _masked_attn(q, k, v, kn, vn, mask_b, mask_n, scale):
    sb = jnp.where(mask_b, _dot_nt(q, k) * scale, NEG)
    sn = jnp.where(mask_n, _dot_nt(q, kn) * scale, NEG)
    m = jnp.maximum(jnp.max(sb, axis=-1, keepdims=True), jnp.max(sn, axis=-1, keepdims=True))
    eb = jnp.exp(sb - m)
    en = jnp.exp(sn - m)
    den = jnp.sum(eb, axis=-1, keepdims=True) + jnp.sum(en, axis=-1, keepdims=True)
    o = _dot((eb / den).astype(BF16), v) + _dot((en / den).astype(BF16), vn)
    return o, m + jnp.log(den)


def _dil_dec_body(T, q_ref, kn_ref, vn_ref, k0, v0, k1, v1, k2, v2, od_ref, nk0, nv0, nk1, nv1, nk2, nv2):
    scale = HD ** -0.5
    R = T * HEADS
    caches = ((k0, v0, nk0, nv0), (k1, v1, nk1, nv1), (k2, v2, nk2, nv2))
    rn = lax.broadcasted_iota(jnp.int32, (R, R), 0)
    cn = lax.broadcasted_iota(jnp.int32, (R, R), 1)
    outs, lses = [], []
    for gi, (_, r) in enumerate(DIL_GROUPS):
        kc, vc, nkc, nvc = caches[gi]
        rows = kc.shape[0]
        for src, new, dst in ((kc, kn_ref, nkc), (vc, vn_ref, nvc)):
            dst[pl.ds(0, rows - R), :] = src[pl.ds(R, rows - R), :]
            dst[pl.ds(rows - R, R), :] = new[gi]
        if r <= T:
            n = rows
            kb = kc[...].astype(BF16)
            vb = vc[...].astype(BF16)
        else:
            n = rows // (r * HEADS) * R
            kb = kc[...].reshape(rows // (r * HEADS), r * HEADS, HD)[:, 0:R, :].reshape(n, HD).astype(BF16)
            vb = vc[...].reshape(rows // (r * HEADS), r * HEADS, HD)[:, 0:R, :].reshape(n, HD).astype(BF16)
        rb = lax.broadcasted_iota(jnp.int32, (R, n), 0)
        cb = lax.broadcasted_iota(jnp.int32, (R, n), 1)
        if r == 1:
            mask_b = jnp.logical_and(cb % HEADS == rb % HEADS, cb // HEADS >= rb // HEADS)
            mask_n = jnp.logical_and(cn % HEADS == rn % HEADS, cn // HEADS <= rn // HEADS)
        else:
            mask_b = cb % R == rb
            mask_n = cn == rn
        o, lse = _masked_attn(q_ref[gi].astype(BF16), kb, vb, kn_ref[gi].astype(BF16), vn_ref[gi].astype(BF16),
                              mask_b, mask_n, scale)
        outs.append(o)
        lses.append(lse)
    l0, l1, l2 = lses
    m = jnp.maximum(jnp.maximum(l0, l1), l2)
    e0, e1, e2 = jnp.exp(l0 - m), jnp.exp(l1 - m), jnp.exp(l2 - m)
    den = e0 + e1 + e2
    od_ref[...] = (e0 / den) * outs[0] + (e1 / den) * outs[1] + (e2 / den) * outs[2]


def _dil_decode(zd, caches):
    Bd, T, _ = zd.shape
    R = T * HEADS
    z6 = zd.reshape(Bd, T, N_DIL, 3, HEADS, HD)
    qkv = [jnp.transpose(z6[:, :, :, c], (0, 2, 1, 3, 4)).reshape(Bd, N_DIL, R, HD) for c in range(3)]
    new = pl.BlockSpec((None, N_DIL, R, HD), lambda b: (b, 0, 0, 0))
    args, specs = list(qkv), [new, new, new]
    out_specs = [pl.BlockSpec((None, R, HD), lambda b: (b, 0, 0))]
    out_shape = [jax.ShapeDtypeStruct((Bd, R, HD), F32)]
    for (wd, r), (kc, vc) in zip(DIL_GROUPS, caches):
        assert kc.shape[1] == wd and wd > T and R % SUBLANES == 0 and (r <= T or wd % r == 0)
        for a in (kc, vc):
            spec = pl.BlockSpec((None, wd * HEADS, HD), lambda b: (b, 0, 0))
            args.append(a.reshape(Bd, wd * HEADS, HD))
            specs.append(spec)
            out_specs.append(spec)
            out_shape.append(jax.ShapeDtypeStruct((Bd, wd * HEADS, HD), F32))
    od, *rolled = pl.pallas_call(
        functools.partial(_dil_dec_body, T),
        grid=(Bd,),
        in_specs=specs,
        out_specs=out_specs,
        out_shape=out_shape,
        compiler_params=_cparams("parallel"),
        name="dil_decode",
    )(*args)
    bufs = [(rolled[2 * gi].reshape(Bd, wd, HEADS, HD), rolled[2 * gi + 1].reshape(Bd, wd, HEADS, HD))
            for gi, (wd, _) in enumerate(DIL_GROUPS)]
    return od.reshape(Bd * T, ATT_COLS), bufs


def _mem_body(q_ref, k_ref, v_ref, o_ref):
    scale = HD ** -0.5
    for h in range(HEADS):
        sl = slice(h * HD, (h + 1) * HD)
        s = _dot_nt(q_ref[:, sl].astype(BF16), k_ref[:, sl].astype(BF16)) * scale
        e = jnp.exp(s - jnp.max(s, axis=-1, keepdims=True))
        p = e / jnp.sum(e, axis=-1, keepdims=True)
        o_ref[:, sl] = _dot(p.astype(BF16), v_ref[:, sl].astype(BF16))


def _mem_attn(qm, mkv, tq):
    B, S, _ = qm.shape
    M = mkv.shape[1]
    return pl.pallas_call(
        _mem_body,
        grid=(B, S // tq),
        in_specs=[
            pl.BlockSpec((None, tq, ATT_COLS), lambda b, i: (b, i, 0)),
            pl.BlockSpec((None, M, ATT_COLS), lambda b, i: (b, 0, 0)),
            pl.BlockSpec((None, M, ATT_COLS), lambda b, i: (b, 0, 1)),
        ],
        out_specs=pl.BlockSpec((None, tq, ATT_COLS), lambda b, i: (b, i, 0)),
        out_shape=jax.ShapeDtypeStruct((B, S, ATT_COLS), F32),
        compiler_params=_cparams("parallel", "arbitrary"),
        name="mem_attn",
    )(qm, mkv, mkv)


def _mem_dec_body(q_ref, k_ref, v_ref, o_ref):
    R, n = q_ref.shape[0], k_ref.shape[0]
    rb = lax.broadcasted_iota(jnp.int32, (R, n), 0)
    cb = lax.broadcasted_iota(jnp.int32, (R, n), 1)
    s = _dot_nt(q_ref[...].astype(BF16), k_ref[...].astype(BF16)) * (HD ** -0.5)
    s = jnp.where(cb % HEADS == rb % HEADS, s, -jnp.inf)
    e = jnp.exp(s - jnp.max(s, axis=-1, keepdims=True))
    p = e / jnp.sum(e, axis=-1, keepdims=True)
    o_ref[...] = _dot(p.astype(BF16), v_ref[...].astype(BF16))


def _mem_attn_decode(qm, mk, mv):
    Bd, T, _ = qm.shape
    M = mk.shape[1]
    R = T * HEADS
    kv = pl.BlockSpec((None, M * HEADS, HD), lambda b: (b, 0, 0))
    o = pl.pallas_call(
        _mem_dec_body,
        grid=(Bd,),
        in_specs=[pl.BlockSpec((None, R, HD), lambda b: (b, 0, 0)), kv, kv],
        out_specs=pl.BlockSpec((None, R, HD), lambda b: (b, 0, 0)),
        out_shape=jax.ShapeDtypeStruct((Bd, R, HD), F32),
        compiler_params=_cparams("parallel"),
        name="mem_attn_decode",
    )(qm.reshape(Bd, R, HD), mk.reshape(Bd, M * HEADS, HD), mv.reshape(Bd, M * HEADS, HD))
    return o.reshape(Bd * T, ATT_COLS)


def _merge_body(x_ref, og_ref, r_ref, od_ref, om_ref, gt_ref, ong_ref, wbg_ref, wbd_ref, wbm_ref, wo_ref,
                ln2_ref, wq_ref, h_ref, n2_ref, qh_ref):
    og = og_ref[...]
    r = r_ref[...]
    ong = ong_ref[...]
    parts = []
    for h in range(GLA_HEADS):
        sl = slice(h * GLA_DV, (h + 1) * GLA_DV)
        parts.append(_rms(og[:, sl], ong) * jax.nn.silu(r[:, sl]))
    br_gla = _dot(jnp.concatenate(parts, axis=-1).astype(BF16), wbg_ref[...])
    br_dil = _dot(od_ref[...].astype(BF16), wbd_ref[...])
    br_mem = _dot(om_ref[...].astype(BF16), wbm_ref[...])
    d = D_MODEL
    merged = gt_ref[:, 0:d] * br_gla + gt_ref[:, d:2 * d] * br_dil + gt_ref[:, 2 * d:3 * d] * br_mem
    hres = x_ref[...] + _dot(merged.astype(BF16), wo_ref[...])
    h_ref[...] = hres
    n2 = _rms(hres, ln2_ref[...]).astype(BF16)
    n2_ref[...] = n2
    qh_ref[...] = _dot(n2, wq_ref[...])


def _merge(x, og, zg, od, om, gt, ong, wbg, wbd, wbm, wo, ln2, wq, tm=256):
    nt, d = x.shape
    tm = min(tm, nt)
    nq = wq.shape[1]
    rblk = (2 * GLA_HEADS * GLA_DK + GLA_HEADS * GLA_DV) // d

    def tok(cols, blk=0):
        return pl.BlockSpec((tm, cols), lambda i: (i, blk))

    def const(shape):
        return pl.BlockSpec(shape, lambda i: (0, 0), pipeline_mode=pl.Buffered(1))

    return pl.pallas_call(
        _merge_body,
        grid=(nt // tm,),
        in_specs=[
            tok(d), tok(d), tok(d, rblk), tok(ATT_COLS), tok(ATT_COLS), tok(3 * d),
            const((1, GLA_DV)), const(wbg.shape), const(wbd.shape), const(wbm.shape), const(wo.shape),
            const((1, d)), const(wq.shape),
        ],
        out_specs=[tok(d), tok(d), tok(nq)],
        out_shape=[
            jax.ShapeDtypeStruct((nt, d), F32),
            jax.ShapeDtypeStruct((nt, d), BF16),
            jax.ShapeDtypeStruct((nt, nq), F32),
        ],
        compiler_params=_cparams("parallel"),
        name="merge",
    )(x, og, zg, od, om, gt, ong.reshape(1, GLA_DV), wbg, wbd, wbm, wo, ln2.reshape(1, d), wq)


PEER_CAND_GROUPS = 10


def _peer_cand_layout(tp):
    r = lax.broadcasted_iota(jnp.int32, (PEER_CAND_GROUPS * SUBLANES, tp), 0)
    grp = r // SUBLANES
    p = r % SUBLANES
    K = PEER_TOPK
    k1 = jnp.where(grp < 2, 0, jnp.where(grp < 9, grp - 1, SUBLANES + p))
    k2 = jnp.where(grp < 2, r, jnp.where(grp < 9, p, 0))
    valid = (k1 + 1) * (k2 + 1) <= K
    return (k1 * K + k2).astype(F32), valid


def _peer_cands(x1, x2):
    top, bot = x2[0:SUBLANES], x2[SUBLANES:2 * SUBLANES]
    g1 = [jnp.broadcast_to(x1[0:1], top.shape)] * 2 + [jnp.broadcast_to(x1[k:k + 1], top.shape) for k in range(1, 8)]
    g1.append(x1[SUBLANES:2 * SUBLANES])
    g2 = [top, bot] + [top] * 7 + [jnp.broadcast_to(x2[0:1], top.shape)]
    return jnp.concatenate(g1, axis=0), jnp.concatenate(g2, axis=0)


def _peer_topk_body(qh_ref, keys_ref, a_ref, b_ref, gw_ref, s_scr, sv_scr, si_scr, cand_scr, eid_scr, sc_scr, ev_scr):
    tp = qh_ref.shape[0]
    K = PEER_TOPK
    NK = PEER_NKEYS
    NH = PEER_HEADS
    rowf = lax.broadcasted_iota(jnp.int32, (NK, tp), 0).astype(F32)
    pos, valid = _peer_cand_layout(tp)
    for i in range(2 * NH):
        s_scr[i] = _dot_nt(keys_ref[i], qh_ref[:, i * LANES:(i + 1) * LANES].astype(BF16))

    def step1(k, carry):
        for i in range(2 * NH):
            s = s_scr[i]
            mx = jnp.max(s, axis=0, keepdims=True)
            idx = jnp.min(jnp.where(s == mx, rowf, float(NK)), axis=0, keepdims=True)
            s_scr[i] = jnp.where(rowf == idx, -jnp.inf, s)
            sv_scr[i, pl.ds(k, 1), :] = mx
            si_scr[i, pl.ds(k, 1), :] = idx
        return carry

    lax.fori_loop(0, K, step1, 0)

    for h in range(NH):
        c1, c2 = _peer_cands(sv_scr[2 * h], sv_scr[2 * h + 1])
        e1, e2 = _peer_cands(si_scr[2 * h], si_scr[2 * h + 1])
        cand_scr[h] = jnp.where(valid, c1 + c2, -jnp.inf)
        eid_scr[h] = e1 * float(NK) + e2

    def step2(k, carry):
        for h in range(NH):
            cand = cand_scr[h]
            mx = jnp.max(cand, axis=0, keepdims=True)
            first = jnp.min(jnp.where(cand == mx, pos, float(K * K)), axis=0, keepdims=True)
            hit = pos == first
            ev_scr[h, pl.ds(k, 1), :] = jnp.max(jnp.where(hit, eid_scr[h], -1.0), axis=0, keepdims=True)
            sc_scr[h, pl.ds(k, 1), :] = mx
            cand_scr[h] = jnp.where(hit, -jnp.inf, cand)
        return carry

    lax.fori_loop(0, K, step2, 0)

    a_rows, b_rows, g_rows = [], [], []
    for h in range(NH):
        sc, ev = sc_scr[h], ev_scr[h]
        e = jnp.exp(sc - jnp.max(sc, axis=0, keepdims=True))
        g_rows.append(e / jnp.sum(e, axis=0, keepdims=True))
        a = jnp.floor(ev * (1.0 / NK))
        a_rows.append(a)
        b_rows.append(ev - a * float(NK))
    a_ref[...] = jnp.transpose(jnp.concatenate(a_rows, axis=0))
    b_ref[...] = jnp.transpose(jnp.concatenate(b_rows, axis=0))
    gw_ref[...] = jnp.transpose(jnp.concatenate(g_rows, axis=0))


def _peer_topk(qh, keys, tp=128):
    nt, nq = qh.shape
    tp = min(tp, nt)
    spec = pl.BlockSpec((tp, LANES), lambda i: (i, 0))
    shp = jax.ShapeDtypeStruct((nt, LANES), F32)
    K, NH = PEER_TOPK, PEER_HEADS
    ncand = PEER_CAND_GROUPS * SUBLANES
    return pl.pallas_call(
        _peer_topk_body,
        grid=(nt // tp,),
        in_specs=[
            pl.BlockSpec((tp, nq), lambda i: (i, 0)),
            pl.BlockSpec(keys.shape, lambda i: (0, 0, 0)),
        ],
        out_specs=[spec, spec, spec],
        out_shape=[shp, shp, shp],
        scratch_shapes=[
            pltpu.VMEM((2 * NH, PEER_NKEYS, tp), F32), pltpu.VMEM((2 * NH, K, tp), F32), pltpu.VMEM((2 * NH, K, tp), F32),
            pltpu.VMEM((NH, ncand, tp), F32), pltpu.VMEM((NH, ncand, tp), F32),
            pltpu.VMEM((NH, K, tp), F32), pltpu.VMEM((NH, K, tp), F32),
        ],
        compiler_params=_cparams("parallel"),
        name="peer_topk",
    )(qh, keys)


def _peer_w_body(a_ref, b_ref, gw_ref, w_ref, scr):
    tw = a_ref.shape[0]
    NK = PEER_NKEYS
    stride = scr.shape[0] // NK
    sub = lax.broadcasted_iota(jnp.int32, (NK, LANES), 0).astype(F32)

    zero = jnp.zeros((NK, LANES), BF16)

    def tok2(p, carry):
        oas, gbs = [], []
        for t in (2 * p, 2 * p + 1):
            arow = a_ref[pl.ds(t, 1), :]
            brow = b_ref[pl.ds(t, 1), :]
            grow = gw_ref[pl.ds(t, 1), :]
            oas.append((arow == sub).astype(BF16))
            gbs.append(jnp.where(brow == sub, grow, 0.0).astype(BF16))
        lhs = jnp.concatenate(oas, axis=1)
        rhs = jnp.concatenate([jnp.concatenate([gbs[0], zero], axis=1),
                               jnp.concatenate([zero, gbs[1]], axis=1)], axis=0)
        w2 = _dot_nt(lhs, rhs)
        scr[pl.ds(2 * p, NK, stride=stride), :] = w2[:, 0:LANES]
        scr[pl.ds(2 * p + 1, NK, stride=stride), :] = w2[:, LANES:2 * LANES]
        return carry

    lax.fori_loop(0, tw // 2, tok2, 0, unroll=32)
    for i1 in range(NK):
        w_ref[:, i1 * NK:(i1 + 1) * NK] = scr[i1 * stride:i1 * stride + tw, :].astype(BF16)


def _peer_weights(a, b, gw, n_experts, tw=128):
    nt = a.shape[0]
    tw = min(tw, nt)
    stride = tw + SUBLANES
    tok = pl.BlockSpec((tw, LANES), lambda i: (i, 0))
    return pl.pallas_call(
        _peer_w_body,
        grid=(nt // tw,),
        in_specs=[tok, tok, tok],
        out_specs=pl.BlockSpec((tw, n_experts), lambda i: (i, 0)),
        out_shape=jax.ShapeDtypeStruct((nt, n_experts), BF16),
        scratch_shapes=[pltpu.VMEM((PEER_NKEYS * stride, LANES), F32)],
        compiler_params=_cparams("parallel"),
        name="peer_weights",
    )(a, b, gw)


def _peer_mix_body(n2_ref, w_ref, h_ref, u_ref, v_ref, y_ref, acc):
    j = pl.program_id(1)

    @pl.when(j == 0)
    def _():
        acc[...] = jnp.zeros(acc.shape, F32)

    hpre = _dot_nt(n2_ref[...], u_ref[...])
    gelu = 0.5 * hpre * (1.0 + lax.erf(hpre * (2.0 ** -0.5)))
    act = (gelu * w_ref[...].astype(F32)).astype(BF16)
    acc[...] += _dot(act, v_ref[...])

    @pl.when(j == pl.num_programs(1) - 1)
    def _():
        y_ref[...] = h_ref[...] + acc[...]


def _peer_mix(n2, w, h, u, v):
    nt, d = n2.shape
    T = min(PEER_T, nt)
    assert nt % T == 0 and u.shape[0] % PEER_EB == 0
    tok = pl.BlockSpec((T, d), lambda i, j: (i, 0))
    tab = pl.BlockSpec((PEER_EB, d), lambda i, j: (j, 0))
    return pl.pallas_call(
        _peer_mix_body,
        grid=(nt // T, u.shape[0] // PEER_EB),
        in_specs=[tok, pl.BlockSpec((T, PEER_EB), lambda i, j: (i, j)), tok, tab, tab],
        out_specs=tok,
        out_shape=jax.ShapeDtypeStruct((nt, d), F32),
        scratch_shapes=[pltpu.VMEM((T, d), F32)],
        compiler_params=_cparams("parallel", "arbitrary"),
        name="peer_mix",
    )(n2, w, h, u, v)


def _rope_tables(pos):
    half = HD // 2
    inv = ROPE_THETA ** (-jnp.arange(half, dtype=F32) / half)
    ang = pos.astype(F32)[:, None] * inv[None, :]
    cos, sin = jnp.cos(ang), jnp.sin(ang)
    return jnp.concatenate([cos, cos], axis=-1), jnp.concatenate([-sin, sin], axis=-1)


def _prep_weights(w_in, gla_wg2, peer_wq, peer_keys, peer_u, peer_v, w_br_gla, w_br_dil, w_br_mem, w_out):
    c0 = GLA_COLS
    c1 = c0 + GLA_RANK
    c2 = c1 + DIL_COLS
    c3 = c2 + ATT_COLS
    wa = jnp.pad(w_in[:, c0:c1], ((0, 0), (0, LANES - GLA_RANK))).astype(BF16)
    wg2 = jnp.pad(gla_wg2, ((0, LANES - GLA_RANK), (0, 0))).astype(BF16)
    return dict(
        w_gla=w_in[:, :c0].astype(BF16), wa=wa, wg2=wg2,
        w_dil=w_in[:, c1:c2].astype(BF16), w_qm=w_in[:, c2:c3].astype(BF16), w_gt=w_in[:, c3:].astype(BF16),
        wbg=w_br_gla.astype(BF16), wbd=w_br_dil.astype(BF16), wbm=w_br_mem.astype(BF16), wo=w_out.astype(BF16),
        wq=peer_wq.astype(BF16),
        keys=peer_keys.reshape(PEER_HEADS * 2, PEER_NKEYS, LANES).astype(BF16),
        u=peer_u.astype(BF16), v=peer_v.astype(BF16),
    )


def _layer(x, pos, mem_k, mem_v, s0, caches, p, W):
    B, S, d = x.shape
    nt = B * S
    xt = x.reshape(nt, d)
    n1 = _norm(xt, p["ln1_g"])
    zg = _proj("gla", n1, W["w_gla"])
    lg = _gla_gate(n1, W["wa"], W["wg2"], p["gla_bg"])
    cosf, sinf = _rope_tables(pos)
    cosf = jnp.broadcast_to(cosf[None], (B, S, HD)).reshape(nt, HD)
    sinf = jnp.broadcast_to(sinf[None], (B, S, HD)).reshape(nt, HD)
    hg = jnp.stack([p["dil_qn_g"], p["dil_kn_g"]])
    qm = _proj("mem_q", n1, W["w_qm"], extra=(p["mem_qn_g"].reshape(1, HD),))
    gt = _proj("sigmoid", n1, W["w_gt"])

    prompt = caches is None
    og, s_fin = _gla(zg.reshape(B, S, GLA_COLS), lg.reshape(B, S, GLA_HEADS * GLA_DK), s0,
                     GLA_CHUNK if prompt else S)

    new_bufs = []
    gcols = 3 * ATT_COLS
    if prompt:
        parts = []
        for gi, (wd, r) in enumerate(DIL_GROUPS):
            keep = min(wd, S)
            zr, kc, vc = _proj_dil_group(n1, W["w_dil"][:, gi * gcols:(gi + 1) * gcols], hg, cosf, sinf, B, S, r, keep)
            parts.extend(_dil_prompt(zr, S))
            new_bufs.append((kc.reshape(B, keep, HEADS, HD), vc.reshape(B, keep, HEADS, HD)))
        od = _dil_mix(parts)
    else:
        zd3 = _proj("dil", n1, W["w_dil"], extra=(hg, cosf, sinf)).reshape(B, S, DIL_COLS)
        od, new_bufs = _dil_decode(zd3, caches)

    qm3 = qm.reshape(B, S, ATT_COLS)
    if prompt:
        om = _mem_attn(qm3, mem_k, 512).reshape(nt, ATT_COLS)
    else:
        om = _mem_attn_decode(qm3, mem_k, mem_v)

    h, n2, qh = _merge(xt, og.reshape(nt, d), zg, od, om, gt, p["gla_onorm_g"], W["wbg"], W["wbd"], W["wbm"],
                       W["wo"], p["ln2_g"], W["wq"])
    a, b, gw = _peer_topk(qh, W["keys"])
    y = _peer_mix(n2, _peer_weights(a, b, gw, W["u"].shape[0]), h, W["u"], W["v"])
    return y.reshape(B, S, d), s_fin, new_bufs


def kernel(x_prompt, x_sample, mem_prompt, state_gla, cache_dil_k0, cache_dil_v0, cache_dil_k1, cache_dil_v1, cache_dil_k2, cache_dil_v2, cache_mem_k, cache_mem_v, ln1_g, w_in, gla_wg2, gla_bg, gla_onorm_g, dil_qn_g, dil_kn_g, mem_norm_g, w_mem_kv, mem_qn_g, mem_kn_g, w_br_gla, w_br_dil, w_br_mem, w_out, ln2_g, peer_wq, peer_keys, peer_u, peer_v):
    B, S, d = x_prompt.shape
    Bd, T, _ = x_sample.shape
    p = dict(ln1_g=ln1_g, gla_bg=gla_bg, gla_onorm_g=gla_onorm_g, dil_qn_g=dil_qn_g, dil_kn_g=dil_kn_g,
             mem_qn_g=mem_qn_g, ln2_g=ln2_g)
    W = _prep_weights(w_in, gla_wg2, peer_wq, peer_keys, peer_u, peer_v, w_br_gla, w_br_dil, w_br_mem, w_out)

    M = mem_prompt.shape[1]
    mkv = _proj("mem_kv", _norm(mem_prompt.reshape(B * M, d), mem_norm_g), w_mem_kv.astype(BF16),
                extra=(mem_kn_g.reshape(1, HD),))
    mem_k_p = mkv[:, :ATT_COLS].reshape(B, M, HEADS, HD)
    mem_v_p = mkv[:, ATT_COLS:].reshape(B, M, HEADS, HD)

    s0 = jnp.zeros((B, GLA_HEADS, GLA_DK, GLA_DV), F32)
    mkv3 = mkv.reshape(B, M, 2 * ATT_COLS)
    y_prompt, gla_state_p, bufs_p = _layer(x_prompt, jnp.arange(S, dtype=jnp.int32), mkv3, mkv3, s0, None, p, W)

    caches = ((cache_dil_k0, cache_dil_v0), (cache_dil_k1, cache_dil_v1), (cache_dil_k2, cache_dil_v2))
    pos_s = PAST_LEN + jnp.arange(T, dtype=jnp.int32)
    y_sample, gla_state_s, bufs_s = _layer(x_sample, pos_s, cache_mem_k, cache_mem_v, state_gla, caches, p, W)

    (dk0_p, dv0_p), (dk1_p, dv1_p), (dk2_p, dv2_p) = bufs_p
    (dk0_s, dv0_s), (dk1_s, dv1_s), (dk2_s, dv2_s) = bufs_s
    return (y_prompt, y_sample,
            gla_state_p, dk0_p, dv0_p, dk1_p, dv1_p, dk2_p, dv2_p, mem_k_p, mem_v_p,
            gla_state_s, dk0_s, dv0_s, dk1_s, dv1_s, dk2_s, dv2_s)
```

```python
import functools

import jax
import jax.numpy as jnp
from jax import lax
from jax.experimental import pallas as pl
from jax.experimental.pallas import tpu as pltpu

F32 = jnp.float32
BF16 = jnp.bfloat16

D_MODEL = 1024
PAST_LEN = 8192
GLA_HEADS = 4
GLA_DK = 128
GLA_DV = 256
GLA_RANK = 16
GLA_TAU = 16.0
DIL_GROUPS = ((128, 1), (512, 4), (2048, 16))
N_DIL = 3
HEADS = 4
HD = 128
SPAN = 128
MEM_LEN = 256
ROPE_THETA = 10000.0
PEER_HEADS = 8
PEER_NKEYS = 128
PEER_TOPK = 16
EPS = 1e-6
NEG = -1e30

LANES = 128
SUBLANES = 8
GLA_COLS = 2 * GLA_HEADS * GLA_DK + 2 * GLA_HEADS * GLA_DV
DIL_COLS = N_DIL * 3 * HEADS * HD
ATT_COLS = HEADS * HD
VMEM_LIMIT = 56 * 1024 * 1024

PEER_T = 512
PEER_EB = 1024


def _cparams(*sem):
    return pltpu.CompilerParams(dimension_semantics=sem, vmem_limit_bytes=VMEM_LIMIT)


def _rms(x, g):
    return x * lax.rsqrt(jnp.mean(x * x, axis=-1, keepdims=True) + EPS) * g


def _dot(a, b):
    return jnp.dot(a, b, preferred_element_type=F32)


def _dot_nt(a, b):
    return lax.dot_general(a, b, (((1,), (1,)), ((), ())), preferred_element_type=F32)


def _split3(x):
    hi = x.astype(BF16)
    r1 = x - hi.astype(F32)
    mid = r1.astype(BF16)
    lo = (r1 - mid.astype(F32)).astype(BF16)
    return hi, mid, lo


def _norm_body(x_ref, g_ref, n_ref):
    n_ref[...] = _rms(x_ref[...], g_ref[...]).astype(BF16)


def _norm(x, g, tm=1024):
    nt, d = x.shape
    tm = min(tm, nt)
    return pl.pallas_call(
        _norm_body,
        grid=(nt // tm,),
        in_specs=[pl.BlockSpec((tm, d), lambda i: (i, 0)), pl.BlockSpec((1, d), lambda i: (0, 0))],
        out_specs=pl.BlockSpec((tm, d), lambda i: (i, 0)),
        out_shape=jax.ShapeDtypeStruct((nt, d), BF16),
        compiler_params=_cparams("parallel"),
        name="norm",
    )(x, g.reshape(1, d))


def _proj_body(kind, *refs):
    if kind == "dil":
        n_ref, w_ref, hg_ref, cos_ref, sin_ref, o_ref = refs
    elif kind in ("mem_q", "mem_kv"):
        n_ref, w_ref, hg_ref, o_ref = refs
    else:
        n_ref, w_ref, o_ref = refs
    j = pl.program_id(1)
    z = _dot(n_ref[...], w_ref[...])
    if kind == "gla":
        col = lax.broadcasted_iota(jnp.int32, (1, z.shape[1]), 1) + j * z.shape[1]
        o_ref[...] = z * jnp.where(col < GLA_HEADS * GLA_DK, GLA_DK ** -0.5, 1.0).astype(F32)
    elif kind == "sigmoid":
        o_ref[...] = jax.nn.sigmoid(z)
    elif kind == "mem_q":
        hg = hg_ref[...]
        o_ref[...] = jnp.concatenate([_rms(z[:, h * HD:(h + 1) * HD], hg) for h in range(HEADS)], axis=-1)
    elif kind == "mem_kv":
        @pl.when(j == 0)
        def _():
            hg = hg_ref[...]
            o_ref[...] = jnp.concatenate([_rms(z[:, h * HD:(h + 1) * HD], hg) for h in range(HEADS)], axis=-1)

        @pl.when(j != 0)
        def _():
            o_ref[...] = z
    elif kind == "dil":
        c = j % 3

        @pl.when(c == 2)
        def _():
            o_ref[...] = z

        @pl.when(c != 2)
        def _():
            hg = jnp.where(c == 0, hg_ref[0:1, :], hg_ref[1:2, :])
            cs = cos_ref[...]
            sn = sin_ref[...]
            parts = []
            for h in range(HEADS):
                zh = _rms(z[:, h * HD:(h + 1) * HD], hg)
                parts.append(zh * cs + pltpu.roll(zh, HD // 2, 1) * sn)
            o_ref[...] = jnp.concatenate(parts, axis=-1)
    else:
        raise ValueError(kind)


def _proj(kind, n, w, extra=(), tm=1024, tn=1024):
    nt, d = n.shape
    ncol = w.shape[1]
    tm = min(tm, nt)
    if kind in ("dil", "mem_q", "mem_kv"):
        tn = ATT_COLS
    assert nt % tm == 0 and ncol % tn == 0
    in_specs = [
        pl.BlockSpec((tm, d), lambda i, j: (i, 0)),
        pl.BlockSpec((d, tn), lambda i, j: (0, j)),
    ]
    if kind == "dil":
        in_specs += [
            pl.BlockSpec((2, HD), lambda i, j: (0, 0)),
            pl.BlockSpec((tm, HD), lambda i, j: (i, 0)),
            pl.BlockSpec((tm, HD), lambda i, j: (i, 0)),
        ]
    elif kind in ("mem_q", "mem_kv"):
        in_specs += [pl.BlockSpec((1, HD), lambda i, j: (0, 0))]
    return pl.pallas_call(
        functools.partial(_proj_body, kind),
        grid=(nt // tm, ncol // tn),
        in_specs=in_specs,
        out_specs=pl.BlockSpec((tm, tn), lambda i, j: (i, j)),
        out_shape=jax.ShapeDtypeStruct((nt, ncol), F32),
        compiler_params=_cparams("parallel", "arbitrary"),
        name="proj_" + kind,
    )(n, w, *extra)


def _gate_body(n_ref, wa_ref, wg2_ref, bg_ref, o_ref):
    a = _dot(n_ref[...], wa_ref[...])
    pre = _dot(a.astype(BF16), wg2_ref[...]) + bg_ref[...]
    o_ref[...] = jax.nn.log_sigmoid(pre) / GLA_TAU


def _gla_gate(n, wa, wg2, bg, tm=1024):
    nt, d = n.shape
    tm = min(tm, nt)
    ncol = wg2.shape[1]
    return pl.pallas_call(
        _gate_body,
        grid=(nt // tm,),
        in_specs=[
            pl.BlockSpec((tm, d), lambda i: (i, 0)),
            pl.BlockSpec((d, LANES), lambda i: (0, 0)),
            pl.BlockSpec((LANES, ncol), lambda i: (0, 0)),
            pl.BlockSpec((1, ncol), lambda i: (0, 0)),
        ],
        out_specs=pl.BlockSpec((tm, ncol), lambda i: (i, 0)),
        out_shape=jax.ShapeDtypeStruct((nt, ncol), F32),
        compiler_params=_cparams("parallel"),
        name="gla_gate",
    )(n, wa, wg2, bg.reshape(1, ncol))


GLA_CHUNK = 128
GLA_SUB = 16


def _gla_intra(q, k, b, qs_scr, bs_scr, h, n_sub):
    P, SB = GLA_CHUNK, GLA_SUB
    rowp = lax.broadcasted_iota(jnp.int32, (P, 1), 0)
    qparts, kparts = [], []
    for j in range(n_sub - 1):
        lo, hi = SB * j, SB * (j + 1)
        be = b[hi - 1:hi, :]
        qparts.append(jnp.where(rowp >= hi, q * jnp.exp(b - be), 0.0).astype(BF16))
        kblk = (k[lo:hi] * jnp.exp(be - b[lo:hi])).astype(BF16)
        pieces = [kblk]
        if lo:
            pieces.insert(0, jnp.zeros((lo, k.shape[1]), BF16))
        pieces.append(jnp.zeros((P - hi, k.shape[1]), BF16))
        kparts.append(jnp.concatenate(pieces, axis=0))
    if qparts:
        a_off = _dot_nt(jnp.concatenate(qparts, axis=1), jnp.concatenate(kparts, axis=1))
    else:
        a_off = jnp.zeros((P, P), F32)
    HS = SB // 2
    lane = lax.broadcasted_iota(jnp.int32, (HS, P), 1)
    srow = lax.broadcasted_iota(jnp.int32, (HS, P), 0)
    tiles = []
    for j in range(P // SB):
        if j >= n_sub:
            tiles.append(jnp.zeros((SB, P), F32))
            continue
        lo, mid, hi = SB * j, SB * j + HS, SB * (j + 1)
        halves = []
        for s0 in (lo, mid):
            kj, bj = k[s0:s0 + HS], b[s0:s0 + HS]
            at = jnp.zeros((HS, P), F32)
            for t in range(s0, hi):
                bt = bs_scr[h, pl.ds(t, 1), :]
                qt = qs_scr[h, pl.ds(t, 1), :]
                colv = jnp.sum(qt * kj * jnp.exp(jnp.minimum(bt - bj, 0.0)), axis=-1, keepdims=True)
                at = jnp.where(lane == t, colv, at)
            halves.append(jnp.where(srow + s0 <= lane, at, 0.0))
        tiles.extend(halves)
    return a_off + jnp.transpose(jnp.concatenate(tiles, axis=0))


def _gla_body(tc, q_ref, k_ref, v_ref, g_ref, s0_ref, o_ref, sf_ref, s_scr, qp, kp, vp, gp, qs_scr, bs_scr):
    c = pl.program_id(1)
    P = GLA_CHUNK
    H, dk, dv = GLA_HEADS, GLA_DK, GLA_DV
    n_sub = -(-tc // GLA_SUB)

    @pl.when(c == 0)
    def _():
        s_scr[...] = s0_ref[...]

    if tc == P:
        q, k, v, g = q_ref[...], k_ref[...], v_ref[...], g_ref[...]
    else:
        for pad, ref in ((qp, q_ref), (kp, k_ref), (vp, v_ref), (gp, g_ref)):
            pad[...] = jnp.zeros(pad.shape, F32)
            pad[0:tc, :] = ref[...]
        q, k, v, g = qp[...], kp[...], vp[...], gp[...]

    row = lax.broadcasted_iota(jnp.int32, (P, P), 0)
    col = lax.broadcasted_iota(jnp.int32, (P, P), 1)
    trib = (row >= col).astype(BF16)
    g1, g2, g3 = _split3(g)
    b = _dot(trib, g1) + _dot(trib, g2) + _dot(trib, g3)
    b_end = b[P - 1:P, :]
    qe = (q * jnp.exp(b)).astype(BF16)
    kd = k * jnp.exp(b_end - b)
    vb = v.astype(BF16)
    for h in range(H):
        ks = slice(h * dk, (h + 1) * dk)
        qs_scr[h] = q[:, ks]
        bs_scr[h] = b[:, ks]
    for h in range(H):
        ks = slice(h * dk, (h + 1) * dk)
        vs = slice(h * dv, (h + 1) * dv)
        a = _gla_intra(q[:, ks], k[:, ks], b[:, ks], qs_scr, bs_scr, h, n_sub)
        s = s_scr[h]
        o = _dot(qe[:, ks], s.astype(BF16)) + _dot(a.astype(BF16), vb[:, vs])
        o_ref[:, vs] = o[0:tc, :]
        decay = jnp.exp(jnp.sum(jnp.transpose(g[:, ks]), axis=1, keepdims=True))
        s_new = decay * s + _dot(jnp.transpose(kd[:, ks]).astype(BF16), vb[:, vs])
        s_scr[h] = s_new

        @pl.when(c == pl.num_programs(1) - 1)
        def _():
            sf_ref[h] = s_new


def _gla(zg, lg, s0, tc):
    B, S, _ = zg.shape
    H, dk, dv = GLA_HEADS, GLA_DK, GLA_DV
    assert S % tc == 0
    P = GLA_CHUNK
    qk = H * dk
    st = pl.BlockSpec((None, H, dk, dv), lambda b, c: (b, 0, 0, 0))
    return pl.pallas_call(
        functools.partial(_gla_body, tc),
        grid=(B, S // tc),
        in_specs=[
            pl.BlockSpec((None, tc, qk), lambda b, c: (b, c, 0)),
            pl.BlockSpec((None, tc, qk), lambda b, c: (b, c, 1)),
            pl.BlockSpec((None, tc, H * dv), lambda b, c: (b, c, 2 * qk // (H * dv))),
            pl.BlockSpec((None, tc, qk), lambda b, c: (b, c, 0)),
            st,
        ],
        out_specs=[pl.BlockSpec((None, tc, H * dv), lambda b, c: (b, c, 0)), st],
        out_shape=[
            jax.ShapeDtypeStruct((B, S, H * dv), F32),
            jax.ShapeDtypeStruct((B, H, dk, dv), F32),
        ],
        scratch_shapes=[
            pltpu.VMEM((H, dk, dv), F32),
            pltpu.VMEM((P, qk), F32), pltpu.VMEM((P, qk), F32), pltpu.VMEM((P, H * dv), F32), pltpu.VMEM((P, qk), F32),
            pltpu.VMEM((H, P, dk), F32), pltpu.VMEM((H, P, dk), F32),
        ],
        compiler_params=_cparams("parallel", "arbitrary"),
        name="gla",
    )(zg, zg, zg, lg, s0)


def _proj_dil_body(r, spb, nkb, n_ref, w_ref, hg_ref, cos_ref, sin_ref, o_ref, kc_ref, vc_ref, z_scr):
    i = pl.program_id(0)
    j = pl.program_id(1)
    z = _dot(n_ref[...], w_ref[...])

    @pl.when(j == 2)
    def _():
        for h in range(HEADS):
            z_scr[h] = z[:, h * HD:(h + 1) * HD]

    @pl.when(j != 2)
    def _():
        hg = jnp.where(j == 0, hg_ref[0:1, :], hg_ref[1:2, :])
        cs = cos_ref[...]
        sn = sin_ref[...]
        zs = [z[:, h * HD:(h + 1) * HD] for h in range(HEADS)]
        ms = [jnp.mean(zh * zh, axis=-1, keepdims=True) for zh in zs]
        ns = [zh * lax.rsqrt(m + EPS) * hg for zh, m in zip(zs, ms)]
        rs = [pltpu.roll(nh, HD // 2, 1) for nh in ns]
        for h in range(HEADS):
            z_scr[h] = ns[h] * cs + rs[h] * sn

    tm = z_scr.shape[1]
    rows = tm // r
    for rho in range(r):
        for h in range(HEADS):
            zh = z_scr[h, pl.ds(rho, rows, stride=r), :] if r > 1 else z_scr[h]
            o_ref[rho, :, h * HD:(h + 1) * HD] = zh

    kb = kc_ref.shape[0] // HEADS
    for jj, dst in ((1, kc_ref), (2, vc_ref)):
        @pl.when(jnp.logical_and(j == jj, i % spb >= spb - nkb))
        def _():
            for h in range(HEADS):
                dst[pl.ds(h, kb, stride=HEADS), :] = z_scr[h, tm - kb:tm, :]


def _proj_dil_group(n, w, hg, cosf, sinf, B, S, r, keep, tm=1024):
    nt, d = n.shape
    assert S % tm == 0 and tm % (r * SUBLANES) == 0
    spb = S // tm
    kb = min(keep, tm)
    assert keep % kb == 0 and keep <= S
    nkb = keep // kb
    cache_spec = pl.BlockSpec((None, kb * HEADS, HD), lambda i, j: (i // spb, jnp.maximum(i % spb - (spb - nkb), 0), 0))
    cache_shape = jax.ShapeDtypeStruct((B, keep * HEADS, HD), F32)
    return pl.pallas_call(
        functools.partial(_proj_dil_body, r, spb, nkb),
        grid=(nt // tm, 3),
        in_specs=[
            pl.BlockSpec((tm, d), lambda i, j: (i, 0)),
            pl.BlockSpec((d, ATT_COLS), lambda i, j: (0, j)),
            pl.BlockSpec((2, HD), lambda i, j: (0, 0)),
            pl.BlockSpec((tm, HD), lambda i, j: (i, 0)),
            pl.BlockSpec((tm, HD), lambda i, j: (i, 0)),
        ],
        out_specs=[pl.BlockSpec((None, r, tm // r, ATT_COLS), lambda i, j: (i // spb, 0, i % spb, j)),
                   cache_spec, cache_spec],
        out_shape=[jax.ShapeDtypeStruct((B, r, S // r, 3 * ATT_COLS), F32), cache_shape, cache_shape],
        scratch_shapes=[pltpu.VMEM((HEADS, tm, HD), F32)],
        compiler_params=_cparams("arbitrary", "arbitrary"),
        name="proj_dil_r%d" % r,
    )(n, w, hg, cosf, sinf)


DIL_QBLOCKS = 2


def _dil_body(r, q_ref, kc_ref, kp_ref, vc_ref, vp_ref, o_ref, l_ref):
    i = pl.program_id(1)
    rho = pl.program_id(2)
    scale = HD ** -0.5
    QB = q_ref.shape[0]
    mq = lax.broadcasted_iota(jnp.int32, (QB, QB), 0)
    mk = lax.broadcasted_iota(jnp.int32, (QB, QB), 1)
    mask_c = jnp.logical_and(mk <= mq, mq - mk <= SPAN)
    pq = lax.broadcasted_iota(jnp.int32, (QB, SPAN), 0)
    pk = lax.broadcasted_iota(jnp.int32, (QB, SPAN), 1)
    mask_p = jnp.logical_and(pk >= pq, i > 0)
    sls = [slice(h * HD, (h + 1) * HD) for h in range(HEADS)]
    qs = [q_ref[:, sl].astype(BF16) for sl in sls]
    scs = [jnp.where(mask_c, _dot_nt(q, kc_ref[:, sl].astype(BF16)) * scale, NEG) for q, sl in zip(qs, sls)]
    sps = [jnp.where(mask_p, _dot_nt(q, kp_ref[:, sl].astype(BF16)) * scale, NEG) for q, sl in zip(qs, sls)]
    ms = [jnp.maximum(jnp.max(sc, axis=-1, keepdims=True), jnp.max(sp, axis=-1, keepdims=True))
          for sc, sp in zip(scs, sps)]
    ecs = [jnp.exp(sc - m) for sc, m in zip(scs, ms)]
    eps = [jnp.exp(sp - m) for sp, m in zip(sps, ms)]
    dens = [jnp.sum(ec, axis=-1, keepdims=True) + jnp.sum(ep, axis=-1, keepdims=True) for ec, ep in zip(ecs, eps)]
    for h, sl in enumerate(sls):
        o = (_dot((ecs[h] / dens[h]).astype(BF16), vc_ref[:, sl].astype(BF16))
             + _dot((eps[h] / dens[h]).astype(BF16), vp_ref[:, sl].astype(BF16)))
        lse = jnp.broadcast_to(ms[h] + jnp.log(dens[h]), (QB, HD))
        if r == 1:
            o_ref[h] = o
            l_ref[h] = lse
        else:
            o_ref[h, pl.ds(rho, QB, stride=r), :] = o
            l_ref[h, pl.ds(rho, QB, stride=r), :] = lse


def _dil_prompt(zr, S):
    B, r, L, _ = zr.shape
    nq = DIL_QBLOCKS
    QB = nq * SPAN
    assert L % QB == 0

    def cur(c):
        return pl.BlockSpec((None, None, QB, ATT_COLS), lambda b, i, p: (b, p, i, c))

    def prev(c):
        return pl.BlockSpec((None, None, SPAN, ATT_COLS), lambda b, i, p: (b, p, jnp.maximum(nq * i - 1, 0), c))

    out_spec = pl.BlockSpec((None, HEADS, QB * r, HD), lambda b, i, p: (b, 0, i, 0))
    return pl.pallas_call(
        functools.partial(_dil_body, r),
        grid=(B, L // QB, r),
        in_specs=[cur(0), cur(1), prev(1), cur(2), prev(2)],
        out_specs=[out_spec, out_spec],
        out_shape=[jax.ShapeDtypeStruct((B, HEADS, S, HD), F32)] * 2,
        compiler_params=_cparams("parallel", "parallel", "arbitrary"),
        name="dil_prompt_r%d" % r,
    )(zr, zr, zr, zr, zr)


def _dil_mix_body(o0, l0, o1, l1, o2, l2, od_ref):
    for h in range(HEADS):
        a0, a1, a2 = l0[h], l1[h], l2[h]
        m = jnp.maximum(jnp.maximum(a0, a1), a2)
        e0, e1, e2 = jnp.exp(a0 - m), jnp.exp(a1 - m), jnp.exp(a2 - m)
        den = e0 + e1 + e2
        od_ref[:, h * HD:(h + 1) * HD] = (e0 / den) * o0[h] + (e1 / den) * o1[h] + (e2 / den) * o2[h]


def _dil_mix(parts, tm=1024):
    B, _, S, _ = parts[0].shape
    spb = S // tm
    spec = pl.BlockSpec((None, HEADS, tm, HD), lambda i: (i // spb, 0, i % spb, 0))
    return pl.pallas_call(
        _dil_mix_body,
        grid=(B * spb,),
        in_specs=[spec] * 6,
        out_specs=pl.BlockSpec((tm, ATT_COLS), lambda i: (i, 0)),
        out_shape=jax.ShapeDtypeStruct((B * S, ATT_COLS), F32),
        compiler_params=_cparams("parallel"),
        name="dil_mix",
    )(*parts)


def _masked_attn(q, k, v, kn, vn, mask_b, mask_n, scale):
    sb = jnp.where(mask_b, _dot_nt(q, k) * scale, NEG)
    sn = jnp.where(mask_n, _dot_nt(q, kn) * scale, NEG)
    m = jnp.maximum(jnp.max(sb, axis=-1, keepdims=True), jnp.max(sn, axis=-1, keepdims=True))
    eb = jnp.exp(sb - m)
    en = jnp.exp(sn - m)
    den = jnp.sum(eb, axis=-1, keepdims=True) + jnp.sum(en, axis=-1, keepdims=True)
    o = _dot((eb / den).astype(BF16), v) + _dot((en / den).astype(BF16), vn)
    return o, m + jnp.log(den)


def _dil_dec_body(T, q_ref, kn_ref, vn_ref, k0, v0, k1, v1, k2, v2, od_ref, nk0, nv0, nk1, nv1, nk2, nv2):
    scale = HD ** -0.5
    R = T * HEADS
    caches = ((k0, v0, nk0, nv0), (k1, v1, nk1, nv1), (k2, v2, nk2, nv2))
    rn = lax.broadcasted_iota(jnp.int32, (R, R), 0)
    cn = lax.broadcasted_iota(jnp.int32, (R, R), 1)
    outs, lses = [], []
    for gi, (_, r) in enumerate(DIL_GROUPS):
        kc, vc, nkc, nvc = caches[gi]
        rows = kc.shape[0]
        for src, new, dst in ((kc, kn_ref, nkc), (vc, vn_ref, nvc)):
            dst[pl.ds(0, rows - R), :] = src[pl.ds(R, rows - R), :]
            dst[pl.ds(rows - R, R), :] = new[gi]
        if r <= T:
            n = rows
            kb = kc[...].astype(BF16)
            vb = vc[...].astype(BF16)
        else:
            n = rows // (r * HEADS) * R
            kb = kc[...].reshape(rows // (r * HEADS), r * HEADS, HD)[:, 0:R, :].reshape(n, HD).astype(BF16)
            vb = vc[...].reshape(rows // (r * HEADS), r * HEADS, HD)[:, 0:R, :].reshape(n, HD).astype(BF16)
        rb = lax.broadcasted_iota(jnp.int32, (R, n), 0)
        cb = lax.broadcasted_iota(jnp.int32, (R, n), 1)
        if r == 1:
            mask_b = jnp.logical_and(cb % HEADS == rb % HEADS, cb // HEADS >= rb // HEADS)
            mask_n = jnp.logical_and(cn % HEADS == rn % HEADS, cn // HEADS <= rn // HEADS)
        else:
            mask_b = cb % R == rb
            mask_n = cn == rn
        o, lse = _masked_attn(q_ref[gi].astype(BF16), kb, vb, kn_ref[gi].astype(BF16), vn_ref[gi].astype(BF16),
                              mask_b, mask_n, scale)
        outs.append(o)
        lses.append(lse)
    l0, l1, l2 = lses
    m = jnp.maximum(jnp.maximum(l0, l1), l2)
    e0, e1, e2 = jnp.exp(l0 - m), jnp.exp(l1 - m), jnp.exp(l2 - m)
    den = e0 + e1 + e2
    od_ref[...] = (e0 / den) * outs[0] + (e1 / den) * outs[1] + (e2 / den) * outs[2]


def _dil_decode(zd, caches):
    Bd, T, _ = zd.shape
    R = T * HEADS
    z6 = zd.reshape(Bd, T, N_DIL, 3, HEADS, HD)
    qkv = [jnp.transpose(z6[:, :, :, c], (0, 2, 1, 3, 4)).reshape(Bd, N_DIL, R, HD) for c in range(3)]
    new = pl.BlockSpec((None, N_DIL, R, HD), lambda b: (b, 0, 0, 0))
    args, specs = list(qkv), [new, new, new]
    out_specs = [pl.BlockSpec((None, R, HD), lambda b: (b, 0, 0))]
    out_shape = [jax.ShapeDtypeStruct((Bd, R, HD), F32)]
    for (wd, r), (kc, vc) in zip(DIL_GROUPS, caches):
        assert kc.shape[1] == wd and wd > T and R % SUBLANES == 0 and (r <= T or wd % r == 0)
        for a in (kc, vc):
            spec = pl.BlockSpec((None, wd * HEADS, HD), lambda b: (b, 0, 0))
            args.append(a.reshape(Bd, wd * HEADS, HD))
            specs.append(spec)
            out_specs.append(spec)
            out_shape.append(jax.ShapeDtypeStruct((Bd, wd * HEADS, HD), F32))
    od, *rolled = pl.pallas_call(
        functools.partial(_dil_dec_body, T),
        grid=(Bd,),
        in_specs=specs,
        out_specs=out_specs,
        out_shape=out_shape,
        compiler_params=_cparams("parallel"),
        name="dil_decode",
    )(*args)
    bufs = [(rolled[2 * gi].reshape(Bd, wd, HEADS, HD), rolled[2 * gi + 1].reshape(Bd, wd, HEADS, HD))
            for gi, (wd, _) in enumerate(DIL_GROUPS)]
    return od.reshape(Bd * T, ATT_COLS), bufs


def _mem_body(q_ref, k_ref, v_ref, o_ref):
    scale = HD ** -0.5
    for h in range(HEADS):
        sl = slice(h * HD, (h + 1) * HD)
        s = _dot_nt(q_ref[:, sl].astype(BF16), k_ref[:, sl].astype(BF16)) * scale
        e = jnp.exp(s - jnp.max(s, axis=-1, keepdims=True))
        p = e / jnp.sum(e, axis=-1, keepdims=True)
        o_ref[:, sl] = _dot(p.astype(BF16), v_ref[:, sl].astype(BF16))


def _mem_attn(qm, mkv, tq):
    B, S, _ = qm.shape
    M = mkv.shape[1]
    return pl.pallas_call(
        _mem_body,
        grid=(B, S // tq),
        in_specs=[
            pl.BlockSpec((None, tq, ATT_COLS), lambda b, i: (b, i, 0)),
            pl.BlockSpec((None, M, ATT_COLS), lambda b, i: (b, 0, 0)),
            pl.BlockSpec((None, M, ATT_COLS), lambda b, i: (b, 0, 1)),
        ],
        out_specs=pl.BlockSpec((None, tq, ATT_COLS), lambda b, i: (b, i, 0)),
        out_shape=jax.ShapeDtypeStruct((B, S, ATT_COLS), F32),
        compiler_params=_cparams("parallel", "arbitrary"),
        name="mem_attn",
    )(qm, mkv, mkv)


def _mem_dec_body(q_ref, k_ref, v_ref, o_ref):
    R, n = q_ref.shape[0], k_ref.shape[0]
    rb = lax.broadcasted_iota(jnp.int32, (R, n), 0)
    cb = lax.broadcasted_iota(jnp.int32, (R, n), 1)
    s = _dot_nt(q_ref[...].astype(BF16), k_ref[...].astype(BF16)) * (HD ** -0.5)
    s = jnp.where(cb % HEADS == rb % HEADS, s, -jnp.inf)
    e = jnp.exp(s - jnp.max(s, axis=-1, keepdims=True))
    p = e / jnp.sum(e, axis=-1, keepdims=True)
    o_ref[...] = _dot(p.astype(BF16), v_ref[...].astype(BF16))


def _mem_attn_decode(qm, mk, mv):
    Bd, T, _ = qm.shape
    M = mk.shape[1]
    R = T * HEADS
    kv = pl.BlockSpec((None, M * HEADS, HD), lambda b: (b, 0, 0))
    o = pl.pallas_call(
        _mem_dec_body,
        grid=(Bd,),
        in_specs=[pl.BlockSpec((None, R, HD), lambda b: (b, 0, 0)), kv, kv],
        out_specs=pl.BlockSpec((None, R, HD), lambda b: (b, 0, 0)),
        out_shape=jax.ShapeDtypeStruct((Bd, R, HD), F32),
        compiler_params=_cparams("parallel"),
        name="mem_attn_decode",
    )(qm.reshape(Bd, R, HD), mk.reshape(Bd, M * HEADS, HD), mv.reshape(Bd, M * HEADS, HD))
    return o.reshape(Bd * T, ATT_COLS)


def _merge_body(x_ref, og_ref, r_ref, od_ref, om_ref, gt_ref, ong_ref, wbg_ref, wbd_ref, wbm_ref, wo_ref,
                ln2_ref, wq_ref, h_ref, n2_ref, qh_ref):
    og = og_ref[...]
    r = r_ref[...]
    ong = ong_ref[...]
    parts = []
    for h in range(GLA_HEADS):
        sl = slice(h * GLA_DV, (h + 1) * GLA_DV)
        parts.append(_rms(og[:, sl], ong) * jax.nn.silu(r[:, sl]))
    br_gla = _dot(jnp.concatenate(parts, axis=-1).astype(BF16), wbg_ref[...])
    br_dil = _dot(od_ref[...].astype(BF16), wbd_ref[...])
    br_mem = _dot(om_ref[...].astype(BF16), wbm_ref[...])
    d = D_MODEL
    merged = gt_ref[:, 0:d] * br_gla + gt_ref[:, d:2 * d] * br_dil + gt_ref[:, 2 * d:3 * d] * br_mem
    hres = x_ref[...] + _dot(merged.astype(BF16), wo_ref[...])
    h_ref[...] = hres
    n2 = _rms(hres, ln2_ref[...]).astype(BF16)
    n2_ref[...] = n2
    qh_ref[...] = _dot(n2, wq_ref[...])


def _merge(x, og, zg, od, om, gt, ong, wbg, wbd, wbm, wo, ln2, wq, tm=256):
    nt, d = x.shape
    tm = min(tm, nt)
    nq = wq.shape[1]
    rblk = (2 * GLA_HEADS * GLA_DK + GLA_HEADS * GLA_DV) // d

    def tok(cols, blk=0):
        return pl.BlockSpec((tm, cols), lambda i: (i, blk))

    def const(shape):
        return pl.BlockSpec(shape, lambda i: (0, 0), pipeline_mode=pl.Buffered(1))

    return pl.pallas_call(
        _merge_body,
        grid=(nt // tm,),
        in_specs=[
            tok(d), tok(d), tok(d, rblk), tok(ATT_COLS), tok(ATT_COLS), tok(3 * d),
            const((1, GLA_DV)), const(wbg.shape), const(wbd.shape), const(wbm.shape), const(wo.shape),
            const((1, d)), const(wq.shape),
        ],
        out_specs=[tok(d), tok(d), tok(nq)],
        out_shape=[
            jax.ShapeDtypeStruct((nt, d), F32),
            jax.ShapeDtypeStruct((nt, d), BF16),
            jax.ShapeDtypeStruct((nt, nq), F32),
        ],
        compiler_params=_cparams("parallel"),
        name="merge",
    )(x, og, zg, od, om, gt, ong.reshape(1, GLA_DV), wbg, wbd, wbm, wo, ln2.reshape(1, d), wq)


PEER_CAND_GROUPS = 10


def _peer_cand_layout(tp):
    r = lax.broadcasted_iota(jnp.int32, (PEER_CAND_GROUPS * SUBLANES, tp), 0)
    grp = r // SUBLANES
    p = r % SUBLANES
    K = PEER_TOPK
    k1 = jnp.where(grp < 2, 0, jnp.where(grp < 9, grp - 1, SUBLANES + p))
    k2 = jnp.where(grp < 2, r, jnp.where(grp < 9, p, 0))
    valid = (k1 + 1) * (k2 + 1) <= K
    return (k1 * K + k2).astype(F32), valid


def _peer_cands(x1, x2):
    top, bot = x2[0:SUBLANES], x2[SUBLANES:2 * SUBLANES]
    g1 = [jnp.broadcast_to(x1[0:1], top.shape)] * 2 + [jnp.broadcast_to(x1[k:k + 1], top.shape) for k in range(1, 8)]
    g1.append(x1[SUBLANES:2 * SUBLANES])
    g2 = [top, bot] + [top] * 7 + [jnp.broadcast_to(x2[0:1], top.shape)]
    return jnp.concatenate(g1, axis=0), jnp.concatenate(g2, axis=0)


def _peer_topk_body(qh_ref, keys_ref, a_ref, b_ref, gw_ref, s_scr, sv_scr, si_scr, cand_scr, eid_scr, sc_scr, ev_scr):
    tp = qh_ref.shape[0]
    K = PEER_TOPK
    NK = PEER_NKEYS
    NH = PEER_HEADS
    rowf = lax.broadcasted_iota(jnp.int32, (NK, tp), 0).astype(F32)
    pos, valid = _peer_cand_layout(tp)
    for i in range(2 * NH):
        s_scr[i] = _dot_nt(keys_ref[i], qh_ref[:, i * LANES:(i + 1) * LANES].astype(BF16))

    def step1(k, carry):
        for i in range(2 * NH):
            s = s_scr[i]
            mx = jnp.max(s, axis=0, keepdims=True)
            idx = jnp.min(jnp.where(s == mx, rowf, float(NK)), axis=0, keepdims=True)
            s_scr[i] = jnp.where(rowf == idx, -jnp.inf, s)
            sv_scr[i, pl.ds(k, 1), :] = mx
            si_scr[i, pl.ds(k, 1), :] = idx
        return carry

    lax.fori_loop(0, K, step1, 0)

    for h in range(NH):
        c1, c2 = _peer_cands(sv_scr[2 * h], sv_scr[2 * h + 1])
        e1, e2 = _peer_cands(si_scr[2 * h], si_scr[2 * h + 1])
        cand_scr[h] = jnp.where(valid, c1 + c2, -jnp.inf)
        eid_scr[h] = e1 * float(NK) + e2

    def step2(k, carry):
        for h in range(NH):
            cand = cand_scr[h]
            mx = jnp.max(cand, axis=0, keepdims=True)
            first = jnp.min(jnp.where(cand == mx, pos, float(K * K)), axis=0, keepdims=True)
            hit = pos == first
            ev_scr[h, pl.ds(k, 1), :] = jnp.max(jnp.where(hit, eid_scr[h], -1.0), axis=0, keepdims=True)
            sc_scr[h, pl.ds(k, 1), :] = mx
            cand_scr[h] = jnp.where(hit, -jnp.inf, cand)
        return carry

    lax.fori_loop(0, K, step2, 0)

    a_rows, b_rows, g_rows = [], [], []
    for h in range(NH):
        sc, ev = sc_scr[h], ev_scr[h]
        e = jnp.exp(sc - jnp.max(sc, axis=0, keepdims=True))
        g_rows.append(e / jnp.sum(e, axis=0, keepdims=True))
        a = jnp.floor(ev * (1.0 / NK))
        a_rows.append(a)
        b_rows.append(ev - a * float(NK))
    a_ref[...] = jnp.transpose(jnp.concatenate(a_rows, axis=0))
    b_ref[...] = jnp.transpose(jnp.concatenate(b_rows, axis=0))
    gw_ref[...] = jnp.transpose(jnp.concatenate(g_rows, axis=0))


def _peer_topk(qh, keys, tp=128):
    nt, nq = qh.shape
    tp = min(tp, nt)
    spec = pl.BlockSpec((tp, LANES), lambda i: (i, 0))
    shp = jax.ShapeDtypeStruct((nt, LANES), F32)
    K, NH = PEER_TOPK, PEER_HEADS
    ncand = PEER_CAND_GROUPS * SUBLANES
    return pl.pallas_call(
        _peer_topk_body,
        grid=(nt // tp,),
        in_specs=[
            pl.BlockSpec((tp, nq), lambda i: (i, 0)),
            pl.BlockSpec(keys.shape, lambda i: (0, 0, 0)),
        ],
        out_specs=[spec, spec, spec],
        out_shape=[shp, shp, shp],
        scratch_shapes=[
            pltpu.VMEM((2 * NH, PEER_NKEYS, tp), F32), pltpu.VMEM((2 * NH, K, tp), F32), pltpu.VMEM((2 * NH, K, tp), F32),
            pltpu.VMEM((NH, ncand, tp), F32), pltpu.VMEM((NH, ncand, tp), F32),
            pltpu.VMEM((NH, K, tp), F32), pltpu.VMEM((NH, K, tp), F32),
        ],
        compiler_params=_cparams("parallel"),
        name="peer_topk",
    )(qh, keys)


PEER_TW = 128


def _peer_w_block(a_ref, b_ref, gw_ref, base, w_ref, scr):
    tw = PEER_TW
    NK = PEER_NKEYS
    stride = scr.shape[0] // NK
    sub = lax.broadcasted_iota(jnp.int32, (NK, LANES), 0).astype(F32)

    zero = jnp.zeros((NK, LANES), BF16)

    def tok2(p, carry):
        oas, gbs = [], []
        for t in (2 * p, 2 * p + 1):
            arow = a_ref[pl.ds(base + t, 1), :]
            brow = b_ref[pl.ds(base + t, 1), :]
            grow = gw_ref[pl.ds(base + t, 1), :]
            oas.append((arow == sub).astype(BF16))
            gbs.append(jnp.where(brow == sub, grow, 0.0).astype(BF16))
        lhs = jnp.concatenate(oas, axis=1)
        rhs = jnp.concatenate([jnp.concatenate([gbs[0], zero], axis=1),
                               jnp.concatenate([zero, gbs[1]], axis=1)], axis=0)
        w2 = _dot_nt(lhs, rhs)
        scr[pl.ds(2 * p, NK, stride=stride), :] = w2[:, 0:LANES]
        scr[pl.ds(2 * p + 1, NK, stride=stride), :] = w2[:, LANES:2 * LANES]
        return carry

    lax.fori_loop(0, tw // 2, tok2, 0, unroll=32)
    rows = pl.ds(pl.multiple_of(base, tw), tw)
    for i1 in range(NK):
        w_ref[rows, i1 * NK:(i1 + 1) * NK] = scr[i1 * stride:i1 * stride + tw, :].astype(BF16)


def _peer_mix_body(n2_ref, a_ref, b_ref, gw_ref, h_ref, u_ref, v_ref, y_ref, acc, w_scr, scr):
    j = pl.program_id(1)
    T = n2_ref.shape[0]
    EB = u_ref.shape[0]

    @pl.when(j == 0)
    def _():
        acc[...] = jnp.zeros(acc.shape, F32)

        def blk(i, carry):
            _peer_w_block(a_ref, b_ref, gw_ref, i * PEER_TW, w_scr, scr)
            return carry

        lax.fori_loop(0, T // PEER_TW, blk, 0)

    hpre = _dot_nt(n2_ref[...], u_ref[...])
    gelu = 0.5 * hpre * (1.0 + lax.erf(hpre * (2.0 ** -0.5)))
    w = w_scr[:, pl.ds(pl.multiple_of(j * EB, LANES), EB)]
    act = (gelu * w.astype(F32)).astype(BF16)
    acc[...] += _dot(act, v_ref[...])

    @pl.when(j == pl.num_programs(1) - 1)
    def _():
        y_ref[...] = h_ref[...] + acc[...]


def _peer_mix(n2, a, b, gw, h, u, v):
    nt, d = n2.shape
    T = min(PEER_T, nt)
    n_exp = u.shape[0]
    assert nt % T == 0 and n_exp % PEER_EB == 0 and T % PEER_TW == 0
    tok = pl.BlockSpec((T, d), lambda i, j: (i, 0))
    sel = pl.BlockSpec((T, LANES), lambda i, j: (i, 0))
    tab = pl.BlockSpec((PEER_EB, d), lambda i, j: (j, 0))
    stride = PEER_TW + SUBLANES
    return pl.pallas_call(
        _peer_mix_body,
        grid=(nt // T, n_exp // PEER_EB),
        in_specs=[tok, sel, sel, sel, tok, tab, tab],
        out_specs=tok,
        out_shape=jax.ShapeDtypeStruct((nt, d), F32),
        scratch_shapes=[pltpu.VMEM((T, d), F32), pltpu.VMEM((T, n_exp), BF16),
                        pltpu.VMEM((PEER_NKEYS * stride, LANES), F32)],
        compiler_params=_cparams("parallel", "arbitrary"),
        name="peer_mix",
    )(n2, a, b, gw, h, u, v)


def _rope_tables(pos):
    half = HD // 2
    inv = ROPE_THETA ** (-jnp.arange(half, dtype=F32) / half)
    ang = pos.astype(F32)[:, None] * inv[None, :]
    cos, sin = jnp.cos(ang), jnp.sin(ang)
    return jnp.concatenate([cos, cos], axis=-1), jnp.concatenate([-sin, sin], axis=-1)


def _prep_weights(w_in, gla_wg2, peer_wq, peer_keys, peer_u, peer_v, w_br_gla, w_br_dil, w_br_mem, w_out):
    c0 = GLA_COLS
    c1 = c0 + GLA_RANK
    c2 = c1 + DIL_COLS
    c3 = c2 + ATT_COLS
    wa = jnp.pad(w_in[:, c0:c1], ((0, 0), (0, LANES - GLA_RANK))).astype(BF16)
    wg2 = jnp.pad(gla_wg2, ((0, LANES - GLA_RANK), (0, 0))).astype(BF16)
    return dict(
        w_gla=w_in[:, :c0].astype(BF16), wa=wa, wg2=wg2,
        w_dil=w_in[:, c1:c2].astype(BF16), w_qm=w_in[:, c2:c3].astype(BF16), w_gt=w_in[:, c3:].astype(BF16),
        wbg=w_br_gla.astype(BF16), wbd=w_br_dil.astype(BF16), wbm=w_br_mem.astype(BF16), wo=w_out.astype(BF16),
        wq=peer_wq.astype(BF16),
        keys=peer_keys.reshape(PEER_HEADS * 2, PEER_NKEYS, LANES).astype(BF16),
        u=peer_u.astype(BF16), v=peer_v.astype(BF16),
    )


def _layer(x, pos, mem_k, mem_v, s0, caches, p, W):
    B, S, d = x.shape
    nt = B * S
    xt = x.reshape(nt, d)
    n1 = _norm(xt, p["ln1_g"])
    zg = _proj("gla", n1, W["w_gla"])
    lg = _gla_gate(n1, W["wa"], W["wg2"], p["gla_bg"])
    cosf, sinf = _rope_tables(pos)
    cosf = jnp.broadcast_to(cosf[None], (B, S, HD)).reshape(nt, HD)
    sinf = jnp.broadcast_to(sinf[None], (B, S, HD)).reshape(nt, HD)
    hg = jnp.stack([p["dil_qn_g"], p["dil_kn_g"]])
    qm = _proj("mem_q", n1, W["w_qm"], extra=(p["mem_qn_g"].reshape(1, HD),))
    gt = _proj("sigmoid", n1, W["w_gt"])

    prompt = caches is None
    og, s_fin = _gla(zg.reshape(B, S, GLA_COLS), lg.reshape(B, S, GLA_HEADS * GLA_DK), s0,
                     GLA_CHUNK if prompt else S)

    new_bufs = []
    gcols = 3 * ATT_COLS
    if prompt:
        parts = []
        for gi, (wd, r) in enumerate(DIL_GROUPS):
            keep = min(wd, S)
            zr, kc, vc = _proj_dil_group(n1, W["w_dil"][:, gi * gcols:(gi + 1) * gcols], hg, cosf, sinf, B, S, r, keep)
            parts.extend(_dil_prompt(zr, S))
            new_bufs.append((kc.reshape(B, keep, HEADS, HD), vc.reshape(B, keep, HEADS, HD)))
        od = _dil_mix(parts)
    else:
        zd3 = _proj("dil", n1, W["w_dil"], extra=(hg, cosf, sinf)).reshape(B, S, DIL_COLS)
        od, new_bufs = _dil_decode(zd3, caches)

    qm3 = qm.reshape(B, S, ATT_COLS)
    if prompt:
        om = _mem_attn(qm3, mem_k, 512).reshape(nt, ATT_COLS)
    else:
        om = _mem_attn_decode(qm3, mem_k, mem_v)

    h, n2, qh = _merge(xt, og.reshape(nt, d), zg, od, om, gt, p["gla_onorm_g"], W["wbg"], W["wbd"], W["wbm"],
                       W["wo"], p["ln2_g"], W["wq"])
    a, b, gw = _peer_topk(qh, W["keys"])
    y = _peer_mix(n2, a, b, gw, h, W["u"], W["v"])
    return y.reshape(B, S, d), s_fin, new_bufs


def kernel(x_prompt, x_sample, mem_prompt, state_gla, cache_dil_k0, cache_dil_v0, cache_dil_k1, cache_dil_v1, cache_dil_k2, cache_dil_v2, cache_mem_k, cache_mem_v, ln1_g, w_in, gla_wg2, gla_bg, gla_onorm_g, dil_qn_g, dil_kn_g, mem_norm_g, w_mem_kv, mem_qn_g, mem_kn_g, w_br_gla, w_br_dil, w_br_mem, w_out, ln2_g, peer_wq, peer_keys, peer_u, peer_v):
    B, S, d = x_prompt.shape
    Bd, T, _ = x_sample.shape
    p = dict(ln1_g=ln1_g, gla_bg=gla_bg, gla_onorm_g=gla_onorm_g, dil_qn_g=dil_qn_g, dil_kn_g=dil_kn_g,
             mem_qn_g=mem_qn_g, ln2_g=ln2_g)
    W = _prep_weights(w_in, gla_wg2, peer_wq, peer_keys, peer_u, peer_v, w_br_gla, w_br_dil, w_br_mem, w_out)

    M = mem_prompt.shape[1]
    mkv = _proj("mem_kv", _norm(mem_prompt.reshape(B * M, d), mem_norm_g), w_mem_kv.astype(BF16),
                extra=(mem_kn_g.reshape(1, HD),))
    mem_k_p = mkv[:, :ATT_COLS].reshape(B, M, HEADS, HD)
    mem_v_p = mkv[:, ATT_COLS:].reshape(B, M, HEADS, HD)

    s0 = jnp.zeros((B, GLA_HEADS, GLA_DK, GLA_DV), F32)
    mkv3 = mkv.reshape(B, M, 2 * ATT_COLS)
    y_prompt, gla_state_p, bufs_p = _layer(x_prompt, jnp.arange(S, dtype=jnp.int32), mkv3, mkv3, s0, None, p, W)

    caches = ((cache_dil_k0, cache_dil_v0), (cache_dil_k1, cache_dil_v1), (cache_dil_k2, cache_dil_v2))
    pos_s = PAST_LEN + jnp.arange(T, dtype=jnp.int32)
    y_sample, gla_state_s, bufs_s = _layer(x_sample, pos_s, cache_mem_k, cache_mem_v, state_gla, caches, p, W)

    (dk0_p, dv0_p), (dk1_p, dv1_p), (dk2_p, dv2_p) = bufs_p
    (dk0_s, dv0_s), (dk1_s, dv1_s), (dk2_s, dv2_s) = bufs_s
    return (y_prompt, y_sample,
            gla_state_p, dk0_p, dv0_p, dk1_p, dv1_p, dk2_p, dv2_p, mem_k_p, mem_v_p,
            gla_state_s, dk0_s, dv0_s, dk1_s, dv1_s, dk2_s, dv2_s)
```
